```python
import math, functools
import jax, jax.numpy as jnp
from jax import lax
import numpy as np

D_MODEL = 1024
BATCH = 2
SEQ = 8192
DEPTH = 4
DEC_BATCH = 32
DEC_SEQ = 8
PAST_LEN = 8192
PAGE_SIZE = 128

N_BRANCH = 4
BRANCH_W = D_MODEL // 2
HG_HEADS = 4
HG_DK = BRANCH_W // HG_HEADS
HG_DV = BRANCH_W // HG_HEADS
HG_CHUNK = 64
DN_HEADS = 4
DN_DK = BRANCH_W // DN_HEADS
DN_DV = BRANCH_W // DN_HEADS
DN_CONV = 4
DN_CHUNK = 64
SC_W = BRANCH_W
SC_CONV = 3
NSA_HEADS = 8
NSA_KV_HEADS = 2
NSA_HD = BRANCH_W // NSA_HEADS
NSA_GROUP = NSA_HEADS // NSA_KV_HEADS
ROPE_DIM = NSA_HD // 4
ROPE_THETA = 500000.0
CMP_BLOCK = 32
CMP_STRIDE = 16
CMP_HIDDEN = 4 * NSA_HD
SEL_BLOCK = 64
SEL_TOPN = 16
WINDOW = 512
Q_BLOCK = 128
D_FF = 4 * D_MODEL
EPS = 1e-6

IN_SPLITS = (
    ("hg_q", BRANCH_W), ("hg_f", BRANCH_W), ("hg_i", BRANCH_W), ("hg_g", BRANCH_W),
    ("dn_qkv", 3 * BRANCH_W), ("dn_b", DN_HEADS), ("dn_a", DN_HEADS), ("dn_z", BRANCH_W),
    ("sc_bch", 3 * SC_W),
    ("nsa_q", NSA_HEADS * NSA_HD), ("nsa_kv", 6 * NSA_KV_HEADS * NSA_HD), ("nsa_gate", 3 * NSA_HEADS),
    ("merge_gate", N_BRANCH * D_MODEL),
)
IN_NAMES = tuple(n for n, _ in IN_SPLITS)
IN_CUTS = tuple(int(c) for c in np.cumsum([s for _, s in IN_SPLITS])[:-1])
N_IN = sum(s for _, s in IN_SPLITS)

kernel_name = "hybrid_hgrn2_gdn_shortconv_nsa_step"


def rms_norm(x, g):
    xf = x.astype(jnp.float32)
    y = xf * lax.rsqrt(jnp.mean(xf * xf, axis=-1, keepdims=True) + EPS)
    return (y * g.astype(jnp.float32)).astype(x.dtype)


def l2_normalize(x):
    xf = x.astype(jnp.float32)
    return xf * lax.rsqrt(jnp.sum(xf * xf, axis=-1, keepdims=True) + EPS)


def masked_softmax(s, mask):
    s = jnp.where(mask, s.astype(jnp.float32), -jnp.inf)
    m = jnp.max(s, axis=-1, keepdims=True)
    m = jnp.where(jnp.isfinite(m), m, 0.0)
    p = jnp.exp(s - m)
    return p / jnp.maximum(jnp.sum(p, axis=-1, keepdims=True), 1e-30)


def rotary(x, pos):
    half = ROPE_DIM // 2
    inv_freq = ROPE_THETA ** (-jnp.arange(half, dtype=jnp.float32) / half)
    ang = pos.astype(jnp.float32)[:, None] * inv_freq
    cos = jnp.cos(ang)[:, None, :]
    sin = jnp.sin(ang)[:, None, :]
    xf = x.astype(jnp.float32)
    x1, x2 = xf[..., :half], xf[..., half:ROPE_DIM]
    return jnp.concatenate([x1 * cos - x2 * sin, x2 * cos + x1 * sin, xf[..., ROPE_DIM:]], axis=-1).astype(x.dtype)


def causal_conv(x, buf, w):
    k_w = w.shape[0]
    seq_len = x.shape[1]
    xp = jnp.concatenate([buf.astype(x.dtype), x], axis=1)
    y = sum(xp[:, j:j + seq_len] * w[j] for j in range(k_w))
    return y, xp[:, seq_len:]


def to_chunks(a, c):
    b, seq_len = a.shape[:2]
    n = -(-seq_len // c)
    a = jnp.pad(a, [(0, 0), (0, n * c - seq_len)] + [(0, 0)] * (a.ndim - 2))
    return jnp.moveaxis(a.reshape((b, n, c) + a.shape[2:]), 1, 0)


def from_chunks(a, seq_len):
    n, b, c = a.shape[:3]
    return jnp.moveaxis(a, 0, 1).reshape((b, n * c) + a.shape[3:])[:, :seq_len]


def hgrn2_scan(q, k, v, logf, s0):
    seq_len = q.shape[1]
    c = min(HG_CHUNK, seq_len)
    tri = jnp.tril(jnp.ones((c, c), bool))[None, :, :, None, None]

    def step(s, inp):
        qc, kc, vc, gc = inp
        b = jnp.cumsum(gc, axis=1)
        decay = jnp.exp(jnp.where(tri, b[:, :, None] - b[:, None], -jnp.inf))
        att = jnp.einsum('bthk,btshk->btsh', qc, decay * kc[:, None])
        o = jnp.einsum('btsh,bshv->bthv', att, vc) + jnp.einsum('bthk,bhkv->bthv', qc * jnp.exp(b), s)
        b_last = b[:, -1]
        s = jnp.exp(b_last)[..., None] * s + jnp.einsum('bshk,bshv->bhkv', kc * jnp.exp(b_last[:, None] - b), vc)
        return s, o

    xs = tuple(to_chunks(a.astype(jnp.float32), c) for a in (q, k, v, logf))
    s, o = lax.scan(step, s0.astype(jnp.float32), xs)
    return from_chunks(o, seq_len), s


def gated_delta_scan(q, k, v, beta, g, s0):
    seq_len = q.shape[1]
    c = min(DN_CHUNK, seq_len)
    incl = jnp.tril(jnp.ones((c, c), bool))
    strict = jnp.tril(jnp.ones((c, c), bool), -1)
    eye = jnp.eye(c, dtype=jnp.float32)

    def step(s, inp):
        qc, kc, vc, bc, gc = inp
        qh, kh, vh = (jnp.swapaxes(a, 1, 2) for a in (qc, kc, vc))
        bh = jnp.swapaxes(bc, 1, 2)
        gam = jnp.cumsum(jnp.swapaxes(gc, 1, 2), axis=-1)
        decay = jnp.exp(jnp.where(incl, gam[..., :, None] - gam[..., None, :], -jnp.inf))
        kk = jnp.einsum('bhtk,bhsk->bhts', kh, kh)
        t_mat = eye + jnp.where(strict, bh[..., :, None] * kk * decay, 0.0)
        u = lax.linalg.triangular_solve(t_mat, vh * bh[..., None], left_side=True, lower=True)
        w = lax.linalg.triangular_solve(t_mat, kh * (bh * jnp.exp(gam))[..., None], left_side=True, lower=True)
        v_new = u - jnp.einsum('bhtk,bhkv->bhtv', w, s)
        qk = jnp.einsum('bhtk,bhsk->bhts', qh, kh) * decay
        o = jnp.einsum('bhtk,bhkv->bhtv', qh * jnp.exp(gam)[..., None], s) + jnp.einsum('bhts,bhsv->bhtv', qk, v_new)
        g_last = gam[..., -1]
        s = jnp.exp(g_last)[..., None, None] * s + jnp.einsum('bhtk,bhtv->bhkv', kh * jnp.exp(g_last[..., None] - gam)[..., None], v_new)
        return s, jnp.swapaxes(o, 1, 2)

    xs = tuple(to_chunks(a.astype(jnp.float32), c) for a in (q, k, v, beta, g))
    s, o = lax.scan(step, s0.astype(jnp.float32), xs)
    return from_chunks(o, seq_len), s


def compress_blocks(k_seq, pos_emb, w1, w2):
    b, lk = k_seq.shape[:2]
    nc = (lk - CMP_BLOCK) // CMP_STRIDE + 1
    idx = jnp.arange(nc)[:, None] * CMP_STRIDE + jnp.arange(CMP_BLOCK)[None, :]
    blk = k_seq[:, idx] + pos_emb[:, None, :]
    blk = jnp.swapaxes(blk, 2, 3).reshape(b, nc, NSA_KV_HEADS, CMP_BLOCK * NSA_HD)
    return jax.nn.silu(blk @ w1) @ w2


def nsa_attend(q, q_rot, gates, q_pos, kw, vw, kw_pos, ck, cv, c_end, ks_b, vs_b):
    b, nq = q.shape[:2]
    scale = NSA_HD ** -0.5
    qg = q.reshape(b, nq, NSA_KV_HEADS, NSA_GROUP, NSA_HD)
    qrg = q_rot.reshape(b, nq, NSA_KV_HEADS, NSA_GROUP, NSA_HD)
    s_c = jnp.einsum('bqgjd,bngd->bqgjn', qg, ck) * scale
    p_c = masked_softmax(s_c, (c_end[None, :] <= q_pos[:, None])[None, :, None, None, :])
    o_c = jnp.einsum('bqgjn,bngd->bqgjd', p_c.astype(cv.dtype), cv)
    ns = ks_b.shape[2]
    blk_start = jnp.arange(ns) * SEL_BLOCK
    c_start = c_end - (CMP_BLOCK - 1)
    overlap = ((c_start[:, None] <= blk_start[None, :] + SEL_BLOCK - 1) & (c_end[:, None] >= blk_start[None, :])).astype(jnp.float32)
    imp = jnp.einsum('bqgjn,nm->bqgm', p_c, overlap)
    cur = q_pos // SEL_BLOCK
    m = jnp.arange(ns)[None, :]
    valid = blk_start[None, :] <= q_pos[:, None]
    forced = (m == 0) | (m == cur[:, None]) | (m == cur[:, None] - 1)
    imp = jnp.where(forced[None, :, None, :], jnp.inf, jnp.where(valid[None, :, None, :], imp, -jnp.inf))
    n_top = min(SEL_TOPN, ns)
    _, top = lax.top_k(imp, n_top)
    top = jnp.moveaxis(top, 2, 1)
    bi = jnp.arange(b)[:, None, None, None]
    gi = jnp.arange(NSA_KV_HEADS)[None, :, None, None]
    gk = ks_b[bi, gi, top]
    gv = vs_b[bi, gi, top].reshape(b, NSA_KV_HEADS, nq, n_top * SEL_BLOCK, NSA_HD)
    tok = top[..., None] * SEL_BLOCK + jnp.arange(SEL_BLOCK)
    sel_mask = (tok <= q_pos[None, None, :, None, None]).reshape(b, NSA_KV_HEADS, nq, n_top * SEL_BLOCK)
    sel_mask = jnp.moveaxis(sel_mask, 1, 2)[:, :, :, None, :]
    s_s = jnp.einsum('bqgjd,bgqtsd->bqgjts', qrg, gk).reshape(b, nq, NSA_KV_HEADS, NSA_GROUP, n_top * SEL_BLOCK) * scale
    p_s = masked_softmax(s_s, sel_mask)
    o_s = jnp.einsum('bqgjk,bgqkd->bqgjd', p_s.astype(gv.dtype), gv)
    s_w = jnp.einsum('bqgjd,bkgd->bqgjk', qrg, kw) * scale
    w_mask = (kw_pos[None, :] <= q_pos[:, None]) & (kw_pos[None, :] > q_pos[:, None] - WINDOW) & (kw_pos[None, :] >= 0)
    p_w = masked_softmax(s_w, w_mask[None, :, None, None, :])
    o_w = jnp.einsum('bqgjk,bkgd->bqgjd', p_w.astype(vw.dtype), vw)
    g = gates.reshape(b, nq, NSA_KV_HEADS, NSA_GROUP, 3)
    o = g[..., 0:1] * o_c + g[..., 1:2] * o_s + g[..., 2:3] * o_w
    return o.reshape(b, nq, NSA_HEADS * NSA_HD)


def nsa_mixer(q_raw, kv_raw, gate_raw, start, past_rows, win_buf, cmp_pos, cmp_w1, cmp_w2):
    b, seq_len = q_raw.shape[:2]
    pos = start + jnp.arange(seq_len)
    q = q_raw.reshape(b, seq_len, NSA_HEADS, NSA_HD)
    kv = kv_raw.reshape(b, seq_len, 6, NSA_KV_HEADS, NSA_HD)
    q_rot = rotary(q, pos)
    rows = jnp.stack([kv[:, :, 0], kv[:, :, 1], rotary(kv[:, :, 2], pos), kv[:, :, 3]], axis=2)
    win_rows = jnp.stack([rotary(kv[:, :, 4], pos), kv[:, :, 5]], axis=2)
    gates = jax.nn.sigmoid(gate_raw.reshape(b, seq_len, NSA_HEADS, 3))
    seq = rows if past_rows is None else jnp.concatenate([past_rows.astype(rows.dtype), rows], axis=1)
    lk = seq.shape[1]
    ck = compress_blocks(seq[:, :, 0], cmp_pos[0], cmp_w1[0], cmp_w2[0])
    cv = compress_blocks(seq[:, :, 1], cmp_pos[1], cmp_w1[1], cmp_w2[1])
    c_end = jnp.arange(ck.shape[1]) * CMP_STRIDE + CMP_BLOCK - 1
    ns = -(-lk // SEL_BLOCK)
    sel = jnp.pad(seq[:, :, 2:4], ((0, 0), (0, ns * SEL_BLOCK - lk), (0, 0), (0, 0), (0, 0)))
    sel = sel.reshape(b, ns, SEL_BLOCK, 2, NSA_KV_HEADS, NSA_HD).transpose(3, 0, 4, 1, 2, 5)
    attend = functools.partial(nsa_attend, ck=ck, cv=cv, c_end=c_end, ks_b=sel[0], vs_b=sel[1])
    if past_rows is None:
        w_all = jnp.pad(win_rows, ((0, 0), (WINDOW, 0), (0, 0), (0, 0), (0, 0)))

        def block(i):
            s0 = i * Q_BLOCK
            sl = lambda a: lax.dynamic_slice_in_dim(a, s0, Q_BLOCK, axis=1)
            kwb = lax.dynamic_slice_in_dim(w_all, s0, WINDOW + Q_BLOCK, axis=1)
            return attend(sl(q), sl(q_rot), sl(gates), s0 + jnp.arange(Q_BLOCK),
                          kw=kwb[:, :, 0], vw=kwb[:, :, 1], kw_pos=s0 - WINDOW + jnp.arange(WINDOW + Q_BLOCK))

        o = lax.map(block, jnp.arange(seq_len // Q_BLOCK))
        o = jnp.moveaxis(o, 0, 1).reshape(b, seq_len, NSA_HEADS * NSA_HD)
        new_win = win_rows[:, max(seq_len - WINDOW, 0):]
    else:
        nb = win_buf.shape[1]
        w_all = jnp.concatenate([win_buf.astype(win_rows.dtype), win_rows], axis=1)
        o = attend(q, q_rot, gates, pos, kw=w_all[:, :, 0], vw=w_all[:, :, 1], kw_pos=start - nb + jnp.arange(nb + seq_len))
        new_win = w_all[:, seq_len:]
    return o, rows, new_win


def trunk_layer(x, start, hg_s, dn_s, dn_buf, sc_buf, past_rows, win_buf, w):
    b, seq_len, _ = x.shape
    f32 = jnp.float32
    h = rms_norm(x, w["norm_mix"])
    c = dict(zip(IN_NAMES, jnp.split(h @ w["w_in"], IN_CUTS, axis=-1)))
    lb = w["lb"]
    hq = jax.nn.silu(c["hg_q"]).reshape(b, seq_len, HG_HEADS, HG_DK)
    z = c["hg_f"].astype(f32).reshape(b, seq_len, HG_HEADS, HG_DK)
    logf = jnp.logaddexp(jnp.log(lb), jnp.log1p(-lb) + jax.nn.log_sigmoid(z))
    hk = (1.0 - lb) * jax.nn.sigmoid(-z)
    hv = c["hg_i"].reshape(b, seq_len, HG_HEADS, HG_DV)
    o_a, hg_s = hgrn2_scan(hq, hk, hv, logf, hg_s)
    o_a = rms_norm(o_a.astype(x.dtype), w["hg_norm"]) * jax.nn.silu(c["hg_g"].reshape(b, seq_len, HG_HEADS, HG_DV))
    qkv, dn_buf = causal_conv(c["dn_qkv"], dn_buf, w["dn_conv"])
    dq, dk, dv = jnp.split(jax.nn.silu(qkv), 3, axis=-1)
    dq = l2_normalize(dq.reshape(b, seq_len, DN_HEADS, DN_DK)) * DN_DK ** -0.5
    dk = l2_normalize(dk.reshape(b, seq_len, DN_HEADS, DN_DK))
    beta = jax.nn.sigmoid(c["dn_b"].astype(f32))
    g = -jnp.exp(w["dn_a_log"].astype(f32)) * jax.nn.softplus(c["dn_a"].astype(f32) + w["dn_dt_bias"])
    o_b, dn_s = gated_delta_scan(dq, dk, dv.reshape(b, seq_len, DN_HEADS, DN_DV), beta, g, dn_s)
    o_b = rms_norm(o_b.astype(x.dtype), w["dn_norm"]) * jax.nn.silu(c["dn_z"].reshape(b, seq_len, DN_HEADS, DN_DV))
    gb, gc, hx = jnp.split(c["sc_bch"], 3, axis=-1)
    conv, sc_buf = causal_conv(gc * hx, sc_buf, w["sc_conv"])
    o_c = gb * conv
    o_d, rows, new_win = nsa_mixer(c["nsa_q"], c["nsa_kv"], c["nsa_gate"], start, past_rows, win_buf,
                                   w["cmp_pos"], w["cmp_w1"], w["cmp_w2"])
    branches = jnp.stack([o_a.reshape(b, seq_len, BRANCH_W), o_b.reshape(b, seq_len, BRANCH_W), o_c, o_d], axis=2)
    gate = jax.nn.sigmoid(c["merge_gate"].reshape(b, seq_len, N_BRANCH, D_MODEL))
    merged = jnp.sum(gate * jnp.einsum('blnc,ncd->blnd', branches, w["w_branch"]), axis=2)
    x = x + merged @ w["w_out"]
    h2 = rms_norm(x, w["norm_mlp"])
    x = x + jnp.square(jax.nn.relu(h2 @ w["w_up"])) @ w["w_down"]
    return x, (hg_s, dn_s, dn_buf, sc_buf, new_win, rows)


def setup_inputs(seed: int = 0) -> dict:
    key = jax.random.key(seed)
    k = jax.random.split(key, 30)
    f32 = jnp.float32
    nrm = lambda kk, shape, scale: jax.random.normal(kk, shape, f32) * scale
    n_pages = PAST_LEN // PAGE_SIZE
    n_pool = (DEC_BATCH * n_pages * 5) // 4
    w_buf = min(WINDOW, PAST_LEN)
    page_table = jax.random.permutation(k[0], n_pool)[:DEC_BATCH * n_pages].reshape(DEC_BATCH, n_pages).astype(jnp.int32)
    dt = jnp.exp(jax.random.uniform(k[1], (DEPTH, DN_HEADS), f32, math.log(1e-3), math.log(1e-1)))
    return {
        "x_prompt": nrm(k[2], (BATCH, SEQ, D_MODEL), 1.0),
        "x_sample": nrm(k[3], (DEC_BATCH, DEC_SEQ, D_MODEL), 1.0),
        "state_hgrn": nrm(k[4], (DEPTH, DEC_BATCH, HG_HEADS, HG_DK, HG_DV), 0.5),
        "state_dn": nrm(k[5], (DEPTH, DEC_BATCH, DN_HEADS, DN_DK, DN_DV), 0.5),
        "state_dn_conv": nrm(k[6], (DEPTH, DEC_BATCH, DN_CONV - 1, 3 * BRANCH_W), 1.0),
        "state_sc_conv": nrm(k[7], (DEPTH, DEC_BATCH, SC_CONV - 1, SC_W), 1.0),
        "state_win_kv": nrm(k[8], (DEPTH, DEC_BATCH, w_buf, 2, NSA_KV_HEADS, NSA_HD), 1.0),
        "cache_kv": nrm(k[9], (DEPTH, n_pool, PAGE_SIZE, 4, NSA_KV_HEADS, NSA_HD), 1.0),
        "page_table": page_table,
        "norm_mix": 1.0 + nrm(k[10], (DEPTH, D_MODEL), 0.01),
        "norm_mlp": 1.0 + nrm(k[11], (DEPTH, D_MODEL), 0.01),
        "norm_final": 1.0 + nrm(k[12], (D_MODEL,), 0.01),
        "w_in": nrm(k[13], (DEPTH, D_MODEL, N_IN), D_MODEL ** -0.5),
        "hg_lb_logits": nrm(k[14], (DEPTH, HG_HEADS, HG_DK), 0.5),
        "hg_norm": 1.0 + nrm(k[15], (DEPTH, HG_DV), 0.01),
        "dn_conv": nrm(k[16], (DEPTH, DN_CONV, 3 * BRANCH_W), DN_CONV ** -0.5),
        "dn_a_log": jnp.log(jax.random.uniform(k[17], (DEPTH, DN_HEADS), f32, 1.0, 16.0)),
        "dn_dt_bias": dt + jnp.log(-jnp.expm1(-dt)),
        "dn_norm": 1.0 + nrm(k[18], (DEPTH, DN_DV), 0.01),
        "sc_conv": nrm(k[19], (DEPTH, SC_CONV, SC_W), SC_CONV ** -0.5),
        "cmp_pos": nrm(k[20], (DEPTH, 2, CMP_BLOCK, NSA_HD), 0.1),
        "cmp_w1": nrm(k[21], (DEPTH, 2, CMP_BLOCK * NSA_HD, CMP_HIDDEN), (CMP_BLOCK * NSA_HD) ** -0.5),
        "cmp_w2": nrm(k[22], (DEPTH, 2, CMP_HIDDEN, NSA_HD), CMP_HIDDEN ** -0.5),
        "w_branch": nrm(k[23], (DEPTH, N_BRANCH, BRANCH_W, D_MODEL), BRANCH_W ** -0.5),
        "w_out": nrm(k[24], (DEPTH, D_MODEL, D_MODEL), D_MODEL ** -0.5),
        "w_up": nrm(k[25], (DEPTH, D_MODEL, D_FF), D_MODEL ** -0.5),
        "w_down": nrm(k[26], (DEPTH, D_FF, D_MODEL), D_FF ** -0.5),
    }


def reference(x_prompt, x_sample, state_hgrn, state_dn, state_dn_conv, state_sc_conv, state_win_kv, cache_kv,
              page_table, norm_mix, norm_mlp, norm_final, w_in, hg_lb_logits, hg_norm, dn_conv, dn_a_log,
              dn_dt_bias, dn_norm, sc_conv, cmp_pos, cmp_w1, cmp_w2, w_branch, w_out, w_up, w_down):
    f32 = jnp.float32
    lbs = jnp.cumsum(jax.nn.softmax(hg_lb_logits.astype(f32), axis=0), axis=0)
    lbs = lbs - lbs[:1]

    def layer_w(l):
        return dict(norm_mix=norm_mix[l], norm_mlp=norm_mlp[l], w_in=w_in[l], lb=lbs[l], hg_norm=hg_norm[l],
                    dn_conv=dn_conv[l], dn_a_log=dn_a_log[l], dn_dt_bias=dn_dt_bias[l], dn_norm=dn_norm[l],
                    sc_conv=sc_conv[l], cmp_pos=cmp_pos[l], cmp_w1=cmp_w1[l], cmp_w2=cmp_w2[l],
                    w_branch=w_branch[l], w_out=w_out[l], w_up=w_up[l], w_down=w_down[l])

    bp = x_prompt.shape[0]
    yp = x_prompt
    p_st = []
    for l in range(DEPTH):
        yp, st = trunk_layer(yp, 0,
                             jnp.zeros((bp, HG_HEADS, HG_DK, HG_DV), f32),
                             jnp.zeros((bp, DN_HEADS, DN_DK, DN_DV), f32),
                             jnp.zeros((bp, DN_CONV - 1, 3 * BRANCH_W), x_prompt.dtype),
                             jnp.zeros((bp, SC_CONV - 1, SC_W), x_prompt.dtype),
                             None, None, layer_w(l))
        p_st.append(st)
    ys = x_sample
    s_st = []
    for l in range(DEPTH):
        past = cache_kv[l][page_table]
        past = past.reshape(page_table.shape[0], -1, *past.shape[3:])
        ys, st = trunk_layer(ys, past.shape[1], state_hgrn[l], state_dn[l], state_dn_conv[l], state_sc_conv[l],
                             past, state_win_kv[l], layer_w(l))
        s_st.append(st)
    p = [jnp.stack([st[i] for st in p_st]) for i in range(6)]
    s = [jnp.stack([st[i] for st in s_st]) for i in range(6)]
    return (rms_norm(yp, norm_final), rms_norm(ys, norm_final),
            p[0], p[1], p[2], p[3], p[4], p[5],
            s[0], s[1], s[2], s[3], s[4], s[5])
```

```python
import math, functools
import jax, jax.numpy as jnp
from jax import lax
import numpy as np
from jax.experimental import pallas as pl
from jax.experimental.pallas import tpu as pltpu

D_MODEL = 1024
DEPTH = 4
PAGE_SIZE = 128
N_BRANCH = 4
BRANCH_W = D_MODEL // 2
HG_HEADS = 4
HG_DK = BRANCH_W // HG_HEADS
HG_DV = BRANCH_W // HG_HEADS
HG_CHUNK = 64
DN_HEADS = 4
DN_DK = BRANCH_W // DN_HEADS
DN_DV = BRANCH_W // DN_HEADS
DN_CONV = 4
DN_CHUNK = 64
SC_W = BRANCH_W
SC_CONV = 3
NSA_HEADS = 8
NSA_KV_HEADS = 2
NSA_HD = BRANCH_W // NSA_HEADS
NSA_GROUP = NSA_HEADS // NSA_KV_HEADS
ROPE_DIM = NSA_HD // 4
ROPE_THETA = 500000.0
CMP_BLOCK = 32
CMP_STRIDE = 16
CMP_HIDDEN = 4 * NSA_HD
SEL_BLOCK = 64
SEL_TOPN = 16
WINDOW = 512
Q_BLOCK = 128
D_FF = 4 * D_MODEL
EPS = 1e-6

IN_SPLITS = (
    ("hg_q", BRANCH_W), ("hg_f", BRANCH_W), ("hg_i", BRANCH_W), ("hg_g", BRANCH_W),
    ("dn_qkv", 3 * BRANCH_W), ("dn_b", DN_HEADS), ("dn_a", DN_HEADS), ("dn_z", BRANCH_W),
    ("sc_bch", 3 * SC_W),
    ("nsa_q", NSA_HEADS * NSA_HD), ("nsa_kv", 6 * NSA_KV_HEADS * NSA_HD), ("nsa_gate", 3 * NSA_HEADS),
    ("merge_gate", N_BRANCH * D_MODEL),
)
IN_NAMES = tuple(n for n, _ in IN_SPLITS)
IN_CUTS = tuple(int(c) for c in np.cumsum([s for _, s in IN_SPLITS])[:-1])
N_IN = sum(s for _, s in IN_SPLITS)

LANE = 128
PROJ_TN = 512
N_IN_PAD = -(-N_IN // PROJ_TN) * PROJ_TN
VMEM_LIMIT = 48 * 1024 * 1024


def _row_tile(m):
    return 512 if m % 512 == 0 else m


def _norm_proj_kernel(x_ref, g_ref, w_ref, o_ref, h_ref):
    @pl.when(pl.program_id(1) == 0)
    def _():
        x = x_ref[...]
        y = x * lax.rsqrt(jnp.mean(x * x, axis=-1, keepdims=True) + EPS)
        h_ref[...] = (y * g_ref[...]).astype(jnp.bfloat16)

    o_ref[...] = jnp.dot(h_ref[...], w_ref[...], preferred_element_type=jnp.float32)


def _norm_proj(x, g, w_bf16):
    m, d = x.shape
    n = w_bf16.shape[1]
    tm = _row_tile(m)
    return pl.pallas_call(
        _norm_proj_kernel,
        grid=(m // tm, n // PROJ_TN),
        in_specs=[pl.BlockSpec((tm, d), lambda i, j: (i, 0)),
                  pl.BlockSpec((1, d), lambda i, j: (0, 0)),
                  pl.BlockSpec((d, PROJ_TN), lambda i, j: (0, j))],
        out_specs=pl.BlockSpec((tm, PROJ_TN), lambda i, j: (i, j)),
        out_shape=jax.ShapeDtypeStruct((m, n), jnp.float32),
        scratch_shapes=[pltpu.VMEM((tm, d), jnp.bfloat16)],
        compiler_params=pltpu.CompilerParams(dimension_semantics=("arbitrary", "arbitrary"),
                                             vmem_limit_bytes=VMEM_LIMIT),
        name="norm_proj",
    )(x, g.reshape(1, d), w_bf16)


def _merge_kernel(x_ref, ba_ref, bb_ref, bc_ref, bd_ref, gate_ref, wb_ref, wo_ref, o_ref):
    acc = None
    for n, b_ref in enumerate((ba_ref, bb_ref, bc_ref, bd_ref)):
        p = jnp.dot(b_ref[...].astype(jnp.bfloat16), wb_ref[n], preferred_element_type=jnp.float32)
        t = jax.nn.sigmoid(gate_ref[:, n * D_MODEL:(n + 1) * D_MODEL]) * p
        acc = t if acc is None else acc + t
    o_ref[...] = x_ref[...] + jnp.dot(acc.astype(jnp.bfloat16), wo_ref[...], preferred_element_type=jnp.float32)


def _merge_out(x, branches, gate_raw, wb_bf16, wo_bf16):
    m, d = x.shape
    tm = 256 if m % 256 == 0 else m
    row = lambda w: pl.BlockSpec((tm, w), lambda i: (i, 0))
    return pl.pallas_call(
        _merge_kernel,
        grid=(m // tm,),
        in_specs=[row(d)] + [row(BRANCH_W)] * N_BRANCH + [row(N_BRANCH * d),
                  pl.BlockSpec((N_BRANCH, BRANCH_W, d), lambda i: (0, 0, 0)),
                  pl.BlockSpec((d, d), lambda i: (0, 0))],
        out_specs=row(d),
        out_shape=jax.ShapeDtypeStruct((m, d), jnp.float32),
        compiler_params=pltpu.CompilerParams(dimension_semantics=("arbitrary",),
                                             vmem_limit_bytes=VMEM_LIMIT),
        name="merge_out",
    )(x, *branches, gate_raw, wb_bf16, wo_bf16)


MLP_TF = 1024


def _mlp_kernel(x_ref, g_ref, wu_ref, wd_ref, o_ref, h_ref, acc_ref):
    j = pl.program_id(1)

    @pl.when(j == 0)
    def _():
        x = x_ref[...]
        y = x * lax.rsqrt(jnp.mean(x * x, axis=-1, keepdims=True) + EPS)
        h_ref[...] = (y * g_ref[...]).astype(jnp.bfloat16)
        acc_ref[...] = jnp.zeros_like(acc_ref)

    u = jnp.maximum(jnp.dot(h_ref[...], wu_ref[...], preferred_element_type=jnp.float32), 0.0)
    acc_ref[...] += jnp.dot((u * u).astype(jnp.bfloat16), wd_ref[...], preferred_element_type=jnp.float32)

    @pl.when(j == pl.num_programs(1) - 1)
    def _():
        o_ref[...] = x_ref[...] + acc_ref[...]


def _mlp(x, g, wu_bf16, wd_bf16):
    m, d = x.shape
    f = wu_bf16.shape[1]
    tm = _row_tile(m)
    return pl.pallas_call(
        _mlp_kernel,
        grid=(m // tm, f // MLP_TF),
        in_specs=[pl.BlockSpec((tm, d), lambda i, j: (i, 0)),
                  pl.BlockSpec((1, d), lambda i, j: (0, 0)),
                  pl.BlockSpec((d, MLP_TF), lambda i, j: (0, j)),
                  pl.BlockSpec((MLP_TF, d), lambda i, j: (j, 0))],
        out_specs=pl.BlockSpec((tm, d), lambda i, j: (i, 0)),
        out_shape=jax.ShapeDtypeStruct((m, d), jnp.float32),
        scratch_shapes=[pltpu.VMEM((tm, d), jnp.bfloat16), pltpu.VMEM((tm, d), jnp.float32)],
        compiler_params=pltpu.CompilerParams(dimension_semantics=("arbitrary", "arbitrary"),
                                             vmem_limit_bytes=VMEM_LIMIT),
        name="mlp",
    )(x, g.reshape(1, d), wu_bf16, wd_bf16)


def rms_norm(x, g):
    xf = x.astype(jnp.float32)
    y = xf * lax.rsqrt(jnp.mean(xf * xf, axis=-1, keepdims=True) + EPS)
    return (y * g.astype(jnp.float32)).astype(x.dtype)


def l2_normalize(x):
    xf = x.astype(jnp.float32)
    return xf * lax.rsqrt(jnp.sum(xf * xf, axis=-1, keepdims=True) + EPS)


def masked_softmax(s, mask):
    s = jnp.where(mask, s.astype(jnp.float32), -jnp.inf)
    m = jnp.max(s, axis=-1, keepdims=True)
    m = jnp.where(jnp.isfinite(m), m, 0.0)
    p = jnp.exp(s - m)
    return p / jnp.maximum(jnp.sum(p, axis=-1, keepdims=True), 1e-30)


def rotary(x, pos):
    half = ROPE_DIM // 2
    inv_freq = ROPE_THETA ** (-jnp.arange(half, dtype=jnp.float32) / half)
    ang = pos.astype(jnp.float32)[:, None] * inv_freq
    cos = jnp.cos(ang)[:, None, :]
    sin = jnp.sin(ang)[:, None, :]
    xf = x.astype(jnp.float32)
    x1, x2 = xf[..., :half], xf[..., half:ROPE_DIM]
    return jnp.concatenate([x1 * cos - x2 * sin, x2 * cos + x1 * sin, xf[..., ROPE_DIM:]], axis=-1).astype(x.dtype)


def causal_conv(x, buf, w):
    k_w = w.shape[0]
    seq_len = x.shape[1]
    xp = jnp.concatenate([buf.astype(x.dtype), x], axis=1)
    y = sum(xp[:, j:j + seq_len] * w[j] for j in range(k_w))
    return y, xp[:, seq_len:]


def to_chunks(a, c):
    b, seq_len = a.shape[:2]
    n = -(-seq_len // c)
    a = jnp.pad(a, [(0, 0), (0, n * c - seq_len)] + [(0, 0)] * (a.ndim - 2))
    return jnp.moveaxis(a.reshape((b, n, c) + a.shape[2:]), 1, 0)


def from_chunks(a, seq_len):
    n, b, c = a.shape[:3]
    return jnp.moveaxis(a, 0, 1).reshape((b, n * c) + a.shape[3:])[:, :seq_len]


def hgrn2_scan(q, k, v, logf, s0):
    seq_len = q.shape[1]
    c = min(HG_CHUNK, seq_len)
    tri = jnp.tril(jnp.ones((c, c), bool))[None, :, :, None, None]

    def step(s, inp):
        qc, kc, vc, gc = inp
        b = jnp.cumsum(gc, axis=1)
        decay = jnp.exp(jnp.where(tri, b[:, :, None] - b[:, None], -jnp.inf))
        att = jnp.einsum('bthk,btshk->btsh', qc, decay * kc[:, None])
        o = jnp.einsum('btsh,bshv->bthv', att, vc) + jnp.einsum('bthk,bhkv->bthv', qc * jnp.exp(b), s)
        b_last = b[:, -1]
        s = jnp.exp(b_last)[..., None] * s + jnp.einsum('bshk,bshv->bhkv', kc * jnp.exp(b_last[:, None] - b), vc)
        return s, o

    xs = tuple(to_chunks(a.astype(jnp.float32), c) for a in (q, k, v, logf))
    s, o = lax.scan(step, s0.astype(jnp.float32), xs)
    return from_chunks(o, seq_len), s


def gated_delta_scan(q, k, v, beta, g, s0):
    seq_len = q.shape[1]
    c = min(DN_CHUNK, seq_len)
    incl = jnp.tril(jnp.ones((c, c), bool))
    strict = jnp.tril(jnp.ones((c, c), bool), -1)
    eye = jnp.eye(c, dtype=jnp.float32)

    def step(s, inp):
        qc, kc, vc, bc, gc = inp
        qh, kh, vh = (jnp.swapaxes(a, 1, 2) for a in (qc, kc, vc))
        bh = jnp.swapaxes(bc, 1, 2)
        gam = jnp.cumsum(jnp.swapaxes(gc, 1, 2), axis=-1)
        decay = jnp.exp(jnp.where(incl, gam[..., :, None] - gam[..., None, :], -jnp.inf))
        kk = jnp.einsum('bhtk,bhsk->bhts', kh, kh)
        t_mat = eye + jnp.where(strict, bh[..., :, None] * kk * decay, 0.0)
        u = lax.linalg.triangular_solve(t_mat, vh * bh[..., None], left_side=True, lower=True)
        w = lax.linalg.triangular_solve(t_mat, kh * (bh * jnp.exp(gam))[..., None], left_side=True, lower=True)
        v_new = u - jnp.einsum('bhtk,bhkv->bhtv', w, s)
        qk = jnp.einsum('bhtk,bhsk->bhts', qh, kh) * decay
        o = jnp.einsum('bhtk,bhkv->bhtv', qh * jnp.exp(gam)[..., None], s) + jnp.einsum('bhts,bhsv->bhtv', qk, v_new)
        g_last = gam[..., -1]
        s = jnp.exp(g_last)[..., None, None] * s + jnp.einsum('bhtk,bhtv->bhkv', kh * jnp.exp(g_last[..., None] - gam)[..., None], v_new)
        return s, jnp.swapaxes(o, 1, 2)

    xs = tuple(to_chunks(a.astype(jnp.float32), c) for a in (q, k, v, beta, g))
    s, o = lax.scan(step, s0.astype(jnp.float32), xs)
    return from_chunks(o, seq_len), s


def compress_blocks(k_seq, pos_emb, w1, w2):
    b, lk = k_seq.shape[:2]
    nc = (lk - CMP_BLOCK) // CMP_STRIDE + 1
    idx = jnp.arange(nc)[:, None] * CMP_STRIDE + jnp.arange(CMP_BLOCK)[None, :]
    blk = k_seq[:, idx] + pos_emb[:, None, :]
    blk = jnp.swapaxes(blk, 2, 3).reshape(b, nc, NSA_KV_HEADS, CMP_BLOCK * NSA_HD)
    return jax.nn.silu(blk @ w1) @ w2


def nsa_attend(q, q_rot, gates, q_pos, kw, vw, kw_pos, ck, cv, c_end, ks_b, vs_b):
    b, nq = q.shape[:2]
    scale = NSA_HD ** -0.5
    qg = q.reshape(b, nq, NSA_KV_HEADS, NSA_GROUP, NSA_HD)
    qrg = q_rot.reshape(b, nq, NSA_KV_HEADS, NSA_GROUP, NSA_HD)
    s_c = jnp.einsum('bqgjd,bngd->bqgjn', qg, ck) * scale
    p_c = masked_softmax(s_c, (c_end[None, :] <= q_pos[:, None])[None, :, None, None, :])
    o_c = jnp.einsum('bqgjn,bngd->bqgjd', p_c.astype(cv.dtype), cv)
    ns = ks_b.shape[2]
    blk_start = jnp.arange(ns) * SEL_BLOCK
    c_start = c_end - (CMP_BLOCK - 1)
    overlap = ((c_start[:, None] <= blk_start[None, :] + SEL_BLOCK - 1) & (c_end[:, None] >= blk_start[None, :])).astype(jnp.float32)
    imp = jnp.einsum('bqgjn,nm->bqgm', p_c, overlap)
    cur = q_pos // SEL_BLOCK
    m = jnp.arange(ns)[None, :]
    valid = blk_start[None, :] <= q_pos[:, None]
    forced = (m == 0) | (m == cur[:, None]) | (m == cur[:, None] - 1)
    imp = jnp.where(forced[None, :, None, :], jnp.inf, jnp.where(valid[None, :, None, :], imp, -jnp.inf))
    n_top = min(SEL_TOPN, ns)
    _, top = lax.top_k(imp, n_top)
    top = jnp.moveaxis(top, 2, 1)
    bi = jnp.arange(b)[:, None, None, None]
    gi = jnp.arange(NSA_KV_HEADS)[None, :, None, None]
    gk = ks_b[bi, gi, top]
    gv = vs_b[bi, gi, top].reshape(b, NSA_KV_HEADS, nq, n_top * SEL_BLOCK, NSA_HD)
    tok = top[..., None] * SEL_BLOCK + jnp.arange(SEL_BLOCK)
    sel_mask = (tok <= q_pos[None, None, :, None, None]).reshape(b, NSA_KV_HEADS, nq, n_top * SEL_BLOCK)
    sel_mask = jnp.moveaxis(sel_mask, 1, 2)[:, :, :, None, :]
    s_s = jnp.einsum('bqgjd,bgqtsd->bqgjts', qrg, gk).reshape(b, nq, NSA_KV_HEADS, NSA_GROUP, n_top * SEL_BLOCK) * scale
    p_s = masked_softmax(s_s, sel_mask)
    o_s = jnp.einsum('bqgjk,bgqkd->bqgjd', p_s.astype(gv.dtype), gv)
    s_w = jnp.einsum('bqgjd,bkgd->bqgjk', qrg, kw) * scale
    w_mask = (kw_pos[None, :] <= q_pos[:, None]) & (kw_pos[None, :] > q_pos[:, None] - WINDOW) & (kw_pos[None, :] >= 0)
    p_w = masked_softmax(s_w, w_mask[None, :, None, None, :])
    o_w = jnp.einsum('bqgjk,bkgd->bqgjd', p_w.astype(vw.dtype), vw)
    g = gates.reshape(b, nq, NSA_KV_HEADS, NSA_GROUP, 3)
    o = g[..., 0:1] * o_c + g[..., 1:2] * o_s + g[..., 2:3] * o_w
    return o.reshape(b, nq, NSA_HEADS * NSA_HD)


def nsa_mixer(q_raw, kv_raw, gate_raw, start, past_rows, win_buf, cmp_pos, cmp_w1, cmp_w2):
    b, seq_len = q_raw.shape[:2]
    pos = start + jnp.arange(seq_len)
    q = q_raw.reshape(b, seq_len, NSA_HEADS, NSA_HD)
    kv = kv_raw.reshape(b, seq_len, 6, NSA_KV_HEADS, NSA_HD)
    q_rot = rotary(q, pos)
    rows = jnp.stack([kv[:, :, 0], kv[:, :, 1], rotary(kv[:, :, 2], pos), kv[:, :, 3]], axis=2)
    win_rows = jnp.stack([rotary(kv[:, :, 4], pos), kv[:, :, 5]], axis=2)
    gates = jax.nn.sigmoid(gate_raw.reshape(b, seq_len, NSA_HEADS, 3))
    seq = rows if past_rows is None else jnp.concatenate([past_rows.astype(rows.dtype), rows], axis=1)
    lk = seq.shape[1]
    ck = compress_blocks(seq[:, :, 0], cmp_pos[0], cmp_w1[0], cmp_w2[0])
    cv = compress_blocks(seq[:, :, 1], cmp_pos[1], cmp_w1[1], cmp_w2[1])
    c_end = jnp.arange(ck.shape[1]) * CMP_STRIDE + CMP_BLOCK - 1
    ns = -(-lk // SEL_BLOCK)
    sel = jnp.pad(seq[:, :, 2:4], ((0, 0), (0, ns * SEL_BLOCK - lk), (0, 0), (0, 0), (0, 0)))
    sel = sel.reshape(b, ns, SEL_BLOCK, 2, NSA_KV_HEADS, NSA_HD).transpose(3, 0, 4, 1, 2, 5)
    attend = functools.partial(nsa_attend, ck=ck, cv=cv, c_end=c_end, ks_b=sel[0], vs_b=sel[1])
    if past_rows is None:
        w_all = jnp.pad(win_rows, ((0, 0), (WINDOW, 0), (0, 0), (0, 0), (0, 0)))

        def block(i):
            s0 = i * Q_BLOCK
            sl = lambda a: lax.dynamic_slice_in_dim(a, s0, Q_BLOCK, axis=1)
            kwb = lax.dynamic_slice_in_dim(w_all, s0, WINDOW + Q_BLOCK, axis=1)
            return attend(sl(q), sl(q_rot), sl(gates), s0 + jnp.arange(Q_BLOCK),
                          kw=kwb[:, :, 0], vw=kwb[:, :, 1], kw_pos=s0 - WINDOW + jnp.arange(WINDOW + Q_BLOCK))

        o = lax.map(block, jnp.arange(seq_len // Q_BLOCK))
        o = jnp.moveaxis(o, 0, 1).reshape(b, seq_len, NSA_HEADS * NSA_HD)
        new_win = win_rows[:, max(seq_len - WINDOW, 0):]
    else:
        nb = win_buf.shape[1]
        w_all = jnp.concatenate([win_buf.astype(win_rows.dtype), win_rows], axis=1)
        o = attend(q, q_rot, gates, pos, kw=w_all[:, :, 0], vw=w_all[:, :, 1], kw_pos=start - nb + jnp.arange(nb + seq_len))
        new_win = w_all[:, seq_len:]
    return o, rows, new_win


def trunk_layer(x, start, hg_s, dn_s, dn_buf, sc_buf, past_rows, win_buf, w):
    b, seq_len, _ = x.shape
    f32 = jnp.float32
    x2 = x.reshape(b * seq_len, D_MODEL)
    c_all = _norm_proj(x2, w["norm_mix"], w["w_in"]).reshape(b, seq_len, N_IN_PAD)
    c = dict(zip(IN_NAMES, jnp.split(c_all[..., :N_IN], IN_CUTS, axis=-1)))
    lb = w["lb"]
    hq = jax.nn.silu(c["hg_q"]).reshape(b, seq_len, HG_HEADS, HG_DK)
    z = c["hg_f"].astype(f32).reshape(b, seq_len, HG_HEADS, HG_DK)
    logf = jnp.logaddexp(jnp.log(lb), jnp.log1p(-lb) + jax.nn.log_sigmoid(z))
    hk = (1.0 - lb) * jax.nn.sigmoid(-z)
    hv = c["hg_i"].reshape(b, seq_len, HG_HEADS, HG_DV)
    o_a, hg_s = hgrn2_scan(hq, hk, hv, logf, hg_s)
    o_a = rms_norm(o_a.astype(x.dtype), w["hg_norm"]) * jax.nn.silu(c["hg_g"].reshape(b, seq_len, HG_HEADS, HG_DV))
    qkv, dn_buf = causal_conv(c["dn_qkv"], dn_buf, w["dn_conv"])
    dq, dk, dv = jnp.split(jax.nn.silu(qkv), 3, axis=-1)
    dq = l2_normalize(dq.reshape(b, seq_len, DN_HEADS, DN_DK)) * DN_DK ** -0.5
    dk = l2_normalize(dk.reshape(b, seq_len, DN_HEADS, DN_DK))
    beta = jax.nn.sigmoid(c["dn_b"].astype(f32))
    g = -jnp.exp(w["dn_a_log"].astype(f32)) * jax.nn.softplus(c["dn_a"].astype(f32) + w["dn_dt_bias"])
    o_b, dn_s = gated_delta_scan(dq, dk, dv.reshape(b, seq_len, DN_HEADS, DN_DV), beta, g, dn_s)
    o_b = rms_norm(o_b.astype(x.dtype), w["dn_norm"]) * jax.nn.silu(c["dn_z"].reshape(b, seq_len, DN_HEADS, DN_DV))
    gb, gc, hx = jnp.split(c["sc_bch"], 3, axis=-1)
    conv, sc_buf = causal_conv(gc * hx, sc_buf, w["sc_conv"])
    o_c = gb * conv
    o_d, rows, new_win = nsa_mixer(c["nsa_q"], c["nsa_kv"], c["nsa_gate"], start, past_rows, win_buf,
                                   w["cmp_pos"], w["cmp_w1"], w["cmp_w2"])
    m = b * seq_len
    branches = [o_a.reshape(m, BRANCH_W), o_b.reshape(m, BRANCH_W), o_c.reshape(m, BRANCH_W), o_d.reshape(m, BRANCH_W)]
    x2 = _merge_out(x2, branches, c["merge_gate"].reshape(m, N_BRANCH * D_MODEL), w["w_branch"], w["w_out"])
    x2 = _mlp(x2, w["norm_mlp"], w["w_up"], w["w_down"])
    return x2.reshape(b, seq_len, D_MODEL), (hg_s, dn_s, dn_buf, sc_buf, new_win, rows)


def kernel(x_prompt, x_sample, state_hgrn, state_dn, state_dn_conv, state_sc_conv, state_win_kv, cache_kv, page_table, norm_mix, norm_mlp, norm_final, w_in, hg_lb_logits, hg_norm, dn_conv, dn_a_log, dn_dt_bias, dn_norm, sc_conv, cmp_pos, cmp_w1, cmp_w2, w_branch, w_out, w_up, w_down):
    f32 = jnp.float32
    bf16 = jnp.bfloat16
    lbs = jnp.cumsum(jax.nn.softmax(hg_lb_logits.astype(f32), axis=0), axis=0)
    lbs = lbs - lbs[:1]
    w_in_b = jnp.pad(w_in.astype(bf16), ((0, 0), (0, 0), (0, N_IN_PAD - N_IN)))
    w_branch_b, w_out_b, w_up_b, w_down_b = (a.astype(bf16) for a in (w_branch, w_out, w_up, w_down))

    def layer_w(l):
        return dict(norm_mix=norm_mix[l], norm_mlp=norm_mlp[l], w_in=w_in_b[l], lb=lbs[l], hg_norm=hg_norm[l],
                    dn_conv=dn_conv[l], dn_a_log=dn_a_log[l], dn_dt_bias=dn_dt_bias[l], dn_norm=dn_norm[l],
                    sc_conv=sc_conv[l], cmp_pos=cmp_pos[l], cmp_w1=cmp_w1[l], cmp_w2=cmp_w2[l],
                    w_branch=w_branch_b[l], w_out=w_out_b[l], w_up=w_up_b[l], w_down=w_down_b[l])

    bp = x_prompt.shape[0]
    yp = x_prompt
    p_st = []
    for l in range(DEPTH):
        yp, st = trunk_layer(yp, 0,
                             jnp.zeros((bp, HG_HEADS, HG_DK, HG_DV), f32),
                             jnp.zeros((bp, DN_HEADS, DN_DK, DN_DV), f32),
                             jnp.zeros((bp, DN_CONV - 1, 3 * BRANCH_W), x_prompt.dtype),
                             jnp.zeros((bp, SC_CONV - 1, SC_W), x_prompt.dtype),
                             None, None, layer_w(l))
        p_st.append(st)
    ys = x_sample
    s_st = []
    for l in range(DEPTH):
        past = cache_kv[l][page_table]
        past = past.reshape(page_table.shape[0], -1, *past.shape[3:])
        ys, st = trunk_layer(ys, past.shape[1], state_hgrn[l], state_dn[l], state_dn_conv[l], state_sc_conv[l],
                             past, state_win_kv[l], layer_w(l))
        s_st.append(st)
    p = [jnp.stack([st[i] for st in p_st]) for i in range(6)]
    s = [jnp.stack([st[i] for st in s_st]) for i in range(6)]
    return (rms_norm(yp, norm_final), rms_norm(ys, norm_final),
            p[0], p[1], p[2], p[3], p[4], p[5],
            s[0], s[1], s[2], s[3], s[4], s[5])
```

```python
import math, functools
import jax, jax.numpy as jnp
from jax import lax
import numpy as np
from jax.experimental import pallas as pl
from jax.experimental.pallas import tpu as pltpu

D_MODEL = 1024
DEPTH = 4
PAGE_SIZE = 128
N_BRANCH = 4
BRANCH_W = D_MODEL // 2
HG_HEADS = 4
HG_DK = BRANCH_W // HG_HEADS
HG_DV = BRANCH_W // HG_HEADS
HG_CHUNK = 64
DN_HEADS = 4
DN_DK = BRANCH_W // DN_HEADS
DN_DV = BRANCH_W // DN_HEADS
DN_CONV = 4
DN_CHUNK = 64
SC_W = BRANCH_W
SC_CONV = 3
NSA_HEADS = 8
NSA_KV_HEADS = 2
NSA_HD = BRANCH_W // NSA_HEADS
NSA_GROUP = NSA_HEADS // NSA_KV_HEADS
ROPE_DIM = NSA_HD // 4
ROPE_THETA = 500000.0
CMP_BLOCK = 32
CMP_STRIDE = 16
CMP_HIDDEN = 4 * NSA_HD
SEL_BLOCK = 64
SEL_TOPN = 16
WINDOW = 512
Q_BLOCK = 128
D_FF = 4 * D_MODEL
EPS = 1e-6

IN_SPLITS = (
    ("hg_q", BRANCH_W), ("hg_f", BRANCH_W), ("hg_i", BRANCH_W), ("hg_g", BRANCH_W),
    ("dn_qkv", 3 * BRANCH_W), ("dn_b", DN_HEADS), ("dn_a", DN_HEADS), ("dn_z", BRANCH_W),
    ("sc_bch", 3 * SC_W),
    ("nsa_q", NSA_HEADS * NSA_HD), ("nsa_kv", 6 * NSA_KV_HEADS * NSA_HD), ("nsa_gate", 3 * NSA_HEADS),
    ("merge_gate", N_BRANCH * D_MODEL),
)
IN_NAMES = tuple(n for n, _ in IN_SPLITS)
IN_CUTS = tuple(int(c) for c in np.cumsum([s for _, s in IN_SPLITS])[:-1])
N_IN = sum(s for _, s in IN_SPLITS)

LANE = 128
PROJ_TN = 512
VMEM_LIMIT = 48 * 1024 * 1024
MXU_DT = jnp.bfloat16

MISC_W = 2 * LANE
C_ORDER = (("merge_gate", N_BRANCH * D_MODEL), ("hg_q", BRANCH_W), ("hg_f", BRANCH_W), ("hg_i", BRANCH_W),
           ("hg_g", BRANCH_W), ("dn_qkv", 3 * BRANCH_W), ("sc_bch", 3 * SC_W), ("dn_z", BRANCH_W),
           ("nsa_q", NSA_HEADS * NSA_HD), ("nsa_kv", 6 * NSA_KV_HEADS * NSA_HD), ("misc", MISC_W))
C_OFF = {}
_o = 0
for _n, _w in C_ORDER:
    C_OFF[_n] = _o
    _o += _w
N_IN_PAD = _o
assert N_IN_PAD % PROJ_TN == 0
MISC_COLS = (("dn_b", 0, DN_HEADS), ("dn_a", DN_HEADS, DN_HEADS), ("nsa_gate", 2 * DN_HEADS, 3 * NSA_HEADS))
NEG_INF = float("-inf")


def _permute_w_in(w_in):
    src = dict(zip(IN_NAMES, jnp.split(w_in, IN_CUTS, axis=-1)))
    misc = jnp.concatenate([src[n] for n, _, _ in MISC_COLS], axis=-1)
    src["misc"] = jnp.pad(misc, ((0, 0), (0, 0), (0, MISC_W - misc.shape[-1])))
    return jnp.concatenate([src[n] for n, _ in C_ORDER], axis=-1)


def _row_tile(m):
    return 512 if m % 512 == 0 else m


def _norm_proj_kernel(x_ref, g_ref, w_ref, o_ref, h_ref):
    @pl.when(pl.program_id(1) == 0)
    def _():
        x = x_ref[...]
        y = x * lax.rsqrt(jnp.mean(x * x, axis=-1, keepdims=True) + EPS)
        h_ref[...] = (y * g_ref[...]).astype(jnp.bfloat16)

    o_ref[...] = jnp.dot(h_ref[...], w_ref[...], preferred_element_type=jnp.float32)


def _norm_proj(x, g, w_bf16):
    m, d = x.shape
    n = w_bf16.shape[1]
    tm = _row_tile(m)
    return pl.pallas_call(
        _norm_proj_kernel,
        grid=(m // tm, n // PROJ_TN),
        in_specs=[pl.BlockSpec((tm, d), lambda i, j: (i, 0)),
                  pl.BlockSpec((1, d), lambda i, j: (0, 0)),
                  pl.BlockSpec((d, PROJ_TN), lambda i, j: (0, j))],
        out_specs=pl.BlockSpec((tm, PROJ_TN), lambda i, j: (i, j)),
        out_shape=jax.ShapeDtypeStruct((m, n), jnp.float32),
        scratch_shapes=[pltpu.VMEM((tm, d), jnp.bfloat16)],
        compiler_params=pltpu.CompilerParams(dimension_semantics=("arbitrary", "arbitrary"),
                                             vmem_limit_bytes=VMEM_LIMIT),
        name="norm_proj",
    )(x, g.reshape(1, d), w_bf16)


def _merge_kernel(x_ref, ba_ref, bb_ref, bc_ref, bd_ref, gate_ref, wb_ref, wo_ref, o_ref):
    acc = None
    for n, b_ref in enumerate((ba_ref, bb_ref, bc_ref, bd_ref)):
        p = jnp.dot(b_ref[...].astype(jnp.bfloat16), wb_ref[n], preferred_element_type=jnp.float32)
        t = jax.nn.sigmoid(gate_ref[:, n * D_MODEL:(n + 1) * D_MODEL]) * p
        acc = t if acc is None else acc + t
    o_ref[...] = x_ref[...] + jnp.dot(acc.astype(jnp.bfloat16), wo_ref[...], preferred_element_type=jnp.float32)


def _merge_out(x, branches, gate_raw, wb_bf16, wo_bf16):
    m, d = x.shape
    tm = 256 if m % 256 == 0 else m
    row = lambda w: pl.BlockSpec((tm, w), lambda i: (i, 0))
    return pl.pallas_call(
        _merge_kernel,
        grid=(m // tm,),
        in_specs=[row(d)] + [row(BRANCH_W)] * N_BRANCH + [row(N_BRANCH * d),
                  pl.BlockSpec((N_BRANCH, BRANCH_W, d), lambda i: (0, 0, 0)),
                  pl.BlockSpec((d, d), lambda i: (0, 0))],
        out_specs=row(d),
        out_shape=jax.ShapeDtypeStruct((m, d), jnp.float32),
        compiler_params=pltpu.CompilerParams(dimension_semantics=("arbitrary",),
                                             vmem_limit_bytes=VMEM_LIMIT),
        name="merge_out",
    )(x, *branches, gate_raw, wb_bf16, wo_bf16)


MLP_TF = 1024


def _mlp_kernel(x_ref, g_ref, wu_ref, wd_ref, o_ref, h_ref, acc_ref):
    j = pl.program_id(1)

    @pl.when(j == 0)
    def _():
        x = x_ref[...]
        y = x * lax.rsqrt(jnp.mean(x * x, axis=-1, keepdims=True) + EPS)
        h_ref[...] = (y * g_ref[...]).astype(jnp.bfloat16)
        acc_ref[...] = jnp.zeros_like(acc_ref)

    u = jnp.maximum(jnp.dot(h_ref[...], wu_ref[...], preferred_element_type=jnp.float32), 0.0)
    acc_ref[...] += jnp.dot((u * u).astype(jnp.bfloat16), wd_ref[...], preferred_element_type=jnp.float32)

    @pl.when(j == pl.num_programs(1) - 1)
    def _():
        o_ref[...] = x_ref[...] + acc_ref[...]


def _mlp(x, g, wu_bf16, wd_bf16):
    m, d = x.shape
    f = wu_bf16.shape[1]
    tm = _row_tile(m)
    return pl.pallas_call(
        _mlp_kernel,
        grid=(m // tm, f // MLP_TF),
        in_specs=[pl.BlockSpec((tm, d), lambda i, j: (i, 0)),
                  pl.BlockSpec((1, d), lambda i, j: (0, 0)),
                  pl.BlockSpec((d, MLP_TF), lambda i, j: (0, j)),
                  pl.BlockSpec((MLP_TF, d), lambda i, j: (j, 0))],
        out_specs=pl.BlockSpec((tm, d), lambda i, j: (i, 0)),
        out_shape=jax.ShapeDtypeStruct((m, d), jnp.float32),
        scratch_shapes=[pltpu.VMEM((tm, d), jnp.bfloat16), pltpu.VMEM((tm, d), jnp.float32)],
        compiler_params=pltpu.CompilerParams(dimension_semantics=("arbitrary", "arbitrary"),
                                             vmem_limit_bytes=VMEM_LIMIT),
        name="mlp",
    )(x, g.reshape(1, d), wu_bf16, wd_bf16)


NSA_TQ = 128
NSA_TK = 512
NSA_ROWS = NSA_HEADS * NSA_TQ
NSA_WSPAN = WINDOW + NSA_TQ


def _dot(a, b, **kw):
    return jnp.dot(a, b, preferred_element_type=jnp.float32, **kw)


def _dot_nt(a, b):
    return lax.dot_general(a, b, (((1,), (1,)), ((), ())), preferred_element_type=jnp.float32)


def rope_tables(pos):
    half = ROPE_DIM // 2
    inv_freq = ROPE_THETA ** (-jnp.arange(half, dtype=jnp.float32) / half)
    ang = pos.astype(jnp.float32)[:, None] * inv_freq
    cos, sin = jnp.cos(ang), jnp.sin(ang)
    n = pos.shape[0]
    one = jnp.ones((n, NSA_HD - ROPE_DIM), jnp.float32)
    zero = jnp.zeros((n, NSA_HD - ROPE_DIM), jnp.float32)
    z8 = jnp.zeros((n, half), jnp.float32)
    c = jnp.concatenate([cos, cos, one], axis=1)
    s1 = jnp.concatenate([-sin, z8, zero], axis=1)
    s2 = jnp.concatenate([z8, sin, zero], axis=1)
    two = lambda a: jnp.concatenate([a, a], axis=1)
    return two(c), two(s1), two(s2)


def _rope(x, c, s1, s2):
    n = x.shape[-1]
    return x * c + pltpu.roll(x, n - ROPE_DIM // 2, 1) * s1 + pltpu.roll(x, ROPE_DIM // 2, 1) * s2


def _nsa_prep_kernel(kv0_ref, kv1_ref, kv2_ref, c_ref, s1_ref, s2_ref, rows_ref, win_ref, kvb_ref):
    c, s1, s2 = c_ref[...], s1_ref[...], s2_ref[...]
    cmp_kv = kv0_ref[...]
    sel = kv1_ref[...]
    wnd = kv2_ref[...]
    ks = _rope(sel[:, :LANE], c, s1, s2)
    kw = _rope(wnd[:, :LANE], c, s1, s2)
    rows_ref[:, 0:2 * LANE] = cmp_kv
    rows_ref[:, 2 * LANE:3 * LANE] = ks
    rows_ref[:, 3 * LANE:4 * LANE] = sel[:, LANE:]
    win_ref[:, 0:LANE] = kw
    win_ref[:, LANE:2 * LANE] = wnd[:, LANE:]
    kvb_ref[:, 0:LANE] = ks.astype(MXU_DT)
    kvb_ref[:, LANE:2 * LANE] = sel[:, LANE:].astype(MXU_DT)
    kvb_ref[:, 2 * LANE:3 * LANE] = kw.astype(MXU_DT)
    kvb_ref[:, 3 * LANE:4 * LANE] = wnd[:, LANE:].astype(MXU_DT)


def nsa_prep(c_all, kv_col0, tabs, seq_len):
    m = c_all.shape[0]
    tm = 512 if seq_len % 512 == 0 else seq_len
    nlt = seq_len // tm
    cb = kv_col0 // (2 * LANE)
    kv_spec = lambda k: pl.BlockSpec((tm, 2 * LANE), lambda i, k=k: (i, cb + k))
    tab_spec = pl.BlockSpec((tm, LANE), lambda i: (i % nlt, 0))
    return pl.pallas_call(
        _nsa_prep_kernel,
        grid=(m // tm,),
        in_specs=[kv_spec(0), kv_spec(1), kv_spec(2), tab_spec, tab_spec, tab_spec],
        out_specs=[pl.BlockSpec((tm, 4 * LANE), lambda i: (i, 0)),
                   pl.BlockSpec((tm, 2 * LANE), lambda i: (i, 0)),
                   pl.BlockSpec((tm, 4 * LANE), lambda i: (i, 0))],
        out_shape=[jax.ShapeDtypeStruct((m, 4 * LANE), jnp.float32),
                   jax.ShapeDtypeStruct((m, 2 * LANE), jnp.float32),
                   jax.ShapeDtypeStruct((m, 4 * LANE), MXU_DT)],
        compiler_params=pltpu.CompilerParams(dimension_semantics=("arbitrary",), vmem_limit_bytes=VMEM_LIMIT),
        name="nsa_prep",
    )(c_all, c_all, c_all, *tabs)


def _nsa_compress_kernel(x_ref, pos_ref, w1_ref, w2_ref, o_ref):
    nh = x_ref.shape[3]
    half = CMP_BLOCK // 2 * NSA_HD
    acc = None
    for g in range(NSA_KV_HEADS):
        x = x_ref[0, 0, g]
        top = _dot((x + pos_ref[0, 0:1, :]).astype(MXU_DT), w1_ref[0, 0:half, :])
        bot = _dot((x + pos_ref[0, 1:2, :]).astype(MXU_DT), w1_ref[0, half:2 * half, :])
        h = top + pltpu.roll(bot, nh - 1, 0)
        h = h * jax.nn.sigmoid(h)
        t = _dot(h.astype(MXU_DT), w2_ref[0, g])
        acc = t if acc is None else acc + t
    o_ref[0, 0] = acc.astype(o_ref.dtype)


def nsa_compress(x16, cmp_pos, cmp_w1, cmp_w2):
    b, _, g, nh, hw = x16.shape
    pos = cmp_pos.reshape(2, 2, hw)
    w1 = cmp_w1.astype(MXU_DT)
    w2 = cmp_w2.astype(MXU_DT)
    w2p = jnp.zeros((2, g, CMP_HIDDEN, LANE), MXU_DT)
    for gi in range(g):
        w2p = w2p.at[:, gi, :, gi * NSA_HD:(gi + 1) * NSA_HD].set(w2)
    return pl.pallas_call(
        _nsa_compress_kernel,
        grid=(b, 2),
        in_specs=[pl.BlockSpec((1, 1, g, nh, hw), lambda i, j: (i, j, 0, 0, 0)),
                  pl.BlockSpec((1, 2, hw), lambda i, j: (j, 0, 0)),
                  pl.BlockSpec((1, 2 * hw, CMP_HIDDEN), lambda i, j: (j, 0, 0)),
                  pl.BlockSpec((1, g, CMP_HIDDEN, LANE), lambda i, j: (j, 0, 0, 0))],
        out_specs=pl.BlockSpec((1, 1, nh, LANE), lambda i, j: (i, j, 0, 0)),
        out_shape=jax.ShapeDtypeStruct((b, 2, nh, LANE), MXU_DT),
        compiler_params=pltpu.CompilerParams(dimension_semantics=("arbitrary", "arbitrary"), vmem_limit_bytes=VMEM_LIMIT),
        name="nsa_compress",
    )(x16, pos, w1, w2p)


def _pad_heads(q):
    lane = lax.broadcasted_iota(jnp.int32, (q.shape[0], LANE), 1)
    blocks = []
    for h in range(NSA_HEADS):
        blk = q[:, (h // 2) * LANE:(h // 2 + 1) * LANE]
        g = h // NSA_GROUP
        if h % 2 != g:
            blk = pltpu.roll(blk, NSA_HD, 1)
        keep = (lane < NSA_HD) if g == 0 else (lane >= NSA_HD)
        blocks.append(jnp.where(keep, blk, 0.0))
    return jnp.concatenate(blocks, axis=0)


def _softmax_rows(s):
    m = jnp.max(s, axis=-1, keepdims=True)
    m = jnp.where(m == NEG_INF, 0.0, m)
    p = jnp.exp(s - m)
    return p / jnp.maximum(jnp.sum(p, axis=-1, keepdims=True), 1e-30)


def _nsa_attn_kernel(q_ref, gate_ref, c_ref, s1_ref, s2_ref, ck_ref, cv_ref, ks_ref, vs_ref, kw_ref, vw_ref,
                     o_ref, m_ref, l_ref, acc_ref, *, q_start, k_start):
    tq = NSA_TQ
    i = pl.program_id(1)
    q0 = q_start + i * tq
    scale = NSA_HD ** -0.5
    q = q_ref[0] * scale
    c = jnp.concatenate([c_ref[...]] * 4, axis=1)
    s1 = jnp.concatenate([s1_ref[...]] * 4, axis=1)
    s2 = jnp.concatenate([s2_ref[...]] * 4, axis=1)
    q_raw = _pad_heads(q).astype(MXU_DT)
    q_rot = _pad_heads(_rope(q, c, s1, s2)).astype(MXU_DT)

    row = lax.broadcasted_iota(jnp.int32, (NSA_ROWS, 1), 0)
    qpos = q0 + (row & (tq - 1))

    nc = ck_ref.shape[1]
    s_c = _dot_nt(q_raw, ck_ref[0])
    c_end = lax.broadcasted_iota(jnp.int32, (1, nc), 1) * CMP_STRIDE + (CMP_BLOCK - 1)
    p_c = _softmax_rows(jnp.where(c_end <= qpos, s_c, NEG_INF))
    o_c = _dot(p_c.astype(MXU_DT), cv_ref[0])

    psum = jnp.concatenate(
        [sum(p_c[(g * NSA_GROUP + j) * tq:(g * NSA_GROUP + j + 1) * tq] for j in range(NSA_GROUP))
         for g in range(NSA_KV_HEADS)], axis=0)
    ns = LANE
    n_i = lax.broadcasted_iota(jnp.int32, (nc, ns), 0) * CMP_STRIDE
    m_i = lax.broadcasted_iota(jnp.int32, (nc, ns), 1) * SEL_BLOCK
    overlap = ((n_i <= m_i + (SEL_BLOCK - 1)) & (n_i + (CMP_BLOCK - 1) >= m_i)).astype(jnp.float32)
    imp = _dot(psum, overlap, precision=lax.Precision.HIGHEST)
    r2 = lax.broadcasted_iota(jnp.int32, (NSA_KV_HEADS * tq, 1), 0)
    qpos2 = q0 + (r2 & (tq - 1))
    cur = qpos2 >> 6
    blk = lax.broadcasted_iota(jnp.int32, (1, ns), 1)
    forced = (blk == 0) | (blk == cur) | (blk == cur - 1)
    valid = blk * SEL_BLOCK <= qpos2
    v = jnp.where(forced, jnp.inf, jnp.where(valid, imp, NEG_INF))
    blk_f = blk.astype(jnp.float32)
    sel = jnp.zeros(v.shape, jnp.float32)
    for _ in range(SEL_TOPN):
        mx = jnp.max(v, axis=-1, keepdims=True)
        first = jnp.min(jnp.where(v == mx, blk_f, float(ns)), axis=-1, keepdims=True)
        pick = blk_f == first
        sel = jnp.where(pick, 1.0, sel)
        v = jnp.where(pick, NEG_INF, v)
    sel_b = sel.astype(MXU_DT)

    m_ref[...] = jnp.full(m_ref.shape, NEG_INF, jnp.float32)
    l_ref[...] = jnp.zeros(l_ref.shape, jnp.float32)
    acc_ref[...] = jnp.zeros(acc_ref.shape, jnp.float32)
    n_kt = (q0 + tq - k_start + NSA_TK - 1) // NSA_TK

    def body(kt, carry):
        koff = pl.multiple_of(kt * NSA_TK, NSA_TK)
        k = ks_ref[0, pl.ds(koff, NSA_TK), :]
        vv = vs_ref[0, pl.ds(koff, NSA_TK), :]
        s = _dot_nt(q_rot, k)
        tok = k_start + koff + lax.broadcasted_iota(jnp.int32, (1, NSA_TK), 1)
        e_m = lax.broadcasted_iota(jnp.int32, (ns, NSA_TK), 0)
        e_t = (k_start + koff + lax.broadcasted_iota(jnp.int32, (ns, NSA_TK), 1)) >> 6
        chosen = _dot(sel_b, (e_m == e_t).astype(MXU_DT))
        bias2 = jnp.where((chosen > 0.5) & (tok <= qpos2), 0.0, NEG_INF)
        bias = jnp.concatenate([bias2[0:tq]] * NSA_GROUP + [bias2[tq:2 * tq]] * NSA_GROUP, axis=0)
        s = s + bias
        m_old = m_ref[...]
        m_new = jnp.maximum(m_old, jnp.max(s, axis=-1, keepdims=True))
        m_safe = jnp.where(m_new == NEG_INF, 0.0, m_new)
        alpha = jnp.exp(m_old - m_safe)
        p = jnp.exp(s - m_safe)
        l_ref[...] = alpha * l_ref[...] + jnp.sum(p, axis=-1, keepdims=True)
        acc_ref[...] = alpha * acc_ref[...] + _dot(p.astype(MXU_DT), vv)
        m_ref[...] = m_new
        return carry

    lax.fori_loop(0, n_kt, body, 0)
    o_s = acc_ref[...] / jnp.maximum(l_ref[...], 1e-30)

    w0 = jnp.maximum(q0 - k_start - WINDOW, 0)
    w0 = pl.multiple_of(w0, tq)
    kw = kw_ref[0, pl.ds(w0, NSA_WSPAN), :]
    vw = vw_ref[0, pl.ds(w0, NSA_WSPAN), :]
    s_w = _dot_nt(q_rot, kw)
    kpos = k_start + w0 + lax.broadcasted_iota(jnp.int32, (1, NSA_WSPAN), 1)
    w_ok = (kpos <= qpos) & (kpos > qpos - WINDOW)
    p_w = _softmax_rows(jnp.where(w_ok, s_w, NEG_INF))
    o_w = _dot(p_w.astype(MXU_DT), vw)

    gates = jax.nn.sigmoid(gate_ref[0])
    lane = lax.broadcasted_iota(jnp.int32, (tq, LANE), 1)
    outs = []
    for h in range(NSA_HEADS):
        sl = slice(h * tq, (h + 1) * tq)
        k0 = 2 * DN_HEADS + 3 * h
        o_h = (gates[:, k0:k0 + 1] * o_c[sl] + gates[:, k0 + 1:k0 + 2] * o_s[sl] + gates[:, k0 + 2:k0 + 3] * o_w[sl])
        if h % 2 != h // NSA_GROUP:
            o_h = pltpu.roll(o_h, NSA_HD, 1)
        outs.append(o_h)
    for p2 in range(NSA_HEADS // 2):
        o_ref[0, :, p2 * LANE:(p2 + 1) * LANE] = jnp.where(lane < NSA_HD, outs[2 * p2], outs[2 * p2 + 1])


def nsa_attend_prompt(c_all, q_col0, misc_col0, tabs, ckv, kvb, batch, seq_len):
    tq = NSA_TQ
    nqt = seq_len // tq
    qb = q_col0 // (4 * LANE)
    mb = misc_col0 // LANE
    c3 = c_all.reshape(batch, seq_len, c_all.shape[1])
    kv3 = kvb.reshape(batch, seq_len, 4 * LANE)
    tab_spec = pl.BlockSpec((tq, LANE), lambda b, i: (i, 0))
    kv_spec = lambda k: pl.BlockSpec((1, seq_len, LANE), lambda b, i, k=k: (b, 0, k))
    nc = ckv.shape[2]
    out = pl.pallas_call(
        functools.partial(_nsa_attn_kernel, q_start=0, k_start=0),
        grid=(batch, nqt),
        in_specs=[pl.BlockSpec((1, tq, 4 * LANE), lambda b, i: (b, i, qb)),
                  pl.BlockSpec((1, tq, LANE), lambda b, i: (b, i, mb)),
                  tab_spec, tab_spec, tab_spec,
                  pl.BlockSpec((None, 1, nc, LANE), lambda b, i: (b, 0, 0, 0)),
                  pl.BlockSpec((None, 1, nc, LANE), lambda b, i: (b, 1, 0, 0)),
                  kv_spec(0), kv_spec(1), kv_spec(2), kv_spec(3)],
        out_specs=pl.BlockSpec((1, tq, 4 * LANE), lambda b, i: (b, i, 0)),
        out_shape=jax.ShapeDtypeStruct((batch, seq_len, 4 * LANE), jnp.float32),
        scratch_shapes=[pltpu.VMEM((NSA_ROWS, 1), jnp.float32), pltpu.VMEM((NSA_ROWS, 1), jnp.float32),
                        pltpu.VMEM((NSA_ROWS, LANE), jnp.float32)],
        compiler_params=pltpu.CompilerParams(dimension_semantics=("arbitrary", "arbitrary"), vmem_limit_bytes=VMEM_LIMIT),
        name="nsa_attn",
    )(c3, c3, *tabs, ckv, ckv, kv3, kv3, kv3, kv3)
    return out.reshape(batch * seq_len, 4 * LANE)


def x16_from_rows(rows, batch, seq_len):
    r = rows.reshape(batch, seq_len // CMP_STRIDE, CMP_STRIDE, 4, NSA_KV_HEADS, NSA_HD)[:, :, :, 0:2]
    r = jnp.transpose(r, (0, 3, 4, 1, 2, 5))
    return r.reshape(batch, 2, NSA_KV_HEADS, seq_len // CMP_STRIDE, CMP_STRIDE * NSA_HD)


def rms_norm(x, g):
    xf = x.astype(jnp.float32)
    y = xf * lax.rsqrt(jnp.mean(xf * xf, axis=-1, keepdims=True) + EPS)
    return (y * g.astype(jnp.float32)).astype(x.dtype)


def l2_normalize(x):
    xf = x.astype(jnp.float32)
    return xf * lax.rsqrt(jnp.sum(xf * xf, axis=-1, keepdims=True) + EPS)


def masked_softmax(s, mask):
    s = jnp.where(mask, s.astype(jnp.float32), -jnp.inf)
    m = jnp.max(s, axis=-1, keepdims=True)
    m = jnp.where(jnp.isfinite(m), m, 0.0)
    p = jnp.exp(s - m)
    return p / jnp.maximum(jnp.sum(p, axis=-1, keepdims=True), 1e-30)


def rotary(x, pos):
    half = ROPE_DIM // 2
    inv_freq = ROPE_THETA ** (-jnp.arange(half, dtype=jnp.float32) / half)
    ang = pos.astype(jnp.float32)[:, None] * inv_freq
    cos = jnp.cos(ang)[:, None, :]
    sin = jnp.sin(ang)[:, None, :]
    xf = x.astype(jnp.float32)
    x1, x2 = xf[..., :half], xf[..., half:ROPE_DIM]
    return jnp.concatenate([x1 * cos - x2 * sin, x2 * cos + x1 * sin, xf[..., ROPE_DIM:]], axis=-1).astype(x.dtype)


def causal_conv(x, buf, w):
    k_w = w.shape[0]
    seq_len = x.shape[1]
    xp = jnp.concatenate([buf.astype(x.dtype), x], axis=1)
    y = sum(xp[:, j:j + seq_len] * w[j] for j in range(k_w))
    return y, xp[:, seq_len:]


def to_chunks(a, c):
    b, seq_len = a.shape[:2]
    n = -(-seq_len // c)
    a = jnp.pad(a, [(0, 0), (0, n * c - seq_len)] + [(0, 0)] * (a.ndim - 2))
    return jnp.moveaxis(a.reshape((b, n, c) + a.shape[2:]), 1, 0)


def from_chunks(a, seq_len):
    n, b, c = a.shape[:3]
    return jnp.moveaxis(a, 0, 1).reshape((b, n * c) + a.shape[3:])[:, :seq_len]


def hgrn2_scan(q, k, v, logf, s0):
    seq_len = q.shape[1]
    c = min(HG_CHUNK, seq_len)
    tri = jnp.tril(jnp.ones((c, c), bool))[None, :, :, None, None]

    def step(s, inp):
        qc, kc, vc, gc = inp
        b = jnp.cumsum(gc, axis=1)
        decay = jnp.exp(jnp.where(tri, b[:, :, None] - b[:, None], -jnp.inf))
        att = jnp.einsum('bthk,btshk->btsh', qc, decay * kc[:, None])
        o = jnp.einsum('btsh,bshv->bthv', att, vc) + jnp.einsum('bthk,bhkv->bthv', qc * jnp.exp(b), s)
        b_last = b[:, -1]
        s = jnp.exp(b_last)[..., None] * s + jnp.einsum('bshk,bshv->bhkv', kc * jnp.exp(b_last[:, None] - b), vc)
        return s, o

    xs = tuple(to_chunks(a.astype(jnp.float32), c) for a in (q, k, v, logf))
    s, o = lax.scan(step, s0.astype(jnp.float32), xs)
    return from_chunks(o, seq_len), s


def gated_delta_scan(q, k, v, beta, g, s0):
    seq_len = q.shape[1]
    c = min(DN_CHUNK, seq_len)
    incl = jnp.tril(jnp.ones((c, c), bool))
    strict = jnp.tril(jnp.ones((c, c), bool), -1)
    eye = jnp.eye(c, dtype=jnp.float32)

    def step(s, inp):
        qc, kc, vc, bc, gc = inp
        qh, kh, vh = (jnp.swapaxes(a, 1, 2) for a in (qc, kc, vc))
        bh = jnp.swapaxes(bc, 1, 2)
        gam = jnp.cumsum(jnp.swapaxes(gc, 1, 2), axis=-1)
        decay = jnp.exp(jnp.where(incl, gam[..., :, None] - gam[..., None, :], -jnp.inf))
        kk = jnp.einsum('bhtk,bhsk->bhts', kh, kh)
        t_mat = eye + jnp.where(strict, bh[..., :, None] * kk * decay, 0.0)
        u = lax.linalg.triangular_solve(t_mat, vh * bh[..., None], left_side=True, lower=True)
        w = lax.linalg.triangular_solve(t_mat, kh * (bh * jnp.exp(gam))[..., None], left_side=True, lower=True)
        v_new = u - jnp.einsum('bhtk,bhkv->bhtv', w, s)
        qk = jnp.einsum('bhtk,bhsk->bhts', qh, kh) * decay
        o = jnp.einsum('bhtk,bhkv->bhtv', qh * jnp.exp(gam)[..., None], s) + jnp.einsum('bhts,bhsv->bhtv', qk, v_new)
        g_last = gam[..., -1]
        s = jnp.exp(g_last)[..., None, None] * s + jnp.einsum('bhtk,bhtv->bhkv', kh * jnp.exp(g_last[..., None] - gam)[..., None], v_new)
        return s, jnp.swapaxes(o, 1, 2)

    xs = tuple(to_chunks(a.astype(jnp.float32), c) for a in (q, k, v, beta, g))
    s, o = lax.scan(step, s0.astype(jnp.float32), xs)
    return from_chunks(o, seq_len), s


def compress_blocks(k_seq, pos_emb, w1, w2):
    b, lk = k_seq.shape[:2]
    nc = (lk - CMP_BLOCK) // CMP_STRIDE + 1
    idx = jnp.arange(nc)[:, None] * CMP_STRIDE + jnp.arange(CMP_BLOCK)[None, :]
    blk = k_seq[:, idx] + pos_emb[:, None, :]
    blk = jnp.swapaxes(blk, 2, 3).reshape(b, nc, NSA_KV_HEADS, CMP_BLOCK * NSA_HD)
    return jax.nn.silu(blk @ w1) @ w2


def nsa_attend(q, q_rot, gates, q_pos, kw, vw, kw_pos, ck, cv, c_end, ks_b, vs_b):
    b, nq = q.shape[:2]
    scale = NSA_HD ** -0.5
    qg = q.reshape(b, nq, NSA_KV_HEADS, NSA_GROUP, NSA_HD)
    qrg = q_rot.reshape(b, nq, NSA_KV_HEADS, NSA_GROUP, NSA_HD)
    s_c = jnp.einsum('bqgjd,bngd->bqgjn', qg, ck) * scale
    p_c = masked_softmax(s_c, (c_end[None, :] <= q_pos[:, None])[None, :, None, None, :])
    o_c = jnp.einsum('bqgjn,bngd->bqgjd', p_c.astype(cv.dtype), cv)
    ns = ks_b.shape[2]
    blk_start = jnp.arange(ns) * SEL_BLOCK
    c_start = c_end - (CMP_BLOCK - 1)
    overlap = ((c_start[:, None] <= blk_start[None, :] + SEL_BLOCK - 1) & (c_end[:, None] >= blk_start[None, :])).astype(jnp.float32)
    imp = jnp.einsum('bqgjn,nm->bqgm', p_c, overlap)
    cur = q_pos // SEL_BLOCK
    m = jnp.arange(ns)[None, :]
    valid = blk_start[None, :] <= q_pos[:, None]
    forced = (m == 0) | (m == cur[:, None]) | (m == cur[:, None] - 1)
    imp = jnp.where(forced[None, :, None, :], jnp.inf, jnp.where(valid[None, :, None, :], imp, -jnp.inf))
    n_top = min(SEL_TOPN, ns)
    _, top = lax.top_k(imp, n_top)
    top = jnp.moveaxis(top, 2, 1)
    bi = jnp.arange(b)[:, None, None, None]
    gi = jnp.arange(NSA_KV_HEADS)[None, :, None, None]
    gk = ks_b[bi, gi, top]
    gv = vs_b[bi, gi, top].reshape(b, NSA_KV_HEADS, nq, n_top * SEL_BLOCK, NSA_HD)
    tok = top[..., None] * SEL_BLOCK + jnp.arange(SEL_BLOCK)
    sel_mask = (tok <= q_pos[None, None, :, None, None]).reshape(b, NSA_KV_HEADS, nq, n_top * SEL_BLOCK)
    sel_mask = jnp.moveaxis(sel_mask, 1, 2)[:, :, :, None, :]
    s_s = jnp.einsum('bqgjd,bgqtsd->bqgjts', qrg, gk).reshape(b, nq, NSA_KV_HEADS, NSA_GROUP, n_top * SEL_BLOCK) * scale
    p_s = masked_softmax(s_s, sel_mask)
    o_s = jnp.einsum('bqgjk,bgqkd->bqgjd', p_s.astype(gv.dtype), gv)
    s_w = jnp.einsum('bqgjd,bkgd->bqgjk', qrg, kw) * scale
    w_mask = (kw_pos[None, :] <= q_pos[:, None]) & (kw_pos[None, :] > q_pos[:, None] - WINDOW) & (kw_pos[None, :] >= 0)
    p_w = masked_softmax(s_w, w_mask[None, :, None, None, :])
    o_w = jnp.einsum('bqgjk,bkgd->bqgjd', p_w.astype(vw.dtype), vw)
    g = gates.reshape(b, nq, NSA_KV_HEADS, NSA_GROUP, 3)
    o = g[..., 0:1] * o_c + g[..., 1:2] * o_s + g[..., 2:3] * o_w
    return o.reshape(b, nq, NSA_HEADS * NSA_HD)


def nsa_mixer(q_raw, kv_raw, gate_raw, start, past_rows, win_buf, cmp_pos, cmp_w1, cmp_w2):
    b, seq_len = q_raw.shape[:2]
    pos = start + jnp.arange(seq_len)
    q = q_raw.reshape(b, seq_len, NSA_HEADS, NSA_HD)
    kv = kv_raw.reshape(b, seq_len, 6, NSA_KV_HEADS, NSA_HD)
    q_rot = rotary(q, pos)
    rows = jnp.stack([kv[:, :, 0], kv[:, :, 1], rotary(kv[:, :, 2], pos), kv[:, :, 3]], axis=2)
    win_rows = jnp.stack([rotary(kv[:, :, 4], pos), kv[:, :, 5]], axis=2)
    gates = jax.nn.sigmoid(gate_raw.reshape(b, seq_len, NSA_HEADS, 3))
    seq = jnp.concatenate([past_rows.astype(rows.dtype), rows], axis=1)
    lk = seq.shape[1]
    ck = compress_blocks(seq[:, :, 0], cmp_pos[0], cmp_w1[0], cmp_w2[0])
    cv = compress_blocks(seq[:, :, 1], cmp_pos[1], cmp_w1[1], cmp_w2[1])
    c_end = jnp.arange(ck.shape[1]) * CMP_STRIDE + CMP_BLOCK - 1
    ns = -(-lk // SEL_BLOCK)
    sel = jnp.pad(seq[:, :, 2:4], ((0, 0), (0, ns * SEL_BLOCK - lk), (0, 0), (0, 0), (0, 0)))
    sel = sel.reshape(b, ns, SEL_BLOCK, 2, NSA_KV_HEADS, NSA_HD).transpose(3, 0, 4, 1, 2, 5)
    attend = functools.partial(nsa_attend, ck=ck, cv=cv, c_end=c_end, ks_b=sel[0], vs_b=sel[1])
    nb = win_buf.shape[1]
    w_all = jnp.concatenate([win_buf.astype(win_rows.dtype), win_rows], axis=1)
    o = attend(q, q_rot, gates, pos, kw=w_all[:, :, 0], vw=w_all[:, :, 1], kw_pos=start - nb + jnp.arange(nb + seq_len))
    return o, rows, w_all[:, seq_len:]


def trunk_layer(x, start, hg_s, dn_s, dn_buf, sc_buf, past_rows, win_buf, w):
    b, seq_len, _ = x.shape
    f32 = jnp.float32
    x2 = x.reshape(b * seq_len, D_MODEL)
    c_all = _norm_proj(x2, w["norm_mix"], w["w_in"]).reshape(b, seq_len, N_IN_PAD)
    c = {n: c_all[..., C_OFF[n]:C_OFF[n] + wd] for n, wd in C_ORDER}
    for n, o0, wd in MISC_COLS:
        c[n] = c["misc"][..., o0:o0 + wd]
    lb = w["lb"]
    hq = jax.nn.silu(c["hg_q"]).reshape(b, seq_len, HG_HEADS, HG_DK)
    z = c["hg_f"].astype(f32).reshape(b, seq_len, HG_HEADS, HG_DK)
    logf = jnp.logaddexp(jnp.log(lb), jnp.log1p(-lb) + jax.nn.log_sigmoid(z))
    hk = (1.0 - lb) * jax.nn.sigmoid(-z)
    hv = c["hg_i"].reshape(b, seq_len, HG_HEADS, HG_DV)
    o_a, hg_s = hgrn2_scan(hq, hk, hv, logf, hg_s)
    o_a = rms_norm(o_a.astype(x.dtype), w["hg_norm"]) * jax.nn.silu(c["hg_g"].reshape(b, seq_len, HG_HEADS, HG_DV))
    qkv, dn_buf = causal_conv(c["dn_qkv"], dn_buf, w["dn_conv"])
    dq, dk, dv = jnp.split(jax.nn.silu(qkv), 3, axis=-1)
    dq = l2_normalize(dq.reshape(b, seq_len, DN_HEADS, DN_DK)) * DN_DK ** -0.5
    dk = l2_normalize(dk.reshape(b, seq_len, DN_HEADS, DN_DK))
    beta = jax.nn.sigmoid(c["dn_b"].astype(f32))
    g = -jnp.exp(w["dn_a_log"].astype(f32)) * jax.nn.softplus(c["dn_a"].astype(f32) + w["dn_dt_bias"])
    o_b, dn_s = gated_delta_scan(dq, dk, dv.reshape(b, seq_len, DN_HEADS, DN_DV), beta, g, dn_s)
    o_b = rms_norm(o_b.astype(x.dtype), w["dn_norm"]) * jax.nn.silu(c["dn_z"].reshape(b, seq_len, DN_HEADS, DN_DV))
    gb, gc, hx = jnp.split(c["sc_bch"], 3, axis=-1)
    conv, sc_buf = causal_conv(gc * hx, sc_buf, w["sc_conv"])
    o_c = gb * conv
    m = b * seq_len
    if past_rows is None:
        c2 = c_all.reshape(m, N_IN_PAD)
        tabs = rope_tables(start + jnp.arange(seq_len))
        rows, win_rows, kvb = nsa_prep(c2, C_OFF["nsa_kv"], tabs, seq_len)
        ckv = nsa_compress(x16_from_rows(rows, b, seq_len), w["cmp_pos"], w["cmp_w1"], w["cmp_w2"])
        o_d = nsa_attend_prompt(c2, C_OFF["nsa_q"], C_OFF["misc"], tabs, ckv, kvb, b, seq_len)
        rows = rows.reshape(b, seq_len, 4, NSA_KV_HEADS, NSA_HD)
        new_win = win_rows.reshape(b, seq_len, 2, NSA_KV_HEADS, NSA_HD)[:, max(seq_len - WINDOW, 0):]
    else:
        o_d, rows, new_win = nsa_mixer(c["nsa_q"], c["nsa_kv"], c["nsa_gate"], start, past_rows, win_buf,
                                       w["cmp_pos"], w["cmp_w1"], w["cmp_w2"])
    branches = [o_a.reshape(m, BRANCH_W), o_b.reshape(m, BRANCH_W), o_c.reshape(m, BRANCH_W), o_d.reshape(m, BRANCH_W)]
    x2 = _merge_out(x2, branches, c["merge_gate"].reshape(m, N_BRANCH * D_MODEL), w["w_branch"], w["w_out"])
    x2 = _mlp(x2, w["norm_mlp"], w["w_up"], w["w_down"])
    return x2.reshape(b, seq_len, D_MODEL), (hg_s, dn_s, dn_buf, sc_buf, new_win, rows)


def kernel(x_prompt, x_sample, state_hgrn, state_dn, state_dn_conv, state_sc_conv, state_win_kv, cache_kv, page_table, norm_mix, norm_mlp, norm_final, w_in, hg_lb_logits, hg_norm, dn_conv, dn_a_log, dn_dt_bias, dn_norm, sc_conv, cmp_pos, cmp_w1, cmp_w2, w_branch, w_out, w_up, w_down):
    f32 = jnp.float32
    bf16 = jnp.bfloat16
    lbs = jnp.cumsum(jax.nn.softmax(hg_lb_logits.astype(f32), axis=0), axis=0)
    lbs = lbs - lbs[:1]
    w_in_b = _permute_w_in(w_in).astype(bf16)
    w_branch_b, w_out_b, w_up_b, w_down_b = (a.astype(bf16) for a in (w_branch, w_out, w_up, w_down))

    def layer_w(l):
        return dict(norm_mix=norm_mix[l], norm_mlp=norm_mlp[l], w_in=w_in_b[l], lb=lbs[l], hg_norm=hg_norm[l],
                    dn_conv=dn_conv[l], dn_a_log=dn_a_log[l], dn_dt_bias=dn_dt_bias[l], dn_norm=dn_norm[l],
                    sc_conv=sc_conv[l], cmp_pos=cmp_pos[l], cmp_w1=cmp_w1[l], cmp_w2=cmp_w2[l],
                    w_branch=w_branch_b[l], w_out=w_out_b[l], w_up=w_up_b[l], w_down=w_down_b[l])

    bp = x_prompt.shape[0]
    yp = x_prompt
    p_st = []
    for l in range(DEPTH):
        yp, st = trunk_layer(yp, 0,
                             jnp.zeros((bp, HG_HEADS, HG_DK, HG_DV), f32),
                             jnp.zeros((bp, DN_HEADS, DN_DK, DN_DV), f32),
                             jnp.zeros((bp, DN_CONV - 1, 3 * BRANCH_W), x_prompt.dtype),
                             jnp.zeros((bp, SC_CONV - 1, SC_W), x_prompt.dtype),
                             None, None, layer_w(l))
        p_st.append(st)
    ys = x_sample
    s_st = []
    for l in range(DEPTH):
        past = cache_kv[l][page_table]
        past = past.reshape(page_table.shape[0], -1, *past.shape[3:])
        ys, st = trunk_layer(ys, past.shape[1], state_hgrn[l], state_dn[l], state_dn_conv[l], state_sc_conv[l],
                             past, state_win_kv[l], layer_w(l))
        s_st.append(st)
    p = [jnp.stack([st[i] for st in p_st]) for i in range(6)]
    s = [jnp.stack([st[i] for st in s_st]) for i in range(6)]
    return (rms_norm(yp, norm_final), rms_norm(ys, norm_final),
            p[0], p[1], p[2], p[3], p[4], p[5],
            s[0], s[1], s[2], s[3], s[4], s[5])
```

```python
import math, functools
import jax, jax.numpy as jnp
from jax import lax
import numpy as np
from jax.experimental import pallas as pl
from jax.experimental.pallas import tpu as pltpu

D_MODEL = 1024
DEPTH = 4
PAGE_SIZE = 128
N_BRANCH = 4
BRANCH_W = D_MODEL // 2
HG_HEADS = 4
HG_DK = BRANCH_W // HG_HEADS
HG_DV = BRANCH_W // HG_HEADS
HG_CHUNK = 64
DN_HEADS = 4
DN_DK = BRANCH_W // DN_HEADS
DN_DV = BRANCH_W // DN_HEADS
DN_CONV = 4
DN_CHUNK = 64
SC_W = BRANCH_W
SC_CONV = 3
NSA_HEADS = 8
NSA_KV_HEADS = 2
NSA_HD = BRANCH_W // NSA_HEADS
NSA_GROUP = NSA_HEADS // NSA_KV_HEADS
ROPE_DIM = NSA_HD // 4
ROPE_THETA = 500000.0
CMP_BLOCK = 32
CMP_STRIDE = 16
CMP_HIDDEN = 4 * NSA_HD
SEL_BLOCK = 64
SEL_TOPN = 16
WINDOW = 512
Q_BLOCK = 128
D_FF = 4 * D_MODEL
EPS = 1e-6

IN_SPLITS = (
    ("hg_q", BRANCH_W), ("hg_f", BRANCH_W), ("hg_i", BRANCH_W), ("hg_g", BRANCH_W),
    ("dn_qkv", 3 * BRANCH_W), ("dn_b", DN_HEADS), ("dn_a", DN_HEADS), ("dn_z", BRANCH_W),
    ("sc_bch", 3 * SC_W),
    ("nsa_q", NSA_HEADS * NSA_HD), ("nsa_kv", 6 * NSA_KV_HEADS * NSA_HD), ("nsa_gate", 3 * NSA_HEADS),
    ("merge_gate", N_BRANCH * D_MODEL),
)
IN_NAMES = tuple(n for n, _ in IN_SPLITS)
IN_CUTS = tuple(int(c) for c in np.cumsum([s for _, s in IN_SPLITS])[:-1])
N_IN = sum(s for _, s in IN_SPLITS)

LANE = 128
PROJ_TN = 512
VMEM_LIMIT = 48 * 1024 * 1024
MXU_DT = jnp.bfloat16

MISC_W = 2 * LANE
C_ORDER = (("merge_gate", N_BRANCH * D_MODEL), ("hg_q", BRANCH_W), ("hg_f", BRANCH_W), ("hg_i", BRANCH_W),
           ("hg_g", BRANCH_W), ("dn_qkv", 3 * BRANCH_W), ("sc_bch", 3 * SC_W), ("dn_z", BRANCH_W),
           ("nsa_q", NSA_HEADS * NSA_HD), ("nsa_kv", 6 * NSA_KV_HEADS * NSA_HD), ("misc", MISC_W))
C_OFF = {}
_o = 0
for _n, _w in C_ORDER:
    C_OFF[_n] = _o
    _o += _w
N_IN_PAD = _o
assert N_IN_PAD % PROJ_TN == 0
MISC_COLS = (("dn_b", 0, DN_HEADS), ("dn_a", DN_HEADS, DN_HEADS), ("nsa_gate", 2 * DN_HEADS, 3 * NSA_HEADS))
NEG_INF = float("-inf")


def _permute_w_in(w_in):
    src = dict(zip(IN_NAMES, jnp.split(w_in, IN_CUTS, axis=-1)))
    misc = jnp.concatenate([src[n] for n, _, _ in MISC_COLS], axis=-1)
    src["misc"] = jnp.pad(misc, ((0, 0), (0, 0), (0, MISC_W - misc.shape[-1])))
    return jnp.concatenate([src[n] for n, _ in C_ORDER], axis=-1)


def _row_tile(m):
    return 512 if m % 512 == 0 else m


def _norm_proj_kernel(x_ref, g_ref, w_ref, o_ref, h_ref):
    @pl.when(pl.program_id(1) == 0)
    def _():
        x = x_ref[...]
        y = x * lax.rsqrt(jnp.mean(x * x, axis=-1, keepdims=True) + EPS)
        h_ref[...] = (y * g_ref[...]).astype(jnp.bfloat16)

    o_ref[...] = jnp.dot(h_ref[...], w_ref[...], preferred_element_type=jnp.float32)


def _norm_proj(x, g, w_bf16):
    m, d = x.shape
    n = w_bf16.shape[1]
    tm = _row_tile(m)
    return pl.pallas_call(
        _norm_proj_kernel,
        grid=(m // tm, n // PROJ_TN),
        in_specs=[pl.BlockSpec((tm, d), lambda i, j: (i, 0)),
                  pl.BlockSpec((1, d), lambda i, j: (0, 0)),
                  pl.BlockSpec((d, PROJ_TN), lambda i, j: (0, j))],
        out_specs=pl.BlockSpec((tm, PROJ_TN), lambda i, j: (i, j)),
        out_shape=jax.ShapeDtypeStruct((m, n), jnp.float32),
        scratch_shapes=[pltpu.VMEM((tm, d), jnp.bfloat16)],
        compiler_params=pltpu.CompilerParams(dimension_semantics=("arbitrary", "arbitrary"),
                                             vmem_limit_bytes=VMEM_LIMIT),
        name="norm_proj",
    )(x, g.reshape(1, d), w_bf16)


def _merge_kernel(x_ref, ba_ref, bb_ref, bc_ref, bd_ref, gate_ref, wb_ref, wo_ref, o_ref):
    acc = None
    for n, b_ref in enumerate((ba_ref, bb_ref, bc_ref, bd_ref)):
        p = jnp.dot(b_ref[...].astype(jnp.bfloat16), wb_ref[n], preferred_element_type=jnp.float32)
        t = jax.nn.sigmoid(gate_ref[:, n * D_MODEL:(n + 1) * D_MODEL]) * p
        acc = t if acc is None else acc + t
    o_ref[...] = x_ref[...] + jnp.dot(acc.astype(jnp.bfloat16), wo_ref[...], preferred_element_type=jnp.float32)


def _merge_out(x, branches, c_all, gate_col, wb_bf16, wo_bf16):
    m, d = x.shape
    tm = 256 if m % 256 == 0 else m
    row = lambda w: pl.BlockSpec((tm, w), lambda i: (i, 0))
    return pl.pallas_call(
        _merge_kernel,
        grid=(m // tm,),
        in_specs=[row(d)] + [row(BRANCH_W)] * N_BRANCH + [
                  pl.BlockSpec((tm, N_BRANCH * d), lambda i: (i, gate_col // (N_BRANCH * d))),
                  pl.BlockSpec((N_BRANCH, BRANCH_W, d), lambda i: (0, 0, 0)),
                  pl.BlockSpec((d, d), lambda i: (0, 0))],
        out_specs=row(d),
        out_shape=jax.ShapeDtypeStruct((m, d), jnp.float32),
        compiler_params=pltpu.CompilerParams(dimension_semantics=("arbitrary",),
                                             vmem_limit_bytes=VMEM_LIMIT),
        name="merge_out",
    )(x, *branches, c_all, wb_bf16, wo_bf16)


MLP_TF = 1024


def _mlp_kernel(x_ref, g_ref, wu_ref, wd_ref, o_ref, h_ref, acc_ref):
    j = pl.program_id(1)

    @pl.when(j == 0)
    def _():
        x = x_ref[...]
        y = x * lax.rsqrt(jnp.mean(x * x, axis=-1, keepdims=True) + EPS)
        h_ref[...] = (y * g_ref[...]).astype(jnp.bfloat16)
        acc_ref[...] = jnp.zeros_like(acc_ref)

    u = jnp.maximum(jnp.dot(h_ref[...], wu_ref[...], preferred_element_type=jnp.float32), 0.0)
    acc_ref[...] += jnp.dot((u * u).astype(jnp.bfloat16), wd_ref[...], preferred_element_type=jnp.float32)

    @pl.when(j == pl.num_programs(1) - 1)
    def _():
        o_ref[...] = x_ref[...] + acc_ref[...]


def _mlp(x, g, wu_bf16, wd_bf16):
    m, d = x.shape
    f = wu_bf16.shape[1]
    tm = _row_tile(m)
    return pl.pallas_call(
        _mlp_kernel,
        grid=(m // tm, f // MLP_TF),
        in_specs=[pl.BlockSpec((tm, d), lambda i, j: (i, 0)),
                  pl.BlockSpec((1, d), lambda i, j: (0, 0)),
                  pl.BlockSpec((d, MLP_TF), lambda i, j: (0, j)),
                  pl.BlockSpec((MLP_TF, d), lambda i, j: (j, 0))],
        out_specs=pl.BlockSpec((tm, d), lambda i, j: (i, 0)),
        out_shape=jax.ShapeDtypeStruct((m, d), jnp.float32),
        scratch_shapes=[pltpu.VMEM((tm, d), jnp.bfloat16), pltpu.VMEM((tm, d), jnp.float32)],
        compiler_params=pltpu.CompilerParams(dimension_semantics=("arbitrary", "arbitrary"),
                                             vmem_limit_bytes=VMEM_LIMIT),
        name="mlp",
    )(x, g.reshape(1, d), wu_bf16, wd_bf16)


NSA_TQ = 128
NSA_TK = 512
NSA_ROWS = NSA_HEADS * NSA_TQ
NSA_WSPAN = WINDOW + NSA_TQ


def _dot(a, b, **kw):
    return jnp.dot(a, b, preferred_element_type=jnp.float32, **kw)


def _dot_nt(a, b):
    return lax.dot_general(a, b, (((1,), (1,)), ((), ())), preferred_element_type=jnp.float32)


def rope_tables(pos):
    half = ROPE_DIM // 2
    inv_freq = ROPE_THETA ** (-jnp.arange(half, dtype=jnp.float32) / half)
    ang = pos.astype(jnp.float32)[:, None] * inv_freq
    cos, sin = jnp.cos(ang), jnp.sin(ang)
    n = pos.shape[0]
    one = jnp.ones((n, NSA_HD - ROPE_DIM), jnp.float32)
    zero = jnp.zeros((n, NSA_HD - ROPE_DIM), jnp.float32)
    z8 = jnp.zeros((n, half), jnp.float32)
    c = jnp.concatenate([cos, cos, one], axis=1)
    s1 = jnp.concatenate([-sin, z8, zero], axis=1)
    s2 = jnp.concatenate([z8, sin, zero], axis=1)
    two = lambda a: jnp.concatenate([a, a], axis=1)
    return two(c), two(s1), two(s2)


def _rope(x, c, s1, s2):
    n = x.shape[-1]
    return x * c + pltpu.roll(x, n - ROPE_DIM // 2, 1) * s1 + pltpu.roll(x, ROPE_DIM // 2, 1) * s2


def _nsa_prep_kernel(kv0_ref, kv1_ref, kv2_ref, c_ref, s1_ref, s2_ref, rows_ref, win_ref, kvb_ref):
    c, s1, s2 = c_ref[...], s1_ref[...], s2_ref[...]
    cmp_kv = kv0_ref[...]
    sel = kv1_ref[...]
    wnd = kv2_ref[...]
    ks = _rope(sel[:, :LANE], c, s1, s2)
    kw = _rope(wnd[:, :LANE], c, s1, s2)
    rows_ref[:, 0:2 * LANE] = cmp_kv
    rows_ref[:, 2 * LANE:3 * LANE] = ks
    rows_ref[:, 3 * LANE:4 * LANE] = sel[:, LANE:]
    win_ref[:, 0:LANE] = kw
    win_ref[:, LANE:2 * LANE] = wnd[:, LANE:]
    kvb_ref[:, 0:LANE] = ks.astype(MXU_DT)
    kvb_ref[:, LANE:2 * LANE] = sel[:, LANE:].astype(MXU_DT)
    kvb_ref[:, 2 * LANE:3 * LANE] = kw.astype(MXU_DT)
    kvb_ref[:, 3 * LANE:4 * LANE] = wnd[:, LANE:].astype(MXU_DT)


def nsa_prep(c_all, kv_col0, tabs, seq_len):
    m = c_all.shape[0]
    tm = 512 if seq_len % 512 == 0 else seq_len
    nlt = seq_len // tm
    cb = kv_col0 // (2 * LANE)
    kv_spec = lambda k: pl.BlockSpec((tm, 2 * LANE), lambda i, k=k: (i, cb + k))
    tab_spec = pl.BlockSpec((tm, LANE), lambda i: (i % nlt, 0))
    return pl.pallas_call(
        _nsa_prep_kernel,
        grid=(m // tm,),
        in_specs=[kv_spec(0), kv_spec(1), kv_spec(2), tab_spec, tab_spec, tab_spec],
        out_specs=[pl.BlockSpec((tm, 4 * LANE), lambda i: (i, 0)),
                   pl.BlockSpec((tm, 2 * LANE), lambda i: (i, 0)),
                   pl.BlockSpec((tm, 4 * LANE), lambda i: (i, 0))],
        out_shape=[jax.ShapeDtypeStruct((m, 4 * LANE), jnp.float32),
                   jax.ShapeDtypeStruct((m, 2 * LANE), jnp.float32),
                   jax.ShapeDtypeStruct((m, 4 * LANE), MXU_DT)],
        compiler_params=pltpu.CompilerParams(dimension_semantics=("arbitrary",), vmem_limit_bytes=VMEM_LIMIT),
        name="nsa_prep",
    )(c_all, c_all, c_all, *tabs)


def _nsa_compress_kernel(x_ref, pos_ref, w1_ref, w2_ref, o_ref):
    nh = x_ref.shape[3]
    half = CMP_BLOCK // 2 * NSA_HD
    acc = None
    for g in range(NSA_KV_HEADS):
        x = x_ref[0, 0, g]
        top = _dot((x + pos_ref[0, 0:1, :]).astype(MXU_DT), w1_ref[0, 0:half, :])
        bot = _dot((x + pos_ref[0, 1:2, :]).astype(MXU_DT), w1_ref[0, half:2 * half, :])
        h = top + pltpu.roll(bot, nh - 1, 0)
        h = h * jax.nn.sigmoid(h)
        t = _dot(h.astype(MXU_DT), w2_ref[0, g])
        acc = t if acc is None else acc + t
    o_ref[0, 0] = acc.astype(o_ref.dtype)


def nsa_compress(x16, cmp_pos, cmp_w1, cmp_w2):
    b, _, g, nh, hw = x16.shape
    pos = cmp_pos.reshape(2, 2, hw)
    w1 = cmp_w1.astype(MXU_DT)
    w2 = cmp_w2.astype(MXU_DT)
    w2p = jnp.zeros((2, g, CMP_HIDDEN, LANE), MXU_DT)
    for gi in range(g):
        w2p = w2p.at[:, gi, :, gi * NSA_HD:(gi + 1) * NSA_HD].set(w2)
    return pl.pallas_call(
        _nsa_compress_kernel,
        grid=(b, 2),
        in_specs=[pl.BlockSpec((1, 1, g, nh, hw), lambda i, j: (i, j, 0, 0, 0)),
                  pl.BlockSpec((1, 2, hw), lambda i, j: (j, 0, 0)),
                  pl.BlockSpec((1, 2 * hw, CMP_HIDDEN), lambda i, j: (j, 0, 0)),
                  pl.BlockSpec((1, g, CMP_HIDDEN, LANE), lambda i, j: (j, 0, 0, 0))],
        out_specs=pl.BlockSpec((1, 1, nh, LANE), lambda i, j: (i, j, 0, 0)),
        out_shape=jax.ShapeDtypeStruct((b, 2, nh, LANE), MXU_DT),
        compiler_params=pltpu.CompilerParams(dimension_semantics=("arbitrary", "arbitrary"), vmem_limit_bytes=VMEM_LIMIT),
        name="nsa_compress",
    )(x16, pos, w1, w2p)


def _pad_heads(q):
    lane = lax.broadcasted_iota(jnp.int32, (q.shape[0], LANE), 1)
    blocks = []
    for h in range(NSA_HEADS):
        blk = q[:, (h // 2) * LANE:(h // 2 + 1) * LANE]
        g = h // NSA_GROUP
        if h % 2 != g:
            blk = pltpu.roll(blk, NSA_HD, 1)
        keep = (lane < NSA_HD) if g == 0 else (lane >= NSA_HD)
        blocks.append(jnp.where(keep, blk, 0.0))
    return jnp.concatenate(blocks, axis=0)


def _softmax_rows(s):
    m = jnp.max(s, axis=-1, keepdims=True)
    m = jnp.where(m == NEG_INF, 0.0, m)
    p = jnp.exp(s - m)
    return p / jnp.maximum(jnp.sum(p, axis=-1, keepdims=True), 1e-30)


def _nsa_attn_kernel(q_ref, gate_ref, c_ref, s1_ref, s2_ref, ck_ref, cv_ref, ks_ref, vs_ref, kw_ref, vw_ref,
                     o_ref, m_ref, l_ref, acc_ref, *, q_start, k_start):
    tq = NSA_TQ
    i = pl.program_id(1)
    q0 = q_start + i * tq
    scale = NSA_HD ** -0.5
    q = q_ref[0] * scale
    c = jnp.concatenate([c_ref[...]] * 4, axis=1)
    s1 = jnp.concatenate([s1_ref[...]] * 4, axis=1)
    s2 = jnp.concatenate([s2_ref[...]] * 4, axis=1)
    q_raw = _pad_heads(q).astype(MXU_DT)
    q_rot = _pad_heads(_rope(q, c, s1, s2)).astype(MXU_DT)

    row = lax.broadcasted_iota(jnp.int32, (NSA_ROWS, 1), 0)
    qpos = q0 + (row & (tq - 1))

    nc = ck_ref.shape[1]
    s_c = _dot_nt(q_raw, ck_ref[0])
    c_end = lax.broadcasted_iota(jnp.int32, (1, nc), 1) * CMP_STRIDE + (CMP_BLOCK - 1)
    p_c = _softmax_rows(jnp.where(c_end <= qpos, s_c, NEG_INF))
    o_c = _dot(p_c.astype(MXU_DT), cv_ref[0])

    psum = jnp.concatenate(
        [sum(p_c[(g * NSA_GROUP + j) * tq:(g * NSA_GROUP + j + 1) * tq] for j in range(NSA_GROUP))
         for g in range(NSA_KV_HEADS)], axis=0)
    ns = LANE
    n_i = lax.broadcasted_iota(jnp.int32, (nc, ns), 0) * CMP_STRIDE
    m_i = lax.broadcasted_iota(jnp.int32, (nc, ns), 1) * SEL_BLOCK
    overlap = ((n_i <= m_i + (SEL_BLOCK - 1)) & (n_i + (CMP_BLOCK - 1) >= m_i)).astype(jnp.float32)
    imp = _dot(psum, overlap, precision=lax.Precision.HIGHEST)
    r2 = lax.broadcasted_iota(jnp.int32, (NSA_KV_HEADS * tq, 1), 0)
    qpos2 = q0 + (r2 & (tq - 1))
    cur = qpos2 >> 6
    blk = lax.broadcasted_iota(jnp.int32, (1, ns), 1)
    forced = (blk == 0) | (blk == cur) | (blk == cur - 1)
    valid = blk * SEL_BLOCK <= qpos2
    v = jnp.where(forced, jnp.inf, jnp.where(valid, imp, NEG_INF))
    blk_f = blk.astype(jnp.float32)
    sel = jnp.zeros(v.shape, jnp.float32)
    for _ in range(SEL_TOPN):
        mx = jnp.max(v, axis=-1, keepdims=True)
        first = jnp.min(jnp.where(v == mx, blk_f, float(ns)), axis=-1, keepdims=True)
        pick = blk_f == first
        sel = jnp.where(pick, 1.0, sel)
        v = jnp.where(pick, NEG_INF, v)
    sel_b = sel.astype(MXU_DT)

    m_ref[...] = jnp.full(m_ref.shape, NEG_INF, jnp.float32)
    l_ref[...] = jnp.zeros(l_ref.shape, jnp.float32)
    acc_ref[...] = jnp.zeros(acc_ref.shape, jnp.float32)
    n_kt = (q0 + tq - k_start + NSA_TK - 1) // NSA_TK

    def body(kt, carry):
        koff = pl.multiple_of(kt * NSA_TK, NSA_TK)
        k = ks_ref[0, pl.ds(koff, NSA_TK), :]
        vv = vs_ref[0, pl.ds(koff, NSA_TK), :]
        s = _dot_nt(q_rot, k)
        tok = k_start + koff + lax.broadcasted_iota(jnp.int32, (1, NSA_TK), 1)
        e_m = lax.broadcasted_iota(jnp.int32, (ns, NSA_TK), 0)
        e_t = (k_start + koff + lax.broadcasted_iota(jnp.int32, (ns, NSA_TK), 1)) >> 6
        chosen = _dot(sel_b, (e_m == e_t).astype(MXU_DT))
        bias2 = jnp.where((chosen > 0.5) & (tok <= qpos2), 0.0, NEG_INF)
        bias = jnp.concatenate([bias2[0:tq]] * NSA_GROUP + [bias2[tq:2 * tq]] * NSA_GROUP, axis=0)
        s = s + bias
        m_old = m_ref[...]
        m_new = jnp.maximum(m_old, jnp.max(s, axis=-1, keepdims=True))
        m_safe = jnp.where(m_new == NEG_INF, 0.0, m_new)
        alpha = jnp.exp(m_old - m_safe)
        p = jnp.exp(s - m_safe)
        l_ref[...] = alpha * l_ref[...] + jnp.sum(p, axis=-1, keepdims=True)
        acc_ref[...] = alpha * acc_ref[...] + _dot(p.astype(MXU_DT), vv)
        m_ref[...] = m_new
        return carry

    lax.fori_loop(0, n_kt, body, 0)
    o_s = acc_ref[...] / jnp.maximum(l_ref[...], 1e-30)

    w0 = jnp.maximum(q0 - k_start - WINDOW, 0)
    w0 = pl.multiple_of(w0, tq)
    kw = kw_ref[0, pl.ds(w0, NSA_WSPAN), :]
    vw = vw_ref[0, pl.ds(w0, NSA_WSPAN), :]
    s_w = _dot_nt(q_rot, kw)
    kpos = k_start + w0 + lax.broadcasted_iota(jnp.int32, (1, NSA_WSPAN), 1)
    w_ok = (kpos <= qpos) & (kpos > qpos - WINDOW)
    p_w = _softmax_rows(jnp.where(w_ok, s_w, NEG_INF))
    o_w = _dot(p_w.astype(MXU_DT), vw)

    gates = jax.nn.sigmoid(gate_ref[0])
    lane = lax.broadcasted_iota(jnp.int32, (tq, LANE), 1)
    outs = []
    for h in range(NSA_HEADS):
        sl = slice(h * tq, (h + 1) * tq)
        k0 = 2 * DN_HEADS + 3 * h
        o_h = (gates[:, k0:k0 + 1] * o_c[sl] + gates[:, k0 + 1:k0 + 2] * o_s[sl] + gates[:, k0 + 2:k0 + 3] * o_w[sl])
        if h % 2 != h // NSA_GROUP:
            o_h = pltpu.roll(o_h, NSA_HD, 1)
        outs.append(o_h)
    for p2 in range(NSA_HEADS // 2):
        o_ref[0, :, p2 * LANE:(p2 + 1) * LANE] = jnp.where(lane < NSA_HD, outs[2 * p2], outs[2 * p2 + 1])


def nsa_attend_prompt(c_all, q_col0, misc_col0, tabs, ckv, kvb, batch, seq_len):
    tq = NSA_TQ
    nqt = seq_len // tq
    qb = q_col0 // (4 * LANE)
    mb = misc_col0 // LANE
    c3 = c_all.reshape(batch, seq_len, c_all.shape[1])
    kv3 = kvb.reshape(batch, seq_len, 4 * LANE)
    tab_spec = pl.BlockSpec((tq, LANE), lambda b, i: (i, 0))
    kv_spec = lambda k: pl.BlockSpec((1, seq_len, LANE), lambda b, i, k=k: (b, 0, k))
    nc = ckv.shape[2]
    out = pl.pallas_call(
        functools.partial(_nsa_attn_kernel, q_start=0, k_start=0),
        grid=(batch, nqt),
        in_specs=[pl.BlockSpec((1, tq, 4 * LANE), lambda b, i: (b, i, qb)),
                  pl.BlockSpec((1, tq, LANE), lambda b, i: (b, i, mb)),
                  tab_spec, tab_spec, tab_spec,
                  pl.BlockSpec((None, 1, nc, LANE), lambda b, i: (b, 0, 0, 0)),
                  pl.BlockSpec((None, 1, nc, LANE), lambda b, i: (b, 1, 0, 0)),
                  kv_spec(0), kv_spec(1), kv_spec(2), kv_spec(3)],
        out_specs=pl.BlockSpec((1, tq, 4 * LANE), lambda b, i: (b, i, 0)),
        out_shape=jax.ShapeDtypeStruct((batch, seq_len, 4 * LANE), jnp.float32),
        scratch_shapes=[pltpu.VMEM((NSA_ROWS, 1), jnp.float32), pltpu.VMEM((NSA_ROWS, 1), jnp.float32),
                        pltpu.VMEM((NSA_ROWS, LANE), jnp.float32)],
        compiler_params=pltpu.CompilerParams(dimension_semantics=("arbitrary", "arbitrary"), vmem_limit_bytes=VMEM_LIMIT),
        name="nsa_attn",
    )(c3, c3, *tabs, ckv, ckv, kv3, kv3, kv3, kv3)
    return out.reshape(batch * seq_len, 4 * LANE)


def x16_from_rows(rows, batch, seq_len):
    r = rows.reshape(batch, seq_len // CMP_STRIDE, CMP_STRIDE, 4, NSA_KV_HEADS, NSA_HD)[:, :, :, 0:2]
    r = jnp.transpose(r, (0, 3, 4, 1, 2, 5))
    return r.reshape(batch, 2, NSA_KV_HEADS, seq_len // CMP_STRIDE, CMP_STRIDE * NSA_HD)


SCAN_C = 64
SCAN_TT = 512
HALO = 8


def _dot_tn(a, b):
    return lax.dot_general(a, b, (((0,), (0,)), ((), ())), preferred_element_type=jnp.float32)


def _mx(x):
    return x.astype(MXU_DT)


def _dot3(a, b):
    a_hi, b_hi = _mx(a), _mx(b)
    a_lo = _mx(a - a_hi.astype(jnp.float32))
    b_lo = _mx(b - b_hi.astype(jnp.float32))
    return _dot(a_hi, b_hi) + (_dot(a_hi, b_lo) + _dot(a_lo, b_hi))


def _chunk_cumsum(x):
    c = x.shape[0]
    row = lax.broadcasted_iota(jnp.int32, x.shape, 0)
    d = 1
    while d < c:
        x = x + jnp.where(row >= d, pltpu.roll(x, d, 0), 0.0)
        d *= 2
    return x


def _silu(x):
    return x * jax.nn.sigmoid(x)


def _conv_tile(x, halo, w_ref):
    kw = w_ref.shape[0]
    r8 = lax.broadcasted_iota(jnp.int32, (HALO, x.shape[1]), 0)
    y = x * w_ref[kw - 1:kw, :]
    for d in range(1, kw):
        xs = pltpu.roll(x, d, 0)
        head = jnp.where(r8 < d, pltpu.roll(halo, d, 0), xs[0:HALO])
        xs = jnp.concatenate([head, xs[HALO:]], axis=0)
        y = y + xs * w_ref[kw - 1 - d:kw - d, :]
    return y


def _halo_spec(tt, width, col_block):
    return pl.BlockSpec((HALO, width), lambda i: (jnp.maximum(i * (tt // HALO) - 1, 0), col_block))


def _pad_buf(buf):
    return jnp.pad(buf.astype(jnp.float32), ((0, 0), (HALO - buf.shape[1], 0), (0, 0)))


def _hgrn_prep_kernel(q_ref, f_ref, v_ref, la_ref, l1_ref, lbc_ref, oi_ref, qb_ref, kb_ref, eb_ref):
    c = SCAN_C
    tt = q_ref.shape[0]
    row = lax.broadcasted_iota(jnp.int32, (c, LANE), 0)
    ti = lax.broadcasted_iota(jnp.int32, (c, c), 0)
    si = lax.broadcasted_iota(jnp.int32, (c, c), 1)
    for h in range(HG_HEADS):
        ls = slice(h * LANE, (h + 1) * LANE)
        la, l1, lbc = la_ref[:, ls], l1_ref[:, ls], lbc_ref[:, ls]
        for ci in range(tt // c):
            rs = slice(ci * c, (ci + 1) * c)
            q = _silu(q_ref[rs, ls])
            z = f_ref[rs, ls]
            v = v_ref[rs, ls]
            lsig = jnp.minimum(z, 0.0) - jnp.log1p(jnp.exp(-jnp.abs(z)))
            t2 = l1 + lsig
            hi = jnp.maximum(la, t2)
            logf = hi + jnp.log1p(jnp.exp(-jnp.abs(la - t2)))
            k = lbc * jax.nn.sigmoid(-z)
            b = _chunk_cumsum(logf)
            b_last = b[c - 1:c, :]
            qb_ref[rs, ls] = q * jnp.exp(b)
            kb_ref[rs, ls] = k * jnp.exp(b_last - b)
            eb_ref[ci:ci + 1, ls] = jnp.exp(b_last)
            att = jnp.zeros((c, c), jnp.float32)
            n = c // 2
            while n >= 8:
                blk = 2 * n
                ref_rows = jnp.concatenate(
                    [jnp.broadcast_to(b[j * blk + n - 1:j * blk + n, :], (blk, LANE)) for j in range(c // blk)], axis=0)
                upper = (row & (blk - 1)) >= n
                qs = q * jnp.exp(jnp.where(upper, b - ref_rows, 0.0))
                ks = k * jnp.exp(jnp.where(upper, 0.0, ref_rows - b))
                lvl = _dot_nt(_mx(qs), _mx(ks))
                ok = ((ti & ~(blk - 1)) == (si & ~(blk - 1))) & ((ti & (blk - 1)) >= n) & ((si & (blk - 1)) < n)
                att = att + jnp.where(ok, lvl, 0.0)
                n //= 2
            o = _dot(_mx(att), _mx(v))
            for d in range(8):
                if d == 0:
                    w = jnp.sum(q * k, axis=-1, keepdims=True)
                    o = o + w * v
                else:
                    e = jnp.exp(jnp.minimum(b - pltpu.roll(b, d, 0), 0.0))
                    w = jnp.sum(q * pltpu.roll(k, d, 0) * e, axis=-1, keepdims=True)
                    w = jnp.where((row[:, 0:1] & 7) >= d, w, 0.0)
                    o = o + w * pltpu.roll(v, d, 0)
            oi_ref[rs, ls] = o


def hgrn_prep(c_all, col_q, col_f, col_v, lb):
    m = c_all.shape[0]
    tt = SCAN_TT
    lbf = lb.reshape(1, BRANCH_W).astype(jnp.float32)
    la, l1, lbc = jnp.log(lbf), jnp.log1p(-lbf), 1.0 - lbf
    blk = lambda col: pl.BlockSpec((tt, BRANCH_W), lambda i, col=col: (i, col // BRANCH_W))
    vec = pl.BlockSpec((1, BRANCH_W), lambda i: (0, 0))
    out = pl.BlockSpec((tt, BRANCH_W), lambda i: (i, 0))
    return pl.pallas_call(
        _hgrn_prep_kernel,
        grid=(m // tt,),
        in_specs=[blk(col_q), blk(col_f), blk(col_v), vec, vec, vec],
        out_specs=[out, out, out, pl.BlockSpec((tt // SCAN_C, BRANCH_W), lambda i: (i, 0))],
        out_shape=[jax.ShapeDtypeStruct((m, BRANCH_W), jnp.float32)] * 3
                  + [jax.ShapeDtypeStruct((m // SCAN_C, BRANCH_W), jnp.float32)],
        compiler_params=pltpu.CompilerParams(dimension_semantics=("arbitrary",), vmem_limit_bytes=VMEM_LIMIT),
        name="hgrn_prep",
    )(c_all, c_all, c_all, la, l1, lbc)


def _hgrn_scan_kernel(oi_ref, qb_ref, kb_ref, eb_ref, v_ref, g_ref, nw_ref, s0_ref, o_ref, sT_out_ref, sT_ref):
    c = SCAN_C
    nb = oi_ref.shape[0]
    tt = oi_ref.shape[1]
    i = pl.program_id(0)

    @pl.when(i == 0)
    def _():
        sT_ref[...] = s0_ref[...]

    nw = nw_ref[...]
    for ci in range(tt // c):
        rs = slice(ci * c, (ci + 1) * c)
        for b in range(nb):
            for h in range(HG_HEADS):
                ls = slice(h * LANE, (h + 1) * LANE)
                sT = sT_ref[b, h]
                o = oi_ref[b, rs, ls] + _dot_nt(_mx(qb_ref[b, rs, ls]), _mx(sT))
                sT_ref[b, h] = eb_ref[b, ci:ci + 1, ls] * sT + _dot_tn(_mx(v_ref[b, rs, ls]), _mx(kb_ref[b, rs, ls]))
                y = o * lax.rsqrt(jnp.mean(o * o, axis=-1, keepdims=True) + EPS) * nw
                o_ref[b, rs, ls] = y * _silu(g_ref[b, rs, ls])

    @pl.when(i == pl.num_programs(0) - 1)
    def _():
        sT_out_ref[...] = sT_ref[...]


def hgrn_scan(oi, qb, kb, eb, c_all, col_v, col_g, norm_w, s0, batch, seq_len):
    tt = SCAN_TT
    n3 = lambda a: a.reshape(batch, seq_len, a.shape[-1])
    c3 = n3(c_all)
    tok = pl.BlockSpec((batch, tt, BRANCH_W), lambda i: (0, i, 0))
    ctok = lambda col: pl.BlockSpec((batch, tt, BRANCH_W), lambda i, col=col: (0, i, col // BRANCH_W))
    st = pl.BlockSpec((batch, HG_HEADS, LANE, LANE), lambda i: (0, 0, 0, 0))
    o, sT = pl.pallas_call(
        _hgrn_scan_kernel,
        grid=(seq_len // tt,),
        in_specs=[tok, tok, tok, pl.BlockSpec((batch, tt // SCAN_C, BRANCH_W), lambda i: (0, i, 0)),
                  ctok(col_v), ctok(col_g), pl.BlockSpec((1, LANE), lambda i: (0, 0)), st],
        out_specs=[tok, st],
        out_shape=[jax.ShapeDtypeStruct((batch, seq_len, BRANCH_W), jnp.float32),
                   jax.ShapeDtypeStruct((batch, HG_HEADS, LANE, LANE), jnp.float32)],
        scratch_shapes=[pltpu.VMEM((batch, HG_HEADS, LANE, LANE), jnp.float32)],
        compiler_params=pltpu.CompilerParams(dimension_semantics=("arbitrary",), vmem_limit_bytes=VMEM_LIMIT),
        name="hgrn_scan",
    )(n3(oi), n3(qb), n3(kb), eb.reshape(batch, seq_len // SCAN_C, BRANCH_W), c3, c3,
      norm_w.reshape(1, LANE), jnp.swapaxes(s0, -1, -2))
    return o.reshape(batch * seq_len, BRANCH_W), jnp.swapaxes(sT, -1, -2)


def _gdn_prep_kernel(x_ref, halo_ref, buf_ref, misc_ref, cw_ref, alog_ref, dtb_ref,
                     u_ref, w_ref, qg_ref, kg_ref, qk_ref, eg_ref, *, tiles_per_seq):
    c = SCAN_C
    tt = x_ref.shape[0]
    first = (pl.program_id(0) % tiles_per_seq) == 0
    halo = jnp.where(first, buf_ref[0], halo_ref[...])
    y = _silu(_conv_tile(x_ref[...], halo, cw_ref))
    misc = misc_ref[...]
    beta_all = jax.nn.sigmoid(misc)
    sp_in = misc + dtb_ref[...]
    sp = jnp.maximum(sp_in, 0.0) + jnp.log1p(jnp.exp(-jnp.abs(sp_in)))
    g_all = -jnp.exp(alog_ref[...]) * sp
    ti = lax.broadcasted_iota(jnp.int32, (c, c), 0)
    si = lax.broadcasted_iota(jnp.int32, (c, c), 1)
    eye = (ti == si).astype(jnp.float32)
    for ci in range(tt // c):
        rs = slice(ci * c, (ci + 1) * c)
        gam_all = _chunk_cumsum(g_all[rs])
        gam_t = gam_all.T
        for h in range(DN_HEADS):
            ls = slice(h * LANE, (h + 1) * LANE)
            q = y[rs, h * LANE:(h + 1) * LANE]
            k = y[rs, BRANCH_W + h * LANE:BRANCH_W + (h + 1) * LANE]
            v = y[rs, 2 * BRANCH_W + h * LANE:2 * BRANCH_W + (h + 1) * LANE]
            q = q * lax.rsqrt(jnp.sum(q * q, axis=-1, keepdims=True) + EPS) * (DN_DK ** -0.5)
            k = k * lax.rsqrt(jnp.sum(k * k, axis=-1, keepdims=True) + EPS)
            beta = beta_all[rs, h:h + 1]
            gam = gam_all[:, DN_HEADS + h:DN_HEADS + h + 1]
            gam_r = gam_t[DN_HEADS + h:DN_HEADS + h + 1, :]
            decay = jnp.exp(jnp.where(si <= ti, gam - gam_r, NEG_INF))
            kb16 = _mx(k)
            kk = _dot_nt(kb16, kb16)
            a = jnp.where(si < ti, beta * kk * decay, 0.0)
            tinv = eye
            s = 1
            while s < c:
                blk = 2 * s
                off = (((ti & ~(blk - 1)) == (si & ~(blk - 1))) & ((ti & (blk - 1)) >= s) & ((si & (blk - 1)) < s))
                a_off = jnp.where(off, a, 0.0)
                tinv = tinv - (a_off if s == 1 else _dot3(_dot3(tinv, a_off), tinv))
                s = blk
            e_gam = jnp.exp(gam)
            rhs = jnp.concatenate([v * beta, k * (beta * e_gam)], axis=1)
            sol = rhs + _dot3(tinv - eye, rhs)
            u_ref[rs, ls] = sol[:, :LANE]
            w_ref[rs, ls] = sol[:, LANE:]
            qg_ref[rs, ls] = q * e_gam
            g_last = gam[c - 1:c, :]
            kg_ref[rs, ls] = k * jnp.exp(g_last - gam)
            qk_ref[rs, h * c:(h + 1) * c] = _dot_nt(_mx(q), kb16) * decay
            eg_ref[ci:ci + 1, ls] = jnp.broadcast_to(jnp.exp(g_last), (1, LANE))


def gdn_prep(c_all, col_x, col_misc, conv_buf, conv_w, a_log, dt_bias, batch, seq_len):
    m = c_all.shape[0]
    tt = SCAN_TT
    tps = seq_len // tt
    xw = 3 * BRANCH_W
    lanes = jnp.zeros((1, LANE), jnp.float32)
    alog = lanes.at[0, DN_HEADS:2 * DN_HEADS].set(a_log.astype(jnp.float32))
    dtb = lanes.at[0, DN_HEADS:2 * DN_HEADS].set(dt_bias.astype(jnp.float32))
    out = pl.BlockSpec((tt, BRANCH_W), lambda i: (i, 0))
    vec = pl.BlockSpec((1, LANE), lambda i: (0, 0))
    return pl.pallas_call(
        functools.partial(_gdn_prep_kernel, tiles_per_seq=tps),
        grid=(m // tt,),
        in_specs=[pl.BlockSpec((tt, xw), lambda i: (i, col_x // xw)),
                  _halo_spec(tt, xw, col_x // xw),
                  pl.BlockSpec((1, HALO, xw), lambda i: (i // tps, 0, 0)),
                  pl.BlockSpec((tt, LANE), lambda i: (i, col_misc // LANE)),
                  pl.BlockSpec((DN_CONV, xw), lambda i: (0, 0)), vec, vec],
        out_specs=[out, out, out, out, pl.BlockSpec((tt, DN_HEADS * SCAN_C), lambda i: (i, 0)),
                   pl.BlockSpec((tt // SCAN_C, BRANCH_W), lambda i: (i, 0))],
        out_shape=[jax.ShapeDtypeStruct((m, BRANCH_W), jnp.float32)] * 4
                  + [jax.ShapeDtypeStruct((m, DN_HEADS * SCAN_C), jnp.float32),
                     jax.ShapeDtypeStruct((m // SCAN_C, BRANCH_W), jnp.float32)],
        compiler_params=pltpu.CompilerParams(dimension_semantics=("arbitrary",), vmem_limit_bytes=VMEM_LIMIT),
        name="gdn_prep",
    )(c_all, c_all, _pad_buf(conv_buf), c_all, conv_w.astype(jnp.float32), alog, dtb)


def _gdn_scan_kernel(u_ref, w_ref, qg_ref, kg_ref, qk_ref, eg_ref, z_ref, nw_ref, s0_ref, o_ref, s_out_ref, s_ref):
    c = SCAN_C
    nb = u_ref.shape[0]
    tt = u_ref.shape[1]
    i = pl.program_id(0)

    @pl.when(i == 0)
    def _():
        s_ref[...] = s0_ref[...]

    nw = nw_ref[...]
    for ci in range(tt // c):
        rs = slice(ci * c, (ci + 1) * c)
        for b in range(nb):
            for h in range(DN_HEADS):
                ls = slice(h * LANE, (h + 1) * LANE)
                s = s_ref[b, h]
                s16 = _mx(s)
                both = _dot(_mx(jnp.concatenate([qg_ref[b, rs, ls], w_ref[b, rs, ls]], axis=0)), s16)
                v_new = u_ref[b, rs, ls] - both[c:]
                v16 = _mx(v_new)
                o = both[:c] + _dot(_mx(qk_ref[b, rs, h * c:(h + 1) * c]), v16)
                s_ref[b, h] = eg_ref[b, ci:ci + 1, ls][:, 0:1] * s + _dot_tn(_mx(kg_ref[b, rs, ls]), v16)
                y = o * lax.rsqrt(jnp.mean(o * o, axis=-1, keepdims=True) + EPS) * nw
                o_ref[b, rs, ls] = y * _silu(z_ref[b, rs, ls])

    @pl.when(i == pl.num_programs(0) - 1)
    def _():
        s_out_ref[...] = s_ref[...]


def gdn_scan(u, w, qg, kg, qk, eg, c_all, col_z, norm_w, s0, batch, seq_len):
    tt = SCAN_TT
    n3 = lambda a: a.reshape(batch, seq_len, a.shape[-1])
    tok = pl.BlockSpec((batch, tt, BRANCH_W), lambda i: (0, i, 0))
    st = pl.BlockSpec((batch, DN_HEADS, LANE, LANE), lambda i: (0, 0, 0, 0))
    o, s = pl.pallas_call(
        _gdn_scan_kernel,
        grid=(seq_len // tt,),
        in_specs=[tok, tok, tok, tok, pl.BlockSpec((batch, tt, DN_HEADS * SCAN_C), lambda i: (0, i, 0)),
                  pl.BlockSpec((batch, tt // SCAN_C, BRANCH_W), lambda i: (0, i, 0)),
                  pl.BlockSpec((batch, tt, BRANCH_W), lambda i: (0, i, col_z // BRANCH_W)),
                  pl.BlockSpec((1, LANE), lambda i: (0, 0)), st],
        out_specs=[tok, st],
        out_shape=[jax.ShapeDtypeStruct((batch, seq_len, BRANCH_W), jnp.float32),
                   jax.ShapeDtypeStruct((batch, DN_HEADS, LANE, LANE), jnp.float32)],
        scratch_shapes=[pltpu.VMEM((batch, DN_HEADS, LANE, LANE), jnp.float32)],
        compiler_params=pltpu.CompilerParams(dimension_semantics=("arbitrary",), vmem_limit_bytes=VMEM_LIMIT),
        name="gdn_scan",
    )(n3(u), n3(w), n3(qg), n3(kg), n3(qk), eg.reshape(batch, seq_len // SCAN_C, BRANCH_W), n3(c_all),
      norm_w.reshape(1, LANE), s0.astype(jnp.float32))
    return o.reshape(batch * seq_len, BRANCH_W), s


def _sc_branch_kernel(x_ref, halo_ref, buf_ref, w_ref, o_ref, *, tiles_per_seq):
    first = (pl.program_id(0) % tiles_per_seq) == 0
    x = x_ref[...]
    hr = halo_ref[...]
    halo = jnp.where(first, buf_ref[0], hr[:, SC_W:2 * SC_W] * hr[:, 2 * SC_W:3 * SC_W])
    o_ref[...] = x[:, 0:SC_W] * _conv_tile(x[:, SC_W:2 * SC_W] * x[:, 2 * SC_W:3 * SC_W], halo, w_ref)


def sc_branch(c_all, col_x, conv_buf, conv_w, seq_len):
    m = c_all.shape[0]
    tt = SCAN_TT
    tps = seq_len // tt
    xw = 3 * SC_W
    return pl.pallas_call(
        functools.partial(_sc_branch_kernel, tiles_per_seq=tps),
        grid=(m // tt,),
        in_specs=[pl.BlockSpec((tt, xw), lambda i: (i, col_x // xw)),
                  _halo_spec(tt, xw, col_x // xw),
                  pl.BlockSpec((1, HALO, SC_W), lambda i: (i // tps, 0, 0)),
                  pl.BlockSpec((SC_CONV, SC_W), lambda i: (0, 0))],
        out_specs=pl.BlockSpec((tt, SC_W), lambda i: (i, 0)),
        out_shape=jax.ShapeDtypeStruct((m, SC_W), jnp.float32),
        compiler_params=pltpu.CompilerParams(dimension_semantics=("arbitrary",), vmem_limit_bytes=VMEM_LIMIT),
        name="sc_branch",
    )(c_all, c_all, _pad_buf(conv_buf), conv_w.astype(jnp.float32))


def rms_norm(x, g):
    xf = x.astype(jnp.float32)
    y = xf * lax.rsqrt(jnp.mean(xf * xf, axis=-1, keepdims=True) + EPS)
    return (y * g.astype(jnp.float32)).astype(x.dtype)


def l2_normalize(x):
    xf = x.astype(jnp.float32)
    return xf * lax.rsqrt(jnp.sum(xf * xf, axis=-1, keepdims=True) + EPS)


def masked_softmax(s, mask):
    s = jnp.where(mask, s.astype(jnp.float32), -jnp.inf)
    m = jnp.max(s, axis=-1, keepdims=True)
    m = jnp.where(jnp.isfinite(m), m, 0.0)
    p = jnp.exp(s - m)
    return p / jnp.maximum(jnp.sum(p, axis=-1, keepdims=True), 1e-30)


def rotary(x, pos):
    half = ROPE_DIM // 2
    inv_freq = ROPE_THETA ** (-jnp.arange(half, dtype=jnp.float32) / half)
    ang = pos.astype(jnp.float32)[:, None] * inv_freq
    cos = jnp.cos(ang)[:, None, :]
    sin = jnp.sin(ang)[:, None, :]
    xf = x.astype(jnp.float32)
    x1, x2 = xf[..., :half], xf[..., half:ROPE_DIM]
    return jnp.concatenate([x1 * cos - x2 * sin, x2 * cos + x1 * sin, xf[..., ROPE_DIM:]], axis=-1).astype(x.dtype)


def causal_conv(x, buf, w):
    k_w = w.shape[0]
    seq_len = x.shape[1]
    xp = jnp.concatenate([buf.astype(x.dtype), x], axis=1)
    y = sum(xp[:, j:j + seq_len] * w[j] for j in range(k_w))
    return y, xp[:, seq_len:]


def to_chunks(a, c):
    b, seq_len = a.shape[:2]
    n = -(-seq_len // c)
    a = jnp.pad(a, [(0, 0), (0, n * c - seq_len)] + [(0, 0)] * (a.ndim - 2))
    return jnp.moveaxis(a.reshape((b, n, c) + a.shape[2:]), 1, 0)


def from_chunks(a, seq_len):
    n, b, c = a.shape[:3]
    return jnp.moveaxis(a, 0, 1).reshape((b, n * c) + a.shape[3:])[:, :seq_len]


def hgrn2_scan(q, k, v, logf, s0):
    seq_len = q.shape[1]
    c = min(HG_CHUNK, seq_len)
    tri = jnp.tril(jnp.ones((c, c), bool))[None, :, :, None, None]

    def step(s, inp):
        qc, kc, vc, gc = inp
        b = jnp.cumsum(gc, axis=1)
        decay = jnp.exp(jnp.where(tri, b[:, :, None] - b[:, None], -jnp.inf))
        att = jnp.einsum('bthk,btshk->btsh', qc, decay * kc[:, None])
        o = jnp.einsum('btsh,bshv->bthv', att, vc) + jnp.einsum('bthk,bhkv->bthv', qc * jnp.exp(b), s)
        b_last = b[:, -1]
        s = jnp.exp(b_last)[..., None] * s + jnp.einsum('bshk,bshv->bhkv', kc * jnp.exp(b_last[:, None] - b), vc)
        return s, o

    xs = tuple(to_chunks(a.astype(jnp.float32), c) for a in (q, k, v, logf))
    s, o = lax.scan(step, s0.astype(jnp.float32), xs)
    return from_chunks(o, seq_len), s


def gated_delta_scan(q, k, v, beta, g, s0):
    seq_len = q.shape[1]
    c = min(DN_CHUNK, seq_len)
    incl = jnp.tril(jnp.ones((c, c), bool))
    strict = jnp.tril(jnp.ones((c, c), bool), -1)
    eye = jnp.eye(c, dtype=jnp.float32)

    def step(s, inp):
        qc, kc, vc, bc, gc = inp
        qh, kh, vh = (jnp.swapaxes(a, 1, 2) for a in (qc, kc, vc))
        bh = jnp.swapaxes(bc, 1, 2)
        gam = jnp.cumsum(jnp.swapaxes(gc, 1, 2), axis=-1)
        decay = jnp.exp(jnp.where(incl, gam[..., :, None] - gam[..., None, :], -jnp.inf))
        kk = jnp.einsum('bhtk,bhsk->bhts', kh, kh)
        t_mat = eye + jnp.where(strict, bh[..., :, None] * kk * decay, 0.0)
        u = lax.linalg.triangular_solve(t_mat, vh * bh[..., None], left_side=True, lower=True)
        w = lax.linalg.triangular_solve(t_mat, kh * (bh * jnp.exp(gam))[..., None], left_side=True, lower=True)
        v_new = u - jnp.einsum('bhtk,bhkv->bhtv', w, s)
        qk = jnp.einsum('bhtk,bhsk->bhts', qh, kh) * decay
        o = jnp.einsum('bhtk,bhkv->bhtv', qh * jnp.exp(gam)[..., None], s) + jnp.einsum('bhts,bhsv->bhtv', qk, v_new)
        g_last = gam[..., -1]
        s = jnp.exp(g_last)[..., None, None] * s + jnp.einsum('bhtk,bhtv->bhkv', kh * jnp.exp(g_last[..., None] - gam)[..., None], v_new)
        return s, jnp.swapaxes(o, 1, 2)

    xs = tuple(to_chunks(a.astype(jnp.float32), c) for a in (q, k, v, beta, g))
    s, o = lax.scan(step, s0.astype(jnp.float32), xs)
    return from_chunks(o, seq_len), s


def compress_blocks(k_seq, pos_emb, w1, w2):
    b, lk = k_seq.shape[:2]
    nc = (lk - CMP_BLOCK) // CMP_STRIDE + 1
    idx = jnp.arange(nc)[:, None] * CMP_STRIDE + jnp.arange(CMP_BLOCK)[None, :]
    blk = k_seq[:, idx] + pos_emb[:, None, :]
    blk = jnp.swapaxes(blk, 2, 3).reshape(b, nc, NSA_KV_HEADS, CMP_BLOCK * NSA_HD)
    return jax.nn.silu(blk @ w1) @ w2


def nsa_attend(q, q_rot, gates, q_pos, kw, vw, kw_pos, ck, cv, c_end, ks_b, vs_b):
    b, nq = q.shape[:2]
    scale = NSA_HD ** -0.5
    qg = q.reshape(b, nq, NSA_KV_HEADS, NSA_GROUP, NSA_HD)
    qrg = q_rot.reshape(b, nq, NSA_KV_HEADS, NSA_GROUP, NSA_HD)
    s_c = jnp.einsum('bqgjd,bngd->bqgjn', qg, ck) * scale
    p_c = masked_softmax(s_c, (c_end[None, :] <= q_pos[:, None])[None, :, None, None, :])
    o_c = jnp.einsum('bqgjn,bngd->bqgjd', p_c.astype(cv.dtype), cv)
    ns = ks_b.shape[2]
    blk_start = jnp.arange(ns) * SEL_BLOCK
    c_start = c_end - (CMP_BLOCK - 1)
    overlap = ((c_start[:, None] <= blk_start[None, :] + SEL_BLOCK - 1) & (c_end[:, None] >= blk_start[None, :])).astype(jnp.float32)
    imp = jnp.einsum('bqgjn,nm->bqgm', p_c, overlap)
    cur = q_pos // SEL_BLOCK
    m = jnp.arange(ns)[None, :]
    valid = blk_start[None, :] <= q_pos[:, None]
    forced = (m == 0) | (m == cur[:, None]) | (m == cur[:, None] - 1)
    imp = jnp.where(forced[None, :, None, :], jnp.inf, jnp.where(valid[None, :, None, :], imp, -jnp.inf))
    n_top = min(SEL_TOPN, ns)
    _, top = lax.top_k(imp, n_top)
    top = jnp.moveaxis(top, 2, 1)
    bi = jnp.arange(b)[:, None, None, None]
    gi = jnp.arange(NSA_KV_HEADS)[None, :, None, None]
    gk = ks_b[bi, gi, top]
    gv = vs_b[bi, gi, top].reshape(b, NSA_KV_HEADS, nq, n_top * SEL_BLOCK, NSA_HD)
    tok = top[..., None] * SEL_BLOCK + jnp.arange(SEL_BLOCK)
    sel_mask = (tok <= q_pos[None, None, :, None, None]).reshape(b, NSA_KV_HEADS, nq, n_top * SEL_BLOCK)
    sel_mask = jnp.moveaxis(sel_mask, 1, 2)[:, :, :, None, :]
    s_s = jnp.einsum('bqgjd,bgqtsd->bqgjts', qrg, gk).reshape(b, nq, NSA_KV_HEADS, NSA_GROUP, n_top * SEL_BLOCK) * scale
    p_s = masked_softmax(s_s, sel_mask)
    o_s = jnp.einsum('bqgjk,bgqkd->bqgjd', p_s.astype(gv.dtype), gv)
    s_w = jnp.einsum('bqgjd,bkgd->bqgjk', qrg, kw) * scale
    w_mask = (kw_pos[None, :] <= q_pos[:, None]) & (kw_pos[None, :] > q_pos[:, None] - WINDOW) & (kw_pos[None, :] >= 0)
    p_w = masked_softmax(s_w, w_mask[None, :, None, None, :])
    o_w = jnp.einsum('bqgjk,bkgd->bqgjd', p_w.astype(vw.dtype), vw)
    g = gates.reshape(b, nq, NSA_KV_HEADS, NSA_GROUP, 3)
    o = g[..., 0:1] * o_c + g[..., 1:2] * o_s + g[..., 2:3] * o_w
    return o.reshape(b, nq, NSA_HEADS * NSA_HD)


def nsa_mixer(q_raw, kv_raw, gate_raw, start, past_rows, win_buf, cmp_pos, cmp_w1, cmp_w2):
    b, seq_len = q_raw.shape[:2]
    pos = start + jnp.arange(seq_len)
    q = q_raw.reshape(b, seq_len, NSA_HEADS, NSA_HD)
    kv = kv_raw.reshape(b, seq_len, 6, NSA_KV_HEADS, NSA_HD)
    q_rot = rotary(q, pos)
    rows = jnp.stack([kv[:, :, 0], kv[:, :, 1], rotary(kv[:, :, 2], pos), kv[:, :, 3]], axis=2)
    win_rows = jnp.stack([rotary(kv[:, :, 4], pos), kv[:, :, 5]], axis=2)
    gates = jax.nn.sigmoid(gate_raw.reshape(b, seq_len, NSA_HEADS, 3))
    seq = jnp.concatenate([past_rows.astype(rows.dtype), rows], axis=1)
    lk = seq.shape[1]
    ck = compress_blocks(seq[:, :, 0], cmp_pos[0], cmp_w1[0], cmp_w2[0])
    cv = compress_blocks(seq[:, :, 1], cmp_pos[1], cmp_w1[1], cmp_w2[1])
    c_end = jnp.arange(ck.shape[1]) * CMP_STRIDE + CMP_BLOCK - 1
    ns = -(-lk // SEL_BLOCK)
    sel = jnp.pad(seq[:, :, 2:4], ((0, 0), (0, ns * SEL_BLOCK - lk), (0, 0), (0, 0), (0, 0)))
    sel = sel.reshape(b, ns, SEL_BLOCK, 2, NSA_KV_HEADS, NSA_HD).transpose(3, 0, 4, 1, 2, 5)
    attend = functools.partial(nsa_attend, ck=ck, cv=cv, c_end=c_end, ks_b=sel[0], vs_b=sel[1])
    nb = win_buf.shape[1]
    w_all = jnp.concatenate([win_buf.astype(win_rows.dtype), win_rows], axis=1)
    o = attend(q, q_rot, gates, pos, kw=w_all[:, :, 0], vw=w_all[:, :, 1], kw_pos=start - nb + jnp.arange(nb + seq_len))
    return o, rows, w_all[:, seq_len:]


def trunk_layer(x, start, hg_s, dn_s, dn_buf, sc_buf, past_rows, win_buf, w):
    b, seq_len, _ = x.shape
    f32 = jnp.float32
    x2 = x.reshape(b * seq_len, D_MODEL)
    m = b * seq_len
    c2 = _norm_proj(x2, w["norm_mix"], w["w_in"])
    c_all = c2.reshape(b, seq_len, N_IN_PAD)
    lb = w["lb"]
    if past_rows is None:
        oi, qb, kb, eb = hgrn_prep(c2, C_OFF["hg_q"], C_OFF["hg_f"], C_OFF["hg_i"], lb)
        o_a, hg_s = hgrn_scan(oi, qb, kb, eb, c2, C_OFF["hg_i"], C_OFF["hg_g"], w["hg_norm"], hg_s, b, seq_len)
        u, wy, qg, kg, qk, eg = gdn_prep(c2, C_OFF["dn_qkv"], C_OFF["misc"], dn_buf, w["dn_conv"], w["dn_a_log"],
                                         w["dn_dt_bias"], b, seq_len)
        o_b, dn_s = gdn_scan(u, wy, qg, kg, qk, eg, c2, C_OFF["dn_z"], w["dn_norm"], dn_s, b, seq_len)
        o_c = sc_branch(c2, C_OFF["sc_bch"], sc_buf, w["sc_conv"], seq_len)
        tail = c_all[:, seq_len - (DN_CONV - 1):]
        dn_buf = tail[..., C_OFF["dn_qkv"]:C_OFF["dn_qkv"] + 3 * BRANCH_W]
        sc_t = tail[:, DN_CONV - SC_CONV:, C_OFF["sc_bch"] + SC_W:C_OFF["sc_bch"] + 3 * SC_W]
        sc_buf = sc_t[..., :SC_W] * sc_t[..., SC_W:]
        tabs = rope_tables(start + jnp.arange(seq_len))
        rows, win_rows, kvb = nsa_prep(c2, C_OFF["nsa_kv"], tabs, seq_len)
        ckv = nsa_compress(x16_from_rows(rows, b, seq_len), w["cmp_pos"], w["cmp_w1"], w["cmp_w2"])
        o_d = nsa_attend_prompt(c2, C_OFF["nsa_q"], C_OFF["misc"], tabs, ckv, kvb, b, seq_len)
        rows = rows.reshape(b, seq_len, 4, NSA_KV_HEADS, NSA_HD)
        new_win = win_rows.reshape(b, seq_len, 2, NSA_KV_HEADS, NSA_HD)[:, max(seq_len - WINDOW, 0):]
    else:
        o_a, o_b, o_c, o_d, (hg_s, dn_s, dn_buf, sc_buf, new_win, rows) = cached_mixers(
            x, c_all, start, hg_s, dn_s, dn_buf, sc_buf, past_rows, win_buf, w)
    branches = [o_a.reshape(m, BRANCH_W), o_b.reshape(m, BRANCH_W), o_c.reshape(m, BRANCH_W), o_d.reshape(m, BRANCH_W)]
    x2 = _merge_out(x2, branches, c2, C_OFF["merge_gate"], w["w_branch"], w["w_out"])
    x2 = _mlp(x2, w["norm_mlp"], w["w_up"], w["w_down"])
    return x2.reshape(b, seq_len, D_MODEL), (hg_s, dn_s, dn_buf, sc_buf, new_win, rows)


def cached_mixers(x, c_all, start, hg_s, dn_s, dn_buf, sc_buf, past_rows, win_buf, w):
    b, seq_len, _ = x.shape
    f32 = jnp.float32
    c = {n: c_all[..., C_OFF[n]:C_OFF[n] + wd] for n, wd in C_ORDER}
    for n, o0, wd in MISC_COLS:
        c[n] = c["misc"][..., o0:o0 + wd]
    lb = w["lb"]
    hq = jax.nn.silu(c["hg_q"]).reshape(b, seq_len, HG_HEADS, HG_DK)
    z = c["hg_f"].astype(f32).reshape(b, seq_len, HG_HEADS, HG_DK)
    logf = jnp.logaddexp(jnp.log(lb), jnp.log1p(-lb) + jax.nn.log_sigmoid(z))
    hk = (1.0 - lb) * jax.nn.sigmoid(-z)
    hv = c["hg_i"].reshape(b, seq_len, HG_HEADS, HG_DV)
    o_a, hg_s = hgrn2_scan(hq, hk, hv, logf, hg_s)
    o_a = rms_norm(o_a.astype(x.dtype), w["hg_norm"]) * jax.nn.silu(c["hg_g"].reshape(b, seq_len, HG_HEADS, HG_DV))
    qkv, dn_buf = causal_conv(c["dn_qkv"], dn_buf, w["dn_conv"])
    dq, dk, dv = jnp.split(jax.nn.silu(qkv), 3, axis=-1)
    dq = l2_normalize(dq.reshape(b, seq_len, DN_HEADS, DN_DK)) * DN_DK ** -0.5
    dk = l2_normalize(dk.reshape(b, seq_len, DN_HEADS, DN_DK))
    beta = jax.nn.sigmoid(c["dn_b"].astype(f32))
    g = -jnp.exp(w["dn_a_log"].astype(f32)) * jax.nn.softplus(c["dn_a"].astype(f32) + w["dn_dt_bias"])
    o_b, dn_s = gated_delta_scan(dq, dk, dv.reshape(b, seq_len, DN_HEADS, DN_DV), beta, g, dn_s)
    o_b = rms_norm(o_b.astype(x.dtype), w["dn_norm"]) * jax.nn.silu(c["dn_z"].reshape(b, seq_len, DN_HEADS, DN_DV))
    gb, gc, hx = jnp.split(c["sc_bch"], 3, axis=-1)
    conv, sc_buf = causal_conv(gc * hx, sc_buf, w["sc_conv"])
    o_c = gb * conv
    o_d, rows, new_win = nsa_mixer(c["nsa_q"], c["nsa_kv"], c["nsa_gate"], start, past_rows, win_buf,
                                   w["cmp_pos"], w["cmp_w1"], w["cmp_w2"])
    return o_a, o_b, o_c, o_d, (hg_s, dn_s, dn_buf, sc_buf, new_win, rows)


def kernel(x_prompt, x_sample, state_hgrn, state_dn, state_dn_conv, state_sc_conv, state_win_kv, cache_kv, page_table, norm_mix, norm_mlp, norm_final, w_in, hg_lb_logits, hg_norm, dn_conv, dn_a_log, dn_dt_bias, dn_norm, sc_conv, cmp_pos, cmp_w1, cmp_w2, w_branch, w_out, w_up, w_down):
    f32 = jnp.float32
    bf16 = jnp.bfloat16
    lbs = jnp.cumsum(jax.nn.softmax(hg_lb_logits.astype(f32), axis=0), axis=0)
    lbs = lbs - lbs[:1]
    w_in_b = _permute_w_in(w_in).astype(bf16)
    w_branch_b, w_out_b, w_up_b, w_down_b = (a.astype(bf16) for a in (w_branch, w_out, w_up, w_down))

    def layer_w(l):
        return dict(norm_mix=norm_mix[l], norm_mlp=norm_mlp[l], w_in=w_in_b[l], lb=lbs[l], hg_norm=hg_norm[l],
                    dn_conv=dn_conv[l], dn_a_log=dn_a_log[l], dn_dt_bias=dn_dt_bias[l], dn_norm=dn_norm[l],
                    sc_conv=sc_conv[l], cmp_pos=cmp_pos[l], cmp_w1=cmp_w1[l], cmp_w2=cmp_w2[l],
                    w_branch=w_branch_b[l], w_out=w_out_b[l], w_up=w_up_b[l], w_down=w_down_b[l])

    bp = x_prompt.shape[0]
    yp = x_prompt
    p_st = []
    for l in range(DEPTH):
        yp, st = trunk_layer(yp, 0,
                             jnp.zeros((bp, HG_HEADS, HG_DK, HG_DV), f32),
                             jnp.zeros((bp, DN_HEADS, DN_DK, DN_DV), f32),
                             jnp.zeros((bp, DN_CONV - 1, 3 * BRANCH_W), x_prompt.dtype),
                             jnp.zeros((bp, SC_CONV - 1, SC_W), x_prompt.dtype),
                             None, None, layer_w(l))
        p_st.append(st)
    ys = x_sample
    s_st = []
    for l in range(DEPTH):
        past = cache_kv[l][page_table]
        past = past.reshape(page_table.shape[0], -1, *past.shape[3:])
        ys, st = trunk_layer(ys, past.shape[1], state_hgrn[l], state_dn[l], state_dn_conv[l], state_sc_conv[l],
                             past, state_win_kv[l], layer_w(l))
        s_st.append(st)
    p = [jnp.stack([st[i] for st in p_st]) for i in range(6)]
    s = [jnp.stack([st[i] for st in s_st]) for i in range(6)]
    return (rms_norm(yp, norm_final), rms_norm(ys, norm_final),
            p[0], p[1], p[2], p[3], p[4], p[5],
            s[0], s[1], s[2], s[3], s[4], s[5])
```

```python
import math, functools
import jax, jax.numpy as jnp
from jax import lax
import numpy as np
from jax.experimental import pallas as pl
from jax.experimental.pallas import tpu as pltpu

D_MODEL = 1024
DEPTH = 4
PAGE_SIZE = 128
N_BRANCH = 4
BRANCH_W = D_MODEL // 2
HG_HEADS = 4
HG_DK = BRANCH_W // HG_HEADS
HG_DV = BRANCH_W // HG_HEADS
HG_CHUNK = 64
DN_HEADS = 4
DN_DK = BRANCH_W // DN_HEADS
DN_DV = BRANCH_W // DN_HEADS
DN_CONV = 4
DN_CHUNK = 64
SC_W = BRANCH_W
SC_CONV = 3
NSA_HEADS = 8
NSA_KV_HEADS = 2
NSA_HD = BRANCH_W // NSA_HEADS
NSA_GROUP = NSA_HEADS // NSA_KV_HEADS
ROPE_DIM = NSA_HD // 4
ROPE_THETA = 500000.0
CMP_BLOCK = 32
CMP_STRIDE = 16
CMP_HIDDEN = 4 * NSA_HD
SEL_BLOCK = 64
SEL_TOPN = 16
WINDOW = 512
Q_BLOCK = 128
D_FF = 4 * D_MODEL
EPS = 1e-6

IN_SPLITS = (
    ("hg_q", BRANCH_W), ("hg_f", BRANCH_W), ("hg_i", BRANCH_W), ("hg_g", BRANCH_W),
    ("dn_qkv", 3 * BRANCH_W), ("dn_b", DN_HEADS), ("dn_a", DN_HEADS), ("dn_z", BRANCH_W),
    ("sc_bch", 3 * SC_W),
    ("nsa_q", NSA_HEADS * NSA_HD), ("nsa_kv", 6 * NSA_KV_HEADS * NSA_HD), ("nsa_gate", 3 * NSA_HEADS),
    ("merge_gate", N_BRANCH * D_MODEL),
)
IN_NAMES = tuple(n for n, _ in IN_SPLITS)
IN_CUTS = tuple(int(c) for c in np.cumsum([s for _, s in IN_SPLITS])[:-1])
N_IN = sum(s for _, s in IN_SPLITS)

LANE = 128
PROJ_TN = 512
VMEM_LIMIT = 48 * 1024 * 1024
MXU_DT = jnp.bfloat16

MISC_W = 2 * LANE
C_ORDER = (("merge_gate", N_BRANCH * D_MODEL), ("hg_q", BRANCH_W), ("hg_f", BRANCH_W), ("hg_i", BRANCH_W),
           ("hg_g", BRANCH_W), ("dn_qkv", 3 * BRANCH_W), ("sc_bch", 3 * SC_W), ("dn_z", BRANCH_W),
           ("nsa_q", NSA_HEADS * NSA_HD), ("nsa_kv", 6 * NSA_KV_HEADS * NSA_HD), ("misc", MISC_W))
C_OFF = {}
_o = 0
for _n, _w in C_ORDER:
    C_OFF[_n] = _o
    _o += _w
N_IN_PAD = _o
assert N_IN_PAD % PROJ_TN == 0
MISC_COLS = (("dn_b", 0, DN_HEADS), ("dn_a", DN_HEADS, DN_HEADS), ("nsa_gate", 2 * DN_HEADS, 3 * NSA_HEADS))
NEG_INF = float("-inf")


def _permute_w_in(w_in):
    src = dict(zip(IN_NAMES, jnp.split(w_in, IN_CUTS, axis=-1)))
    misc = jnp.concatenate([src[n] for n, _, _ in MISC_COLS], axis=-1)
    src["misc"] = jnp.pad(misc, ((0, 0), (0, 0), (0, MISC_W - misc.shape[-1])))
    return jnp.concatenate([src[n] for n, _ in C_ORDER], axis=-1)


def _row_tile(m):
    return 512 if m % 512 == 0 else m


def _norm_proj_kernel(x_ref, g_ref, w_ref, o_ref, h_ref):
    @pl.when(pl.program_id(1) == 0)
    def _():
        x = x_ref[...]
        y = x * lax.rsqrt(jnp.mean(x * x, axis=-1, keepdims=True) + EPS)
        h_ref[...] = (y * g_ref[...]).astype(jnp.bfloat16)

    o_ref[...] = jnp.dot(h_ref[...], w_ref[...], preferred_element_type=jnp.float32)


def _norm_proj(x, g, w_bf16):
    m, d = x.shape
    n = w_bf16.shape[1]
    tm = _row_tile(m)
    return pl.pallas_call(
        _norm_proj_kernel,
        grid=(m // tm, n // PROJ_TN),
        in_specs=[pl.BlockSpec((tm, d), lambda i, j: (i, 0)),
                  pl.BlockSpec((1, d), lambda i, j: (0, 0)),
                  pl.BlockSpec((d, PROJ_TN), lambda i, j: (0, j))],
        out_specs=pl.BlockSpec((tm, PROJ_TN), lambda i, j: (i, j)),
        out_shape=jax.ShapeDtypeStruct((m, n), jnp.float32),
        scratch_shapes=[pltpu.VMEM((tm, d), jnp.bfloat16)],
        compiler_params=pltpu.CompilerParams(dimension_semantics=("arbitrary", "arbitrary"),
                                             vmem_limit_bytes=VMEM_LIMIT),
        name="norm_proj",
    )(x, g.reshape(1, d), w_bf16)


def _merge_kernel(x_ref, ba_ref, bb_ref, bc_ref, bd_ref, gate_ref, wb_ref, wo_ref, o_ref):
    acc = None
    for n, b_ref in enumerate((ba_ref, bb_ref, bc_ref, bd_ref)):
        p = jnp.dot(b_ref[...].astype(jnp.bfloat16), wb_ref[n], preferred_element_type=jnp.float32)
        t = jax.nn.sigmoid(gate_ref[:, n * D_MODEL:(n + 1) * D_MODEL]) * p
        acc = t if acc is None else acc + t
    o_ref[...] = x_ref[...] + jnp.dot(acc.astype(jnp.bfloat16), wo_ref[...], preferred_element_type=jnp.float32)


def _merge_out(x, branches, c_all, gate_col, wb_bf16, wo_bf16):
    m, d = x.shape
    tm = 256 if m % 256 == 0 else m
    row = lambda w: pl.BlockSpec((tm, w), lambda i: (i, 0))
    return pl.pallas_call(
        _merge_kernel,
        grid=(m // tm,),
        in_specs=[row(d)] + [row(BRANCH_W)] * N_BRANCH + [
                  pl.BlockSpec((tm, N_BRANCH * d), lambda i: (i, gate_col // (N_BRANCH * d))),
                  pl.BlockSpec((N_BRANCH, BRANCH_W, d), lambda i: (0, 0, 0)),
                  pl.BlockSpec((d, d), lambda i: (0, 0))],
        out_specs=row(d),
        out_shape=jax.ShapeDtypeStruct((m, d), jnp.float32),
        compiler_params=pltpu.CompilerParams(dimension_semantics=("arbitrary",),
                                             vmem_limit_bytes=VMEM_LIMIT),
        name="merge_out",
    )(x, *branches, c_all, wb_bf16, wo_bf16)


MLP_TF = 1024


def _mlp_kernel(x_ref, g_ref, wu_ref, wd_ref, o_ref, h_ref, acc_ref):
    j = pl.program_id(1)

    @pl.when(j == 0)
    def _():
        x = x_ref[...]
        y = x * lax.rsqrt(jnp.mean(x * x, axis=-1, keepdims=True) + EPS)
        h_ref[...] = (y * g_ref[...]).astype(jnp.bfloat16)
        acc_ref[...] = jnp.zeros_like(acc_ref)

    u = jnp.maximum(jnp.dot(h_ref[...], wu_ref[...], preferred_element_type=jnp.float32), 0.0)
    acc_ref[...] += jnp.dot((u * u).astype(jnp.bfloat16), wd_ref[...], preferred_element_type=jnp.float32)

    @pl.when(j == pl.num_programs(1) - 1)
    def _():
        o_ref[...] = x_ref[...] + acc_ref[...]


def _mlp(x, g, wu_bf16, wd_bf16):
    m, d = x.shape
    f = wu_bf16.shape[1]
    tm = _row_tile(m)
    return pl.pallas_call(
        _mlp_kernel,
        grid=(m // tm, f // MLP_TF),
        in_specs=[pl.BlockSpec((tm, d), lambda i, j: (i, 0)),
                  pl.BlockSpec((1, d), lambda i, j: (0, 0)),
                  pl.BlockSpec((d, MLP_TF), lambda i, j: (0, j)),
                  pl.BlockSpec((MLP_TF, d), lambda i, j: (j, 0))],
        out_specs=pl.BlockSpec((tm, d), lambda i, j: (i, 0)),
        out_shape=jax.ShapeDtypeStruct((m, d), jnp.float32),
        scratch_shapes=[pltpu.VMEM((tm, d), jnp.bfloat16), pltpu.VMEM((tm, d), jnp.float32)],
        compiler_params=pltpu.CompilerParams(dimension_semantics=("arbitrary", "arbitrary"),
                                             vmem_limit_bytes=VMEM_LIMIT),
        name="mlp",
    )(x, g.reshape(1, d), wu_bf16, wd_bf16)


NSA_TQ = 128
NSA_TK = 512
NSA_ROWS = NSA_HEADS * NSA_TQ
NSA_WSPAN = WINDOW + NSA_TQ


def _dot(a, b, **kw):
    return jnp.dot(a, b, preferred_element_type=jnp.float32, **kw)


def _dot_nt(a, b):
    return lax.dot_general(a, b, (((1,), (1,)), ((), ())), preferred_element_type=jnp.float32)


def rope_tables(pos):
    half = ROPE_DIM // 2
    inv_freq = ROPE_THETA ** (-jnp.arange(half, dtype=jnp.float32) / half)
    ang = pos.astype(jnp.float32)[:, None] * inv_freq
    cos, sin = jnp.cos(ang), jnp.sin(ang)
    n = pos.shape[0]
    one = jnp.ones((n, NSA_HD - ROPE_DIM), jnp.float32)
    zero = jnp.zeros((n, NSA_HD - ROPE_DIM), jnp.float32)
    z8 = jnp.zeros((n, half), jnp.float32)
    c = jnp.concatenate([cos, cos, one], axis=1)
    s1 = jnp.concatenate([-sin, z8, zero], axis=1)
    s2 = jnp.concatenate([z8, sin, zero], axis=1)
    two = lambda a: jnp.concatenate([a, a], axis=1)
    return two(c), two(s1), two(s2)


def _rope(x, c, s1, s2):
    n = x.shape[-1]
    return x * c + pltpu.roll(x, n - ROPE_DIM // 2, 1) * s1 + pltpu.roll(x, ROPE_DIM // 2, 1) * s2


def _nsa_prep_kernel(kv0_ref, kv1_ref, kv2_ref, c_ref, s1_ref, s2_ref, rows_ref, win_ref, kvb_ref):
    c, s1, s2 = c_ref[...], s1_ref[...], s2_ref[...]
    cmp_kv = kv0_ref[...]
    sel = kv1_ref[...]
    wnd = kv2_ref[...]
    ks = _rope(sel[:, :LANE], c, s1, s2)
    kw = _rope(wnd[:, :LANE], c, s1, s2)
    rows_ref[:, 0:2 * LANE] = cmp_kv
    rows_ref[:, 2 * LANE:3 * LANE] = ks
    rows_ref[:, 3 * LANE:4 * LANE] = sel[:, LANE:]
    win_ref[:, 0:LANE] = kw
    win_ref[:, LANE:2 * LANE] = wnd[:, LANE:]
    kvb_ref[:, 0:LANE] = ks.astype(MXU_DT)
    kvb_ref[:, LANE:2 * LANE] = sel[:, LANE:].astype(MXU_DT)
    kvb_ref[:, 2 * LANE:3 * LANE] = kw.astype(MXU_DT)
    kvb_ref[:, 3 * LANE:4 * LANE] = wnd[:, LANE:].astype(MXU_DT)


def nsa_prep(c_all, kv_col0, tabs, seq_len):
    m = c_all.shape[0]
    tm = 512 if seq_len % 512 == 0 else seq_len
    nlt = seq_len // tm
    cb = kv_col0 // (2 * LANE)
    kv_spec = lambda k: pl.BlockSpec((tm, 2 * LANE), lambda i, k=k: (i, cb + k))
    tab_spec = pl.BlockSpec((tm, LANE), lambda i: (i % nlt, 0))
    return pl.pallas_call(
        _nsa_prep_kernel,
        grid=(m // tm,),
        in_specs=[kv_spec(0), kv_spec(1), kv_spec(2), tab_spec, tab_spec, tab_spec],
        out_specs=[pl.BlockSpec((tm, 4 * LANE), lambda i: (i, 0)),
                   pl.BlockSpec((tm, 2 * LANE), lambda i: (i, 0)),
                   pl.BlockSpec((tm, 4 * LANE), lambda i: (i, 0))],
        out_shape=[jax.ShapeDtypeStruct((m, 4 * LANE), jnp.float32),
                   jax.ShapeDtypeStruct((m, 2 * LANE), jnp.float32),
                   jax.ShapeDtypeStruct((m, 4 * LANE), MXU_DT)],
        compiler_params=pltpu.CompilerParams(dimension_semantics=("arbitrary",), vmem_limit_bytes=VMEM_LIMIT),
        name="nsa_prep",
    )(c_all, c_all, c_all, *tabs)


def _nsa_compress_kernel(x_ref, pos_ref, w1_ref, w2_ref, o_ref):
    nh = x_ref.shape[0] // CMP_STRIDE
    top = jnp.zeros((nh, NSA_KV_HEADS * CMP_HIDDEN), jnp.float32)
    bot = jnp.zeros((nh, NSA_KV_HEADS * CMP_HIDDEN), jnp.float32)
    for t in range(CMP_STRIDE):
        x = x_ref[pl.ds(t, nh, stride=CMP_STRIDE), :]
        top = top + _dot((x + pos_ref[0, t:t + 1, :]).astype(MXU_DT), w1_ref[0, 0, t])
        bot = bot + _dot((x + pos_ref[0, CMP_STRIDE + t:CMP_STRIDE + t + 1, :]).astype(MXU_DT), w1_ref[0, 1, t])
    h = top + pltpu.roll(bot, nh - 1, 0)
    h = h * jax.nn.sigmoid(h)
    o_ref[0, 0] = _dot(h.astype(MXU_DT), w2_ref[0]).astype(o_ref.dtype)


def _block_diag2(w):
    z = jnp.zeros_like(w)
    return jnp.concatenate([jnp.concatenate([w, z], axis=-1), jnp.concatenate([z, w], axis=-1)], axis=-2)


def nsa_compress(seq_rows, cmp_pos, cmp_w1, cmp_w2):
    b, seq_len = seq_rows.shape[:2]
    nh = seq_len // CMP_STRIDE
    pos = jnp.concatenate([cmp_pos, cmp_pos], axis=-1)
    w1 = cmp_w1.reshape(2, 2, CMP_STRIDE, NSA_HD, CMP_HIDDEN)
    w1 = _block_diag2(w1).astype(MXU_DT)
    w2 = _block_diag2(cmp_w2).astype(MXU_DT)
    return pl.pallas_call(
        _nsa_compress_kernel,
        grid=(b, 2),
        in_specs=[pl.BlockSpec((None, seq_len, LANE), lambda i, j: (i, 0, j)),
                  pl.BlockSpec((1, CMP_BLOCK, LANE), lambda i, j: (j, 0, 0)),
                  pl.BlockSpec((1, 2, CMP_STRIDE, LANE, NSA_KV_HEADS * CMP_HIDDEN), lambda i, j: (j, 0, 0, 0, 0)),
                  pl.BlockSpec((1, NSA_KV_HEADS * CMP_HIDDEN, LANE), lambda i, j: (j, 0, 0))],
        out_specs=pl.BlockSpec((1, 1, nh, LANE), lambda i, j: (i, j, 0, 0)),
        out_shape=jax.ShapeDtypeStruct((b, 2, nh, LANE), MXU_DT),
        compiler_params=pltpu.CompilerParams(dimension_semantics=("arbitrary", "arbitrary"), vmem_limit_bytes=VMEM_LIMIT),
        name="nsa_compress",
    )(seq_rows, pos, w1, w2)


def _pad_heads(q):
    lane = lax.broadcasted_iota(jnp.int32, (q.shape[0], LANE), 1)
    blocks = []
    for h in range(NSA_HEADS):
        blk = q[:, (h // 2) * LANE:(h // 2 + 1) * LANE]
        g = h // NSA_GROUP
        if h % 2 != g:
            blk = pltpu.roll(blk, NSA_HD, 1)
        keep = (lane < NSA_HD) if g == 0 else (lane >= NSA_HD)
        blocks.append(jnp.where(keep, blk, 0.0))
    return jnp.concatenate(blocks, axis=0)


def _softmax_rows(s):
    m = jnp.max(s, axis=-1, keepdims=True)
    m = jnp.where(m == NEG_INF, 0.0, m)
    p = jnp.exp(s - m)
    return p / jnp.maximum(jnp.sum(p, axis=-1, keepdims=True), 1e-30)


def _nsa_attn_kernel(q_ref, gate_ref, c_ref, s1_ref, s2_ref, ck_ref, cv_ref, ks_ref, vs_ref, kn_ref, kw_ref, vw_ref,
                     o_ref, m_ref, l_ref, acc_ref, *, q_start, win_start, ns, n_new):
    tq = q_ref.shape[1]
    n_rows = NSA_HEADS * tq
    n_keys = ks_ref.shape[1]
    i = pl.program_id(1)
    q0 = q_start + i * tq
    scale = NSA_HD ** -0.5
    q = q_ref[0] * scale
    c = jnp.concatenate([c_ref[...]] * 4, axis=1)
    s1 = jnp.concatenate([s1_ref[...]] * 4, axis=1)
    s2 = jnp.concatenate([s2_ref[...]] * 4, axis=1)
    q_raw = _pad_heads(q).astype(MXU_DT)
    q_rot = _pad_heads(_rope(q, c, s1, s2)).astype(MXU_DT)

    row = lax.broadcasted_iota(jnp.int32, (n_rows, 1), 0)
    qpos = q0 + (row & (tq - 1))

    nc = ck_ref.shape[1]
    s_c = _dot_nt(q_raw, ck_ref[0])
    c_end = lax.broadcasted_iota(jnp.int32, (1, nc), 1) * CMP_STRIDE + (CMP_BLOCK - 1)
    p_c = _softmax_rows(jnp.where(c_end <= qpos, s_c, NEG_INF))
    o_c = _dot(p_c.astype(MXU_DT), cv_ref[0])

    psum = jnp.concatenate(
        [sum(p_c[(g * NSA_GROUP + j) * tq:(g * NSA_GROUP + j + 1) * tq] for j in range(NSA_GROUP))
         for g in range(NSA_KV_HEADS)], axis=0)
    n_i = lax.broadcasted_iota(jnp.int32, (nc, ns), 0) * CMP_STRIDE
    m_i = lax.broadcasted_iota(jnp.int32, (nc, ns), 1) * SEL_BLOCK
    overlap = ((n_i <= m_i + (SEL_BLOCK - 1)) & (n_i + (CMP_BLOCK - 1) >= m_i)).astype(jnp.float32)
    imp = _dot(psum, overlap, precision=lax.Precision.HIGHEST)
    r2 = lax.broadcasted_iota(jnp.int32, (NSA_KV_HEADS * tq, 1), 0)
    qpos2 = q0 + (r2 & (tq - 1))
    cur = qpos2 >> 6
    blk = lax.broadcasted_iota(jnp.int32, (1, ns), 1)
    forced = (blk == 0) | (blk == cur) | (blk == cur - 1)
    valid = blk * SEL_BLOCK <= qpos2
    v = jnp.where(forced, jnp.inf, jnp.where(valid, imp, NEG_INF))
    blk_f = blk.astype(jnp.float32)
    sel = jnp.zeros(v.shape, jnp.float32)
    for _ in range(SEL_TOPN):
        mx = jnp.max(v, axis=-1, keepdims=True)
        first = jnp.min(jnp.where(v == mx, blk_f, float(ns)), axis=-1, keepdims=True)
        pick = blk_f == first
        sel = jnp.where(pick, 1.0, sel)
        v = jnp.where(pick, NEG_INF, v)
    sel_b = sel.astype(MXU_DT)

    m_ref[...] = jnp.full(m_ref.shape, NEG_INF, jnp.float32)
    l_ref[...] = jnp.zeros(l_ref.shape, jnp.float32)
    acc_ref[...] = jnp.zeros(acc_ref.shape, jnp.float32)
    n_kt = jnp.minimum((q0 + tq + NSA_TK - 1) // NSA_TK, n_keys // NSA_TK)

    def step(k, vv, tok0):
        nk = k.shape[0]
        s = _dot_nt(q_rot, k.astype(MXU_DT))
        tok = tok0 + lax.broadcasted_iota(jnp.int32, (1, nk), 1)
        e_m = lax.broadcasted_iota(jnp.int32, (ns, nk), 0)
        e_t = (tok0 + lax.broadcasted_iota(jnp.int32, (ns, nk), 1)) >> 6
        chosen = _dot(sel_b, (e_m == e_t).astype(MXU_DT))
        bias2 = jnp.where((chosen > 0.5) & (tok <= qpos2), 0.0, NEG_INF)
        bias = jnp.concatenate([bias2[0:tq]] * NSA_GROUP + [bias2[tq:2 * tq]] * NSA_GROUP, axis=0)
        s = s + bias
        m_old = m_ref[...]
        m_new = jnp.maximum(m_old, jnp.max(s, axis=-1, keepdims=True))
        m_safe = jnp.where(m_new == NEG_INF, 0.0, m_new)
        alpha = jnp.exp(m_old - m_safe)
        p = jnp.exp(s - m_safe)
        l_ref[...] = alpha * l_ref[...] + jnp.sum(p, axis=-1, keepdims=True)
        acc_ref[...] = alpha * acc_ref[...] + _dot(p.astype(MXU_DT), vv.astype(MXU_DT))
        m_ref[...] = m_new

    def body(kt, carry):
        koff = pl.multiple_of(kt * NSA_TK, NSA_TK)
        step(ks_ref[0, pl.ds(koff, NSA_TK), :], vs_ref[0, pl.ds(koff, NSA_TK), :], koff)
        return carry

    lax.fori_loop(0, n_kt, body, 0)
    if n_new:
        step(kn_ref[0, :, 0:LANE], kn_ref[0, :, LANE:2 * LANE], n_keys)
    o_s = acc_ref[...] / jnp.maximum(l_ref[...], 1e-30)

    w0 = jnp.clip(q0 - win_start - WINDOW, 0, kw_ref.shape[1] - NSA_WSPAN)
    w0 = pl.multiple_of(w0, tq)
    kw = kw_ref[0, pl.ds(w0, NSA_WSPAN), :]
    vw = vw_ref[0, pl.ds(w0, NSA_WSPAN), :]
    s_w = _dot_nt(q_rot, kw.astype(MXU_DT))
    kpos = win_start + w0 + lax.broadcasted_iota(jnp.int32, (1, NSA_WSPAN), 1)
    w_ok = (kpos <= qpos) & (kpos > qpos - WINDOW)
    p_w = _softmax_rows(jnp.where(w_ok, s_w, NEG_INF))
    o_w = _dot(p_w.astype(MXU_DT), vw.astype(MXU_DT))

    gates = jax.nn.sigmoid(gate_ref[0])
    lane = lax.broadcasted_iota(jnp.int32, (tq, LANE), 1)
    outs = []
    for h in range(NSA_HEADS):
        sl = slice(h * tq, (h + 1) * tq)
        k0 = 2 * DN_HEADS + 3 * h
        o_h = (gates[:, k0:k0 + 1] * o_c[sl] + gates[:, k0 + 1:k0 + 2] * o_s[sl] + gates[:, k0 + 2:k0 + 3] * o_w[sl])
        if h % 2 != h // NSA_GROUP:
            o_h = pltpu.roll(o_h, NSA_HD, 1)
        outs.append(o_h)
    for p2 in range(NSA_HEADS // 2):
        o_ref[0, :, p2 * LANE:(p2 + 1) * LANE] = jnp.where(lane < NSA_HD, outs[2 * p2], outs[2 * p2 + 1])


def nsa_attend(c_all, q_col0, misc_col0, tabs, ckv, sel_kv, sel_cols, new_kv, n_new, win_kv, win_cols, win_start,
               batch, seq_len, q_start):
    tq = NSA_TQ if seq_len % NSA_TQ == 0 else seq_len
    n_rows = NSA_HEADS * tq
    qb = q_col0 // (4 * LANE)
    mb = misc_col0 // LANE
    c3 = c_all.reshape(batch, seq_len, c_all.shape[1])
    n_blocks = -(-(sel_kv.shape[1] + n_new) // SEL_BLOCK)
    ns = -(-n_blocks // LANE) * LANE
    tab_spec = pl.BlockSpec((tq, LANE), lambda b, i: (i, 0))
    col_spec = lambda a, k: pl.BlockSpec((1, a.shape[1], LANE), lambda b, i, k=k: (b, 0, k))
    nc = ckv.shape[2]
    out = pl.pallas_call(
        functools.partial(_nsa_attn_kernel, q_start=q_start, win_start=win_start, ns=ns, n_new=n_new),
        grid=(batch, seq_len // tq),
        in_specs=[pl.BlockSpec((1, tq, 4 * LANE), lambda b, i: (b, i, qb)),
                  pl.BlockSpec((1, tq, LANE), lambda b, i: (b, i, mb)),
                  tab_spec, tab_spec, tab_spec,
                  pl.BlockSpec((None, 1, nc, LANE), lambda b, i: (b, 0, 0, 0)),
                  pl.BlockSpec((None, 1, nc, LANE), lambda b, i: (b, 1, 0, 0)),
                  col_spec(sel_kv, sel_cols[0]), col_spec(sel_kv, sel_cols[1]),
                  pl.BlockSpec((1, new_kv.shape[1], 2 * LANE), lambda b, i: (b, 0, 0)),
                  col_spec(win_kv, win_cols[0]), col_spec(win_kv, win_cols[1])],
        out_specs=pl.BlockSpec((1, tq, 4 * LANE), lambda b, i: (b, i, 0)),
        out_shape=jax.ShapeDtypeStruct((batch, seq_len, 4 * LANE), jnp.float32),
        scratch_shapes=[pltpu.VMEM((n_rows, 1), jnp.float32), pltpu.VMEM((n_rows, 1), jnp.float32),
                        pltpu.VMEM((n_rows, LANE), jnp.float32)],
        compiler_params=pltpu.CompilerParams(dimension_semantics=("arbitrary", "arbitrary"), vmem_limit_bytes=VMEM_LIMIT),
        name="nsa_attn",
    )(c3, c3, *tabs, ckv, ckv, sel_kv, sel_kv, new_kv, win_kv, win_kv)
    return out.reshape(batch * seq_len, 4 * LANE)


SCAN_C = 64
SCAN_TT = 512
HALO = 8


def _dot_tn(a, b):
    return lax.dot_general(a, b, (((0,), (0,)), ((), ())), preferred_element_type=jnp.float32)


def _mx(x):
    return x.astype(MXU_DT)


def _dot3(a, b):
    a_hi, b_hi = _mx(a), _mx(b)
    a_lo = _mx(a - a_hi.astype(jnp.float32))
    b_lo = _mx(b - b_hi.astype(jnp.float32))
    return _dot(a_hi, b_hi) + (_dot(a_hi, b_lo) + _dot(a_lo, b_hi))


def _chunk_cumsum(x):
    c = x.shape[0]
    row = lax.broadcasted_iota(jnp.int32, x.shape, 0)
    d = 1
    while d < c:
        x = x + jnp.where(row >= d, pltpu.roll(x, d, 0), 0.0)
        d *= 2
    return x


def _silu(x):
    return x * jax.nn.sigmoid(x)


def _conv_tile(x, halo, w_ref):
    kw = w_ref.shape[0]
    r8 = lax.broadcasted_iota(jnp.int32, (HALO, x.shape[1]), 0)
    y = x * w_ref[kw - 1:kw, :]
    for d in range(1, kw):
        xs = pltpu.roll(x, d, 0)
        head = jnp.where(r8 < d, pltpu.roll(halo, d, 0), xs[0:HALO])
        xs = jnp.concatenate([head, xs[HALO:]], axis=0)
        y = y + xs * w_ref[kw - 1 - d:kw - d, :]
    return y


def _halo_spec(tt, width, col_block):
    return pl.BlockSpec((HALO, width), lambda i: (jnp.maximum(i * (tt // HALO) - 1, 0), col_block))


def _pad_buf(buf):
    return jnp.pad(buf.astype(jnp.float32), ((0, 0), (HALO - buf.shape[1], 0), (0, 0)))


def _hgrn_prep_kernel(q_ref, f_ref, v_ref, la_ref, l1_ref, lbc_ref, oi_ref, qb_ref, kb_ref, eb_ref):
    c = SCAN_C
    tt = q_ref.shape[0]
    row = lax.broadcasted_iota(jnp.int32, (c, LANE), 0)
    ti = lax.broadcasted_iota(jnp.int32, (c, c), 0)
    si = lax.broadcasted_iota(jnp.int32, (c, c), 1)
    for h in range(HG_HEADS):
        ls = slice(h * LANE, (h + 1) * LANE)
        la, l1, lbc = la_ref[:, ls], l1_ref[:, ls], lbc_ref[:, ls]
        for ci in range(tt // c):
            rs = slice(ci * c, (ci + 1) * c)
            q = _silu(q_ref[rs, ls])
            z = f_ref[rs, ls]
            v = v_ref[rs, ls]
            lsig = jnp.minimum(z, 0.0) - jnp.log1p(jnp.exp(-jnp.abs(z)))
            t2 = l1 + lsig
            hi = jnp.maximum(la, t2)
            logf = hi + jnp.log1p(jnp.exp(-jnp.abs(la - t2)))
            k = lbc * jax.nn.sigmoid(-z)
            b = _chunk_cumsum(logf)
            b_last = b[c - 1:c, :]
            qb_ref[rs, ls] = q * jnp.exp(b)
            kb_ref[rs, ls] = k * jnp.exp(b_last - b)
            eb_ref[ci:ci + 1, ls] = jnp.exp(b_last)
            att = jnp.zeros((c, c), jnp.float32)
            n = c // 2
            while n >= 8:
                blk = 2 * n
                ref_rows = jnp.concatenate(
                    [jnp.broadcast_to(b[j * blk + n - 1:j * blk + n, :], (blk, LANE)) for j in range(c // blk)], axis=0)
                upper = (row & (blk - 1)) >= n
                qs = q * jnp.exp(jnp.where(upper, b - ref_rows, 0.0))
                ks = k * jnp.exp(jnp.where(upper, 0.0, ref_rows - b))
                lvl = _dot_nt(_mx(qs), _mx(ks))
                ok = ((ti & ~(blk - 1)) == (si & ~(blk - 1))) & ((ti & (blk - 1)) >= n) & ((si & (blk - 1)) < n)
                att = att + jnp.where(ok, lvl, 0.0)
                n //= 2
            o = _dot(_mx(att), _mx(v))
            for d in range(8):
                if d == 0:
                    w = jnp.sum(q * k, axis=-1, keepdims=True)
                    o = o + w * v
                else:
                    e = jnp.exp(jnp.minimum(b - pltpu.roll(b, d, 0), 0.0))
                    w = jnp.sum(q * pltpu.roll(k, d, 0) * e, axis=-1, keepdims=True)
                    w = jnp.where((row[:, 0:1] & 7) >= d, w, 0.0)
                    o = o + w * pltpu.roll(v, d, 0)
            oi_ref[rs, ls] = o


def hgrn_prep(c_all, col_q, col_f, col_v, lb):
    m = c_all.shape[0]
    tt = SCAN_TT
    lbf = lb.reshape(1, BRANCH_W).astype(jnp.float32)
    la, l1, lbc = jnp.log(lbf), jnp.log1p(-lbf), 1.0 - lbf
    blk = lambda col: pl.BlockSpec((tt, BRANCH_W), lambda i, col=col: (i, col // BRANCH_W))
    vec = pl.BlockSpec((1, BRANCH_W), lambda i: (0, 0))
    out = pl.BlockSpec((tt, BRANCH_W), lambda i: (i, 0))
    return pl.pallas_call(
        _hgrn_prep_kernel,
        grid=(m // tt,),
        in_specs=[blk(col_q), blk(col_f), blk(col_v), vec, vec, vec],
        out_specs=[out, out, out, pl.BlockSpec((tt // SCAN_C, BRANCH_W), lambda i: (i, 0))],
        out_shape=[jax.ShapeDtypeStruct((m, BRANCH_W), jnp.float32)] * 3
                  + [jax.ShapeDtypeStruct((m // SCAN_C, BRANCH_W), jnp.float32)],
        compiler_params=pltpu.CompilerParams(dimension_semantics=("arbitrary",), vmem_limit_bytes=VMEM_LIMIT),
        name="hgrn_prep",
    )(c_all, c_all, c_all, la, l1, lbc)


def _hgrn_scan_kernel(oi_ref, qb_ref, kb_ref, eb_ref, v_ref, g_ref, nw_ref, s0_ref, o_ref, sT_out_ref, sT_ref):
    c = SCAN_C
    nb = oi_ref.shape[0]
    tt = oi_ref.shape[1]
    i = pl.program_id(0)

    @pl.when(i == 0)
    def _():
        sT_ref[...] = s0_ref[...]

    nw = nw_ref[...]
    for ci in range(tt // c):
        rs = slice(ci * c, (ci + 1) * c)
        for b in range(nb):
            for h in range(HG_HEADS):
                ls = slice(h * LANE, (h + 1) * LANE)
                sT = sT_ref[b, h]
                o = oi_ref[b, rs, ls] + _dot_nt(_mx(qb_ref[b, rs, ls]), _mx(sT))
                sT_ref[b, h] = eb_ref[b, ci:ci + 1, ls] * sT + _dot_tn(_mx(v_ref[b, rs, ls]), _mx(kb_ref[b, rs, ls]))
                y = o * lax.rsqrt(jnp.mean(o * o, axis=-1, keepdims=True) + EPS) * nw
                o_ref[b, rs, ls] = y * _silu(g_ref[b, rs, ls])

    @pl.when(i == pl.num_programs(0) - 1)
    def _():
        sT_out_ref[...] = sT_ref[...]


def hgrn_scan(oi, qb, kb, eb, c_all, col_v, col_g, norm_w, s0, batch, seq_len):
    tt = SCAN_TT
    n3 = lambda a: a.reshape(batch, seq_len, a.shape[-1])
    c3 = n3(c_all)
    tok = pl.BlockSpec((batch, tt, BRANCH_W), lambda i: (0, i, 0))
    ctok = lambda col: pl.BlockSpec((batch, tt, BRANCH_W), lambda i, col=col: (0, i, col // BRANCH_W))
    st = pl.BlockSpec((batch, HG_HEADS, LANE, LANE), lambda i: (0, 0, 0, 0))
    o, sT = pl.pallas_call(
        _hgrn_scan_kernel,
        grid=(seq_len // tt,),
        in_specs=[tok, tok, tok, pl.BlockSpec((batch, tt // SCAN_C, BRANCH_W), lambda i: (0, i, 0)),
                  ctok(col_v), ctok(col_g), pl.BlockSpec((1, LANE), lambda i: (0, 0)), st],
        out_specs=[tok, st],
        out_shape=[jax.ShapeDtypeStruct((batch, seq_len, BRANCH_W), jnp.float32),
                   jax.ShapeDtypeStruct((batch, HG_HEADS, LANE, LANE), jnp.float32)],
        scratch_shapes=[pltpu.VMEM((batch, HG_HEADS, LANE, LANE), jnp.float32)],
        compiler_params=pltpu.CompilerParams(dimension_semantics=("arbitrary",), vmem_limit_bytes=VMEM_LIMIT),
        name="hgrn_scan",
    )(n3(oi), n3(qb), n3(kb), eb.reshape(batch, seq_len // SCAN_C, BRANCH_W), c3, c3,
      norm_w.reshape(1, LANE), jnp.swapaxes(s0, -1, -2))
    return o.reshape(batch * seq_len, BRANCH_W), jnp.swapaxes(sT, -1, -2)


def _gdn_prep_kernel(x_ref, halo_ref, buf_ref, misc_ref, cw_ref, alog_ref, dtb_ref,
                     u_ref, w_ref, qg_ref, kg_ref, qk_ref, eg_ref, *, tiles_per_seq):
    c = SCAN_C
    tt = x_ref.shape[0]
    first = (pl.program_id(0) % tiles_per_seq) == 0
    halo = jnp.where(first, buf_ref[0], halo_ref[...])
    y = _silu(_conv_tile(x_ref[...], halo, cw_ref))
    misc = misc_ref[...]
    beta_all = jax.nn.sigmoid(misc)
    sp_in = misc + dtb_ref[...]
    sp = jnp.maximum(sp_in, 0.0) + jnp.log1p(jnp.exp(-jnp.abs(sp_in)))
    g_all = -jnp.exp(alog_ref[...]) * sp
    ti = lax.broadcasted_iota(jnp.int32, (c, c), 0)
    si = lax.broadcasted_iota(jnp.int32, (c, c), 1)
    eye = (ti == si).astype(jnp.float32)
    for ci in range(tt // c):
        rs = slice(ci * c, (ci + 1) * c)
        gam_all = _chunk_cumsum(g_all[rs])
        gam_t = gam_all.T
        for h in range(DN_HEADS):
            ls = slice(h * LANE, (h + 1) * LANE)
            q = y[rs, h * LANE:(h + 1) * LANE]
            k = y[rs, BRANCH_W + h * LANE:BRANCH_W + (h + 1) * LANE]
            v = y[rs, 2 * BRANCH_W + h * LANE:2 * BRANCH_W + (h + 1) * LANE]
            q = q * lax.rsqrt(jnp.sum(q * q, axis=-1, keepdims=True) + EPS) * (DN_DK ** -0.5)
            k = k * lax.rsqrt(jnp.sum(k * k, axis=-1, keepdims=True) + EPS)
            beta = beta_all[rs, h:h + 1]
            gam = gam_all[:, DN_HEADS + h:DN_HEADS + h + 1]
            gam_r = gam_t[DN_HEADS + h:DN_HEADS + h + 1, :]
            decay = jnp.exp(jnp.where(si <= ti, gam - gam_r, NEG_INF))
            kb16 = _mx(k)
            kk = _dot_nt(kb16, kb16)
            a = jnp.where(si < ti, beta * kk * decay, 0.0)
            tinv = eye
            s = 1
            while s < c:
                blk = 2 * s
                off = (((ti & ~(blk - 1)) == (si & ~(blk - 1))) & ((ti & (blk - 1)) >= s) & ((si & (blk - 1)) < s))
                a_off = jnp.where(off, a, 0.0)
                tinv = tinv - (a_off if s == 1 else _dot3(_dot3(tinv, a_off), tinv))
                s = blk
            e_gam = jnp.exp(gam)
            rhs = jnp.concatenate([v * beta, k * (beta * e_gam)], axis=1)
            sol = rhs + _dot3(tinv - eye, rhs)
            u_ref[rs, ls] = sol[:, :LANE]
            w_ref[rs, ls] = sol[:, LANE:]
            qg_ref[rs, ls] = q * e_gam
            g_last = gam[c - 1:c, :]
            kg_ref[rs, ls] = k * jnp.exp(g_last - gam)
            qk_ref[rs, h * c:(h + 1) * c] = _dot_nt(_mx(q), kb16) * decay
            eg_ref[ci:ci + 1, ls] = jnp.broadcast_to(jnp.exp(g_last), (1, LANE))


def gdn_prep(c_all, col_x, col_misc, conv_buf, conv_w, a_log, dt_bias, batch, seq_len):
    m = c_all.shape[0]
    tt = SCAN_TT
    tps = seq_len // tt
    xw = 3 * BRANCH_W
    lanes = jnp.zeros((1, LANE), jnp.float32)
    alog = lanes.at[0, DN_HEADS:2 * DN_HEADS].set(a_log.astype(jnp.float32))
    dtb = lanes.at[0, DN_HEADS:2 * DN_HEADS].set(dt_bias.astype(jnp.float32))
    out = pl.BlockSpec((tt, BRANCH_W), lambda i: (i, 0))
    vec = pl.BlockSpec((1, LANE), lambda i: (0, 0))
    return pl.pallas_call(
        functools.partial(_gdn_prep_kernel, tiles_per_seq=tps),
        grid=(m // tt,),
        in_specs=[pl.BlockSpec((tt, xw), lambda i: (i, col_x // xw)),
                  _halo_spec(tt, xw, col_x // xw),
                  pl.BlockSpec((1, HALO, xw), lambda i: (i // tps, 0, 0)),
                  pl.BlockSpec((tt, LANE), lambda i: (i, col_misc // LANE)),
                  pl.BlockSpec((DN_CONV, xw), lambda i: (0, 0)), vec, vec],
        out_specs=[out, out, out, out, pl.BlockSpec((tt, DN_HEADS * SCAN_C), lambda i: (i, 0)),
                   pl.BlockSpec((tt // SCAN_C, BRANCH_W), lambda i: (i, 0))],
        out_shape=[jax.ShapeDtypeStruct((m, BRANCH_W), jnp.float32)] * 4
                  + [jax.ShapeDtypeStruct((m, DN_HEADS * SCAN_C), jnp.float32),
                     jax.ShapeDtypeStruct((m // SCAN_C, BRANCH_W), jnp.float32)],
        compiler_params=pltpu.CompilerParams(dimension_semantics=("arbitrary",), vmem_limit_bytes=VMEM_LIMIT),
        name="gdn_prep",
    )(c_all, c_all, _pad_buf(conv_buf), c_all, conv_w.astype(jnp.float32), alog, dtb)


def _gdn_scan_kernel(u_ref, w_ref, qg_ref, kg_ref, qk_ref, eg_ref, z_ref, nw_ref, s0_ref, o_ref, s_out_ref, s_ref):
    c = SCAN_C
    nb = u_ref.shape[0]
    tt = u_ref.shape[1]
    i = pl.program_id(0)

    @pl.when(i == 0)
    def _():
        s_ref[...] = s0_ref[...]

    nw = nw_ref[...]
    for ci in range(tt // c):
        rs = slice(ci * c, (ci + 1) * c)
        for b in range(nb):
            for h in range(DN_HEADS):
                ls = slice(h * LANE, (h + 1) * LANE)
                s = s_ref[b, h]
                s16 = _mx(s)
                both = _dot(_mx(jnp.concatenate([qg_ref[b, rs, ls], w_ref[b, rs, ls]], axis=0)), s16)
                v_new = u_ref[b, rs, ls] - both[c:]
                v16 = _mx(v_new)
                o = both[:c] + _dot(_mx(qk_ref[b, rs, h * c:(h + 1) * c]), v16)
                s_ref[b, h] = eg_ref[b, ci:ci + 1, ls][:, 0:1] * s + _dot_tn(_mx(kg_ref[b, rs, ls]), v16)
                y = o * lax.rsqrt(jnp.mean(o * o, axis=-1, keepdims=True) + EPS) * nw
                o_ref[b, rs, ls] = y * _silu(z_ref[b, rs, ls])

    @pl.when(i == pl.num_programs(0) - 1)
    def _():
        s_out_ref[...] = s_ref[...]


def gdn_scan(u, w, qg, kg, qk, eg, c_all, col_z, norm_w, s0, batch, seq_len):
    tt = SCAN_TT
    n3 = lambda a: a.reshape(batch, seq_len, a.shape[-1])
    tok = pl.BlockSpec((batch, tt, BRANCH_W), lambda i: (0, i, 0))
    st = pl.BlockSpec((batch, DN_HEADS, LANE, LANE), lambda i: (0, 0, 0, 0))
    o, s = pl.pallas_call(
        _gdn_scan_kernel,
        grid=(seq_len // tt,),
        in_specs=[tok, tok, tok, tok, pl.BlockSpec((batch, tt, DN_HEADS * SCAN_C), lambda i: (0, i, 0)),
                  pl.BlockSpec((batch, tt // SCAN_C, BRANCH_W), lambda i: (0, i, 0)),
                  pl.BlockSpec((batch, tt, BRANCH_W), lambda i: (0, i, col_z // BRANCH_W)),
                  pl.BlockSpec((1, LANE), lambda i: (0, 0)), st],
        out_specs=[tok, st],
        out_shape=[jax.ShapeDtypeStruct((batch, seq_len, BRANCH_W), jnp.float32),
                   jax.ShapeDtypeStruct((batch, DN_HEADS, LANE, LANE), jnp.float32)],
        scratch_shapes=[pltpu.VMEM((batch, DN_HEADS, LANE, LANE), jnp.float32)],
        compiler_params=pltpu.CompilerParams(dimension_semantics=("arbitrary",), vmem_limit_bytes=VMEM_LIMIT),
        name="gdn_scan",
    )(n3(u), n3(w), n3(qg), n3(kg), n3(qk), eg.reshape(batch, seq_len // SCAN_C, BRANCH_W), n3(c_all),
      norm_w.reshape(1, LANE), s0.astype(jnp.float32))
    return o.reshape(batch * seq_len, BRANCH_W), s


def _sc_branch_kernel(x_ref, halo_ref, buf_ref, w_ref, o_ref, *, tiles_per_seq):
    first = (pl.program_id(0) % tiles_per_seq) == 0
    x = x_ref[...]
    hr = halo_ref[...]
    halo = jnp.where(first, buf_ref[0], hr[:, SC_W:2 * SC_W] * hr[:, 2 * SC_W:3 * SC_W])
    o_ref[...] = x[:, 0:SC_W] * _conv_tile(x[:, SC_W:2 * SC_W] * x[:, 2 * SC_W:3 * SC_W], halo, w_ref)


def sc_branch(c_all, col_x, conv_buf, conv_w, seq_len):
    m = c_all.shape[0]
    tt = SCAN_TT
    tps = seq_len // tt
    xw = 3 * SC_W
    return pl.pallas_call(
        functools.partial(_sc_branch_kernel, tiles_per_seq=tps),
        grid=(m // tt,),
        in_specs=[pl.BlockSpec((tt, xw), lambda i: (i, col_x // xw)),
                  _halo_spec(tt, xw, col_x // xw),
                  pl.BlockSpec((1, HALO, SC_W), lambda i: (i // tps, 0, 0)),
                  pl.BlockSpec((SC_CONV, SC_W), lambda i: (0, 0))],
        out_specs=pl.BlockSpec((tt, SC_W), lambda i: (i, 0)),
        out_shape=jax.ShapeDtypeStruct((m, SC_W), jnp.float32),
        compiler_params=pltpu.CompilerParams(dimension_semantics=("arbitrary",), vmem_limit_bytes=VMEM_LIMIT),
        name="sc_branch",
    )(c_all, c_all, _pad_buf(conv_buf), conv_w.astype(jnp.float32))


def rms_norm(x, g):
    xf = x.astype(jnp.float32)
    y = xf * lax.rsqrt(jnp.mean(xf * xf, axis=-1, keepdims=True) + EPS)
    return (y * g.astype(jnp.float32)).astype(x.dtype)


def l2_normalize(x):
    xf = x.astype(jnp.float32)
    return xf * lax.rsqrt(jnp.sum(xf * xf, axis=-1, keepdims=True) + EPS)


def causal_conv(x, buf, w):
    k_w = w.shape[0]
    seq_len = x.shape[1]
    xp = jnp.concatenate([buf.astype(x.dtype), x], axis=1)
    y = sum(xp[:, j:j + seq_len] * w[j] for j in range(k_w))
    return y, xp[:, seq_len:]


def to_chunks(a, c):
    b, seq_len = a.shape[:2]
    n = -(-seq_len // c)
    a = jnp.pad(a, [(0, 0), (0, n * c - seq_len)] + [(0, 0)] * (a.ndim - 2))
    return jnp.moveaxis(a.reshape((b, n, c) + a.shape[2:]), 1, 0)


def from_chunks(a, seq_len):
    n, b, c = a.shape[:3]
    return jnp.moveaxis(a, 0, 1).reshape((b, n * c) + a.shape[3:])[:, :seq_len]


def hgrn2_scan(q, k, v, logf, s0):
    seq_len = q.shape[1]
    c = min(HG_CHUNK, seq_len)
    tri = jnp.tril(jnp.ones((c, c), bool))[None, :, :, None, None]

    def step(s, inp):
        qc, kc, vc, gc = inp
        b = jnp.cumsum(gc, axis=1)
        decay = jnp.exp(jnp.where(tri, b[:, :, None] - b[:, None], -jnp.inf))
        att = jnp.einsum('bthk,btshk->btsh', qc, decay * kc[:, None])
        o = jnp.einsum('btsh,bshv->bthv', att, vc) + jnp.einsum('bthk,bhkv->bthv', qc * jnp.exp(b), s)
        b_last = b[:, -1]
        s = jnp.exp(b_last)[..., None] * s + jnp.einsum('bshk,bshv->bhkv', kc * jnp.exp(b_last[:, None] - b), vc)
        return s, o

    xs = tuple(to_chunks(a.astype(jnp.float32), c) for a in (q, k, v, logf))
    s, o = lax.scan(step, s0.astype(jnp.float32), xs)
    return from_chunks(o, seq_len), s


def gated_delta_scan(q, k, v, beta, g, s0):
    seq_len = q.shape[1]
    c = min(DN_CHUNK, seq_len)
    incl = jnp.tril(jnp.ones((c, c), bool))
    strict = jnp.tril(jnp.ones((c, c), bool), -1)
    eye = jnp.eye(c, dtype=jnp.float32)

    def step(s, inp):
        qc, kc, vc, bc, gc = inp
        qh, kh, vh = (jnp.swapaxes(a, 1, 2) for a in (qc, kc, vc))
        bh = jnp.swapaxes(bc, 1, 2)
        gam = jnp.cumsum(jnp.swapaxes(gc, 1, 2), axis=-1)
        decay = jnp.exp(jnp.where(incl, gam[..., :, None] - gam[..., None, :], -jnp.inf))
        kk = jnp.einsum('bhtk,bhsk->bhts', kh, kh)
        t_mat = eye + jnp.where(strict, bh[..., :, None] * kk * decay, 0.0)
        u = lax.linalg.triangular_solve(t_mat, vh * bh[..., None], left_side=True, lower=True)
        w = lax.linalg.triangular_solve(t_mat, kh * (bh * jnp.exp(gam))[..., None], left_side=True, lower=True)
        v_new = u - jnp.einsum('bhtk,bhkv->bhtv', w, s)
        qk = jnp.einsum('bhtk,bhsk->bhts', qh, kh) * decay
        o = jnp.einsum('bhtk,bhkv->bhtv', qh * jnp.exp(gam)[..., None], s) + jnp.einsum('bhts,bhsv->bhtv', qk, v_new)
        g_last = gam[..., -1]
        s = jnp.exp(g_last)[..., None, None] * s + jnp.einsum('bhtk,bhtv->bhkv', kh * jnp.exp(g_last[..., None] - gam)[..., None], v_new)
        return s, jnp.swapaxes(o, 1, 2)

    xs = tuple(to_chunks(a.astype(jnp.float32), c) for a in (q, k, v, beta, g))
    s, o = lax.scan(step, s0.astype(jnp.float32), xs)
    return from_chunks(o, seq_len), s


def trunk_layer(x, start, hg_s, dn_s, dn_buf, sc_buf, past_rows, win_buf, w):
    b, seq_len, _ = x.shape
    f32 = jnp.float32
    x2 = x.reshape(b * seq_len, D_MODEL)
    m = b * seq_len
    c2 = _norm_proj(x2, w["norm_mix"], w["w_in"])
    c_all = c2.reshape(b, seq_len, N_IN_PAD)
    lb = w["lb"]
    if past_rows is None:
        oi, qb, kb, eb = hgrn_prep(c2, C_OFF["hg_q"], C_OFF["hg_f"], C_OFF["hg_i"], lb)
        o_a, hg_s = hgrn_scan(oi, qb, kb, eb, c2, C_OFF["hg_i"], C_OFF["hg_g"], w["hg_norm"], hg_s, b, seq_len)
        u, wy, qg, kg, qk, eg = gdn_prep(c2, C_OFF["dn_qkv"], C_OFF["misc"], dn_buf, w["dn_conv"], w["dn_a_log"],
                                         w["dn_dt_bias"], b, seq_len)
        o_b, dn_s = gdn_scan(u, wy, qg, kg, qk, eg, c2, C_OFF["dn_z"], w["dn_norm"], dn_s, b, seq_len)
        o_c = sc_branch(c2, C_OFF["sc_bch"], sc_buf, w["sc_conv"], seq_len)
        tail = c_all[:, seq_len - (DN_CONV - 1):]
        dn_buf = tail[..., C_OFF["dn_qkv"]:C_OFF["dn_qkv"] + 3 * BRANCH_W]
        sc_t = tail[:, DN_CONV - SC_CONV:, C_OFF["sc_bch"] + SC_W:C_OFF["sc_bch"] + 3 * SC_W]
        sc_buf = sc_t[..., :SC_W] * sc_t[..., SC_W:]
    else:
        o_a, o_b, o_c, (hg_s, dn_s, dn_buf, sc_buf) = cached_mixers(x, c_all, hg_s, dn_s, dn_buf, sc_buf, w)
    tabs = rope_tables(start + jnp.arange(seq_len))
    rows, win_rows, kvb = nsa_prep(c2, C_OFF["nsa_kv"], tabs, seq_len)
    rows = rows.reshape(b, seq_len, 4 * LANE)
    win_rows = win_rows.reshape(b, seq_len, 2 * LANE)
    kvb = kvb.reshape(b, seq_len, 4 * LANE)
    nsa_cols = (c2, C_OFF["nsa_q"], C_OFF["misc"], tabs)
    cmp_w = (w["cmp_pos"], w["cmp_w1"], w["cmp_w2"])
    if past_rows is None:
        ckv = nsa_compress(rows, *cmp_w)
        no_new = jnp.zeros((b, LANE, 2 * LANE), MXU_DT)
        o_d = nsa_attend(*nsa_cols, ckv, kvb, (0, 1), no_new, 0, kvb, (2, 3), 0, b, seq_len, 0)
        new_win = win_rows[:, max(seq_len - WINDOW, 0):]
    else:
        n_past = past_rows.shape[1]
        assert n_past % CMP_STRIDE == 0 and seq_len < CMP_STRIDE
        past = past_rows.reshape(b, n_past, 4 * LANE)
        ckv = nsa_compress(past, *cmp_w)
        new_kv = jnp.pad(kvb[:, :, 0:2 * LANE], ((0, 0), (0, LANE - seq_len), (0, 0)))
        nb = win_buf.shape[1]
        w_all = jnp.concatenate([win_buf.reshape(b, nb, 2 * LANE).astype(jnp.float32), win_rows], axis=1)
        w_pad = jnp.pad(w_all, ((0, 0), (0, max(NSA_WSPAN - nb - seq_len, 0)), (0, 0)))
        o_d = nsa_attend(*nsa_cols, ckv, past, (2, 3), new_kv, seq_len, w_pad, (0, 1), start - nb, b, seq_len, start)
        new_win = w_all[:, seq_len:]
    rows = rows.reshape(b, seq_len, 4, NSA_KV_HEADS, NSA_HD)
    new_win = new_win.reshape(b, new_win.shape[1], 2, NSA_KV_HEADS, NSA_HD)
    branches = [o_a.reshape(m, BRANCH_W), o_b.reshape(m, BRANCH_W), o_c.reshape(m, BRANCH_W), o_d.reshape(m, BRANCH_W)]
    x2 = _merge_out(x2, branches, c2, C_OFF["merge_gate"], w["w_branch"], w["w_out"])
    x2 = _mlp(x2, w["norm_mlp"], w["w_up"], w["w_down"])
    return x2.reshape(b, seq_len, D_MODEL), (hg_s, dn_s, dn_buf, sc_buf, new_win, rows)


def cached_mixers(x, c_all, hg_s, dn_s, dn_buf, sc_buf, w):
    b, seq_len, _ = x.shape
    f32 = jnp.float32
    c = {n: c_all[..., C_OFF[n]:C_OFF[n] + wd] for n, wd in C_ORDER}
    for n, o0, wd in MISC_COLS:
        c[n] = c["misc"][..., o0:o0 + wd]
    lb = w["lb"]
    hq = jax.nn.silu(c["hg_q"]).reshape(b, seq_len, HG_HEADS, HG_DK)
    z = c["hg_f"].astype(f32).reshape(b, seq_len, HG_HEADS, HG_DK)
    logf = jnp.logaddexp(jnp.log(lb), jnp.log1p(-lb) + jax.nn.log_sigmoid(z))
    hk = (1.0 - lb) * jax.nn.sigmoid(-z)
    hv = c["hg_i"].reshape(b, seq_len, HG_HEADS, HG_DV)
    o_a, hg_s = hgrn2_scan(hq, hk, hv, logf, hg_s)
    o_a = rms_norm(o_a.astype(x.dtype), w["hg_norm"]) * jax.nn.silu(c["hg_g"].reshape(b, seq_len, HG_HEADS, HG_DV))
    qkv, dn_buf = causal_conv(c["dn_qkv"], dn_buf, w["dn_conv"])
    dq, dk, dv = jnp.split(jax.nn.silu(qkv), 3, axis=-1)
    dq = l2_normalize(dq.reshape(b, seq_len, DN_HEADS, DN_DK)) * DN_DK ** -0.5
    dk = l2_normalize(dk.reshape(b, seq_len, DN_HEADS, DN_DK))
    beta = jax.nn.sigmoid(c["dn_b"].astype(f32))
    g = -jnp.exp(w["dn_a_log"].astype(f32)) * jax.nn.softplus(c["dn_a"].astype(f32) + w["dn_dt_bias"])
    o_b, dn_s = gated_delta_scan(dq, dk, dv.reshape(b, seq_len, DN_HEADS, DN_DV), beta, g, dn_s)
    o_b = rms_norm(o_b.astype(x.dtype), w["dn_norm"]) * jax.nn.silu(c["dn_z"].reshape(b, seq_len, DN_HEADS, DN_DV))
    gb, gc, hx = jnp.split(c["sc_bch"], 3, axis=-1)
    conv, sc_buf = causal_conv(gc * hx, sc_buf, w["sc_conv"])
    o_c = gb * conv
    return o_a, o_b, o_c, (hg_s, dn_s, dn_buf, sc_buf)


def kernel(x_prompt, x_sample, state_hgrn, state_dn, state_dn_conv, state_sc_conv, state_win_kv, cache_kv, page_table, norm_mix, norm_mlp, norm_final, w_in, hg_lb_logits, hg_norm, dn_conv, dn_a_log, dn_dt_bias, dn_norm, sc_conv, cmp_pos, cmp_w1, cmp_w2, w_branch, w_out, w_up, w_down):
    f32 = jnp.float32
    bf16 = jnp.bfloat16
    lbs = jnp.cumsum(jax.nn.softmax(hg_lb_logits.astype(f32), axis=0), axis=0)
    lbs = lbs - lbs[:1]
    w_in_b = _permute_w_in(w_in).astype(bf16)
    w_branch_b, w_out_b, w_up_b, w_down_b = (a.astype(bf16) for a in (w_branch, w_out, w_up, w_down))

    def layer_w(l):
        return dict(norm_mix=norm_mix[l], norm_mlp=norm_mlp[l], w_in=w_in_b[l], lb=lbs[l], hg_norm=hg_norm[l],
                    dn_conv=dn_conv[l], dn_a_log=dn_a_log[l], dn_dt_bias=dn_dt_bias[l], dn_norm=dn_norm[l],
                    sc_conv=sc_conv[l], cmp_pos=cmp_pos[l], cmp_w1=cmp_w1[l], cmp_w2=cmp_w2[l],
                    w_branch=w_branch_b[l], w_out=w_out_b[l], w_up=w_up_b[l], w_down=w_down_b[l])

    bp = x_prompt.shape[0]
    yp = x_prompt
    p_st = []
    for l in range(DEPTH):
        yp, st = trunk_layer(yp, 0,
                             jnp.zeros((bp, HG_HEADS, HG_DK, HG_DV), f32),
                             jnp.zeros((bp, DN_HEADS, DN_DK, DN_DV), f32),
                             jnp.zeros((bp, DN_CONV - 1, 3 * BRANCH_W), x_prompt.dtype),
                             jnp.zeros((bp, SC_CONV - 1, SC_W), x_prompt.dtype),
                             None, None, layer_w(l))
        p_st.append(st)
    ys = x_sample
    s_st = []
    for l in range(DEPTH):
        past = cache_kv[l][page_table]
        past = past.reshape(page_table.shape[0], -1, *past.shape[3:])
        ys, st = trunk_layer(ys, past.shape[1], state_hgrn[l], state_dn[l], state_dn_conv[l], state_sc_conv[l],
                             past, state_win_kv[l], layer_w(l))
        s_st.append(st)
    p = [jnp.stack([st[i] for st in p_st]) for i in range(6)]
    s = [jnp.stack([st[i] for st in s_st]) for i in range(6)]
    return (rms_norm(yp, norm_final), rms_norm(ys, norm_final),
            p[0], p[1], p[2], p[3], p[4], p[5],
            s[0], s[1], s[2], s[3], s[4], s[5])
```

```python
import math, functools
import jax, jax.numpy as jnp
from jax import lax
import numpy as np
from jax.experimental import pallas as pl
from jax.experimental.pallas import tpu as pltpu

D_MODEL = 1024
DEPTH = 4
PAGE_SIZE = 128
N_BRANCH = 4
BRANCH_W = D_MODEL // 2
HG_HEADS = 4
HG_DK = BRANCH_W // HG_HEADS
HG_DV = BRANCH_W // HG_HEADS
HG_CHUNK = 64
DN_HEADS = 4
DN_DK = BRANCH_W // DN_HEADS
DN_DV = BRANCH_W // DN_HEADS
DN_CONV = 4
DN_CHUNK = 64
SC_W = BRANCH_W
SC_CONV = 3
NSA_HEADS = 8
NSA_KV_HEADS = 2
NSA_HD = BRANCH_W // NSA_HEADS
NSA_GROUP = NSA_HEADS // NSA_KV_HEADS
ROPE_DIM = NSA_HD // 4
ROPE_THETA = 500000.0
CMP_BLOCK = 32
CMP_STRIDE = 16
CMP_HIDDEN = 4 * NSA_HD
SEL_BLOCK = 64
SEL_TOPN = 16
WINDOW = 512
Q_BLOCK = 128
D_FF = 4 * D_MODEL
EPS = 1e-6

IN_SPLITS = (
    ("hg_q", BRANCH_W), ("hg_f", BRANCH_W), ("hg_i", BRANCH_W), ("hg_g", BRANCH_W),
    ("dn_qkv", 3 * BRANCH_W), ("dn_b", DN_HEADS), ("dn_a", DN_HEADS), ("dn_z", BRANCH_W),
    ("sc_bch", 3 * SC_W),
    ("nsa_q", NSA_HEADS * NSA_HD), ("nsa_kv", 6 * NSA_KV_HEADS * NSA_HD), ("nsa_gate", 3 * NSA_HEADS),
    ("merge_gate", N_BRANCH * D_MODEL),
)
IN_NAMES = tuple(n for n, _ in IN_SPLITS)
IN_CUTS = tuple(int(c) for c in np.cumsum([s for _, s in IN_SPLITS])[:-1])
N_IN = sum(s for _, s in IN_SPLITS)

LANE = 128
PROJ_TM = 1024
PROJ_TN = 1024
VMEM_LIMIT = 48 * 1024 * 1024
MXU_DT = jnp.bfloat16

MISC_W = 2 * LANE
C_ORDER = (("merge_gate", N_BRANCH * D_MODEL), ("hg_q", BRANCH_W), ("hg_f", BRANCH_W), ("hg_i", BRANCH_W),
           ("hg_g", BRANCH_W), ("dn_qkv", 3 * BRANCH_W), ("sc_bch", 3 * SC_W), ("dn_z", BRANCH_W),
           ("nsa_q", NSA_HEADS * NSA_HD), ("nsa_kv", 6 * NSA_KV_HEADS * NSA_HD), ("misc", MISC_W))
C_OFF = {}
_o = 0
for _n, _w in C_ORDER:
    C_OFF[_n] = _o
    _o += _w
N_IN_PAD = _o
assert N_IN_PAD % PROJ_TN == 0
MISC_COLS = (("dn_b", 0, DN_HEADS), ("dn_a", DN_HEADS, DN_HEADS), ("nsa_gate", 2 * DN_HEADS, 3 * NSA_HEADS))
NEG_INF = float("-inf")


def _permute_w_in(w_in):
    src = dict(zip(IN_NAMES, jnp.split(w_in, IN_CUTS, axis=-1)))
    misc = jnp.concatenate([src[n] for n, _, _ in MISC_COLS], axis=-1)
    src["misc"] = jnp.pad(misc, ((0, 0), (0, 0), (0, MISC_W - misc.shape[-1])))
    return jnp.concatenate([src[n] for n, _ in C_ORDER], axis=-1)


def _row_tile(m):
    return 512 if m % 512 == 0 else m


def _norm_proj_kernel(x_ref, g_ref, w_ref, o_ref, h_ref):
    @pl.when(pl.program_id(1) == 0)
    def _():
        x = x_ref[...]
        y = x * lax.rsqrt(jnp.mean(x * x, axis=-1, keepdims=True) + EPS)
        h_ref[...] = (y * g_ref[...]).astype(jnp.bfloat16)

    o_ref[...] = jnp.dot(h_ref[...], w_ref[...], preferred_element_type=jnp.float32)


def _norm_proj(x, g, w_bf16):
    m, d = x.shape
    n = w_bf16.shape[1]
    tm = PROJ_TM if m % PROJ_TM == 0 else m
    return pl.pallas_call(
        _norm_proj_kernel,
        grid=(m // tm, n // PROJ_TN),
        in_specs=[pl.BlockSpec((tm, d), lambda i, j: (i, 0)),
                  pl.BlockSpec((1, d), lambda i, j: (0, 0)),
                  pl.BlockSpec((d, PROJ_TN), lambda i, j: (0, j))],
        out_specs=pl.BlockSpec((tm, PROJ_TN), lambda i, j: (i, j)),
        out_shape=jax.ShapeDtypeStruct((m, n), jnp.float32),
        scratch_shapes=[pltpu.VMEM((tm, d), jnp.bfloat16)],
        compiler_params=pltpu.CompilerParams(dimension_semantics=("arbitrary", "arbitrary"),
                                             vmem_limit_bytes=VMEM_LIMIT),
        name="norm_proj",
    )(x, g.reshape(1, d), w_bf16)


def _merge_kernel(x_ref, ba_ref, bb_ref, bc_ref, bd_ref, gate_ref, wb_ref, wo_ref, o_ref):
    acc = None
    for n, b_ref in enumerate((ba_ref, bb_ref, bc_ref, bd_ref)):
        p = jnp.dot(b_ref[...].astype(jnp.bfloat16), wb_ref[n], preferred_element_type=jnp.float32)
        t = jax.nn.sigmoid(gate_ref[:, n * D_MODEL:(n + 1) * D_MODEL]) * p
        acc = t if acc is None else acc + t
    o_ref[...] = x_ref[...] + jnp.dot(acc.astype(jnp.bfloat16), wo_ref[...], preferred_element_type=jnp.float32)


def _merge_out(x, branches, c_all, gate_col, wb_bf16, wo_bf16):
    m, d = x.shape
    tm = 256 if m % 256 == 0 else m
    row = lambda w: pl.BlockSpec((tm, w), lambda i: (i, 0))
    return pl.pallas_call(
        _merge_kernel,
        grid=(m // tm,),
        in_specs=[row(d)] + [row(BRANCH_W)] * N_BRANCH + [
                  pl.BlockSpec((tm, N_BRANCH * d), lambda i: (i, gate_col // (N_BRANCH * d))),
                  pl.BlockSpec((N_BRANCH, BRANCH_W, d), lambda i: (0, 0, 0)),
                  pl.BlockSpec((d, d), lambda i: (0, 0))],
        out_specs=row(d),
        out_shape=jax.ShapeDtypeStruct((m, d), jnp.float32),
        compiler_params=pltpu.CompilerParams(dimension_semantics=("arbitrary",),
                                             vmem_limit_bytes=VMEM_LIMIT),
        name="merge_out",
    )(x, *branches, c_all, wb_bf16, wo_bf16)


MLP_TF = 1024


def _mlp_kernel(x_ref, g_ref, wu_ref, wd_ref, o_ref, h_ref, acc_ref):
    j = pl.program_id(1)

    @pl.when(j == 0)
    def _():
        x = x_ref[...]
        y = x * lax.rsqrt(jnp.mean(x * x, axis=-1, keepdims=True) + EPS)
        h_ref[...] = (y * g_ref[...]).astype(jnp.bfloat16)
        acc_ref[...] = jnp.zeros_like(acc_ref)

    u = jnp.maximum(jnp.dot(h_ref[...], wu_ref[...], preferred_element_type=jnp.float32), 0.0)
    acc_ref[...] += jnp.dot((u * u).astype(jnp.bfloat16), wd_ref[...], preferred_element_type=jnp.float32)

    @pl.when(j == pl.num_programs(1) - 1)
    def _():
        o_ref[...] = x_ref[...] + acc_ref[...]


def _mlp(x, g, wu_bf16, wd_bf16):
    m, d = x.shape
    f = wu_bf16.shape[1]
    tm = _row_tile(m)
    return pl.pallas_call(
        _mlp_kernel,
        grid=(m // tm, f // MLP_TF),
        in_specs=[pl.BlockSpec((tm, d), lambda i, j: (i, 0)),
                  pl.BlockSpec((1, d), lambda i, j: (0, 0)),
                  pl.BlockSpec((d, MLP_TF), lambda i, j: (0, j)),
                  pl.BlockSpec((MLP_TF, d), lambda i, j: (j, 0))],
        out_specs=pl.BlockSpec((tm, d), lambda i, j: (i, 0)),
        out_shape=jax.ShapeDtypeStruct((m, d), jnp.float32),
        scratch_shapes=[pltpu.VMEM((tm, d), jnp.bfloat16), pltpu.VMEM((tm, d), jnp.float32)],
        compiler_params=pltpu.CompilerParams(dimension_semantics=("arbitrary", "arbitrary"),
                                             vmem_limit_bytes=VMEM_LIMIT),
        name="mlp",
    )(x, g.reshape(1, d), wu_bf16, wd_bf16)


NSA_TQ = 128
NSA_TK = 512
NSA_ROWS = NSA_HEADS * NSA_TQ
NSA_WSPAN = WINDOW + NSA_TQ


def _dot(a, b, **kw):
    return jnp.dot(a, b, preferred_element_type=jnp.float32, **kw)


def _dot_nt(a, b):
    return lax.dot_general(a, b, (((1,), (1,)), ((), ())), preferred_element_type=jnp.float32)


def rope_tables(pos):
    half = ROPE_DIM // 2
    inv_freq = ROPE_THETA ** (-jnp.arange(half, dtype=jnp.float32) / half)
    ang = pos.astype(jnp.float32)[:, None] * inv_freq
    cos, sin = jnp.cos(ang), jnp.sin(ang)
    n = pos.shape[0]
    one = jnp.ones((n, NSA_HD - ROPE_DIM), jnp.float32)
    zero = jnp.zeros((n, NSA_HD - ROPE_DIM), jnp.float32)
    z8 = jnp.zeros((n, half), jnp.float32)
    c = jnp.concatenate([cos, cos, one], axis=1)
    s1 = jnp.concatenate([-sin, z8, zero], axis=1)
    s2 = jnp.concatenate([z8, sin, zero], axis=1)
    two = lambda a: jnp.concatenate([a, a], axis=1)
    return two(c), two(s1), two(s2)


def _rope(x, c, s1, s2):
    n = x.shape[-1]
    return x * c + pltpu.roll(x, n - ROPE_DIM // 2, 1) * s1 + pltpu.roll(x, ROPE_DIM // 2, 1) * s2


def _nsa_prep_kernel(kv0_ref, kv1_ref, kv2_ref, c_ref, s1_ref, s2_ref, rows_ref, win_ref, kvb_ref):
    c, s1, s2 = c_ref[...], s1_ref[...], s2_ref[...]
    cmp_kv = kv0_ref[...]
    sel = kv1_ref[...]
    wnd = kv2_ref[...]
    ks = _rope(sel[:, :LANE], c, s1, s2)
    kw = _rope(wnd[:, :LANE], c, s1, s2)
    rows_ref[:, 0:2 * LANE] = cmp_kv
    rows_ref[:, 2 * LANE:3 * LANE] = ks
    rows_ref[:, 3 * LANE:4 * LANE] = sel[:, LANE:]
    win_ref[:, 0:LANE] = kw
    win_ref[:, LANE:2 * LANE] = wnd[:, LANE:]
    kvb_ref[:, 0:LANE] = ks.astype(MXU_DT)
    kvb_ref[:, LANE:2 * LANE] = sel[:, LANE:].astype(MXU_DT)
    kvb_ref[:, 2 * LANE:3 * LANE] = kw.astype(MXU_DT)
    kvb_ref[:, 3 * LANE:4 * LANE] = wnd[:, LANE:].astype(MXU_DT)


def nsa_prep(c_all, kv_col0, tabs, seq_len):
    m = c_all.shape[0]
    tm = 512 if seq_len % 512 == 0 else seq_len
    nlt = seq_len // tm
    cb = kv_col0 // (2 * LANE)
    kv_spec = lambda k: pl.BlockSpec((tm, 2 * LANE), lambda i, k=k: (i, cb + k))
    tab_spec = pl.BlockSpec((tm, LANE), lambda i: (i % nlt, 0))
    return pl.pallas_call(
        _nsa_prep_kernel,
        grid=(m // tm,),
        in_specs=[kv_spec(0), kv_spec(1), kv_spec(2), tab_spec, tab_spec, tab_spec],
        out_specs=[pl.BlockSpec((tm, 4 * LANE), lambda i: (i, 0)),
                   pl.BlockSpec((tm, 2 * LANE), lambda i: (i, 0)),
                   pl.BlockSpec((tm, 4 * LANE), lambda i: (i, 0))],
        out_shape=[jax.ShapeDtypeStruct((m, 4 * LANE), jnp.float32),
                   jax.ShapeDtypeStruct((m, 2 * LANE), jnp.float32),
                   jax.ShapeDtypeStruct((m, 4 * LANE), MXU_DT)],
        compiler_params=pltpu.CompilerParams(dimension_semantics=("arbitrary",), vmem_limit_bytes=VMEM_LIMIT),
        name="nsa_prep",
    )(c_all, c_all, c_all, *tabs)


def _nsa_compress_kernel(x_ref, pos_ref, w1_ref, w2_ref, o_ref):
    nh = x_ref.shape[0] // CMP_STRIDE
    top = jnp.zeros((nh, NSA_KV_HEADS * CMP_HIDDEN), jnp.float32)
    bot = jnp.zeros((nh, NSA_KV_HEADS * CMP_HIDDEN), jnp.float32)
    for t in range(CMP_STRIDE):
        x = x_ref[pl.ds(t, nh, stride=CMP_STRIDE), :]
        top = top + _dot((x + pos_ref[0, t:t + 1, :]).astype(MXU_DT), w1_ref[0, 0, t])
        bot = bot + _dot((x + pos_ref[0, CMP_STRIDE + t:CMP_STRIDE + t + 1, :]).astype(MXU_DT), w1_ref[0, 1, t])
    h = top + pltpu.roll(bot, nh - 1, 0)
    h = h * jax.nn.sigmoid(h)
    o_ref[0, 0] = _dot(h.astype(MXU_DT), w2_ref[0]).astype(o_ref.dtype)


def _block_diag2(w):
    z = jnp.zeros_like(w)
    return jnp.concatenate([jnp.concatenate([w, z], axis=-1), jnp.concatenate([z, w], axis=-1)], axis=-2)


def nsa_compress(seq_rows, cmp_pos, cmp_w1, cmp_w2):
    b, seq_len = seq_rows.shape[:2]
    nh = seq_len // CMP_STRIDE
    pos = jnp.concatenate([cmp_pos, cmp_pos], axis=-1)
    w1 = cmp_w1.reshape(2, 2, CMP_STRIDE, NSA_HD, CMP_HIDDEN)
    w1 = _block_diag2(w1).astype(MXU_DT)
    w2 = _block_diag2(cmp_w2).astype(MXU_DT)
    return pl.pallas_call(
        _nsa_compress_kernel,
        grid=(b, 2),
        in_specs=[pl.BlockSpec((None, seq_len, LANE), lambda i, j: (i, 0, j)),
                  pl.BlockSpec((1, CMP_BLOCK, LANE), lambda i, j: (j, 0, 0)),
                  pl.BlockSpec((1, 2, CMP_STRIDE, LANE, NSA_KV_HEADS * CMP_HIDDEN), lambda i, j: (j, 0, 0, 0, 0)),
                  pl.BlockSpec((1, NSA_KV_HEADS * CMP_HIDDEN, LANE), lambda i, j: (j, 0, 0))],
        out_specs=pl.BlockSpec((1, 1, nh, LANE), lambda i, j: (i, j, 0, 0)),
        out_shape=jax.ShapeDtypeStruct((b, 2, nh, LANE), MXU_DT),
        compiler_params=pltpu.CompilerParams(dimension_semantics=("arbitrary", "arbitrary"), vmem_limit_bytes=VMEM_LIMIT),
        name="nsa_compress",
    )(seq_rows, pos, w1, w2)


def _pad_heads(q):
    lane = lax.broadcasted_iota(jnp.int32, (q.shape[0], LANE), 1)
    blocks = []
    for h in range(NSA_HEADS):
        blk = q[:, (h // 2) * LANE:(h // 2 + 1) * LANE]
        g = h // NSA_GROUP
        if h % 2 != g:
            blk = pltpu.roll(blk, NSA_HD, 1)
        keep = (lane < NSA_HD) if g == 0 else (lane >= NSA_HD)
        blocks.append(jnp.where(keep, blk, 0.0))
    return jnp.concatenate(blocks, axis=0)


def _softmax_rows(s):
    m = jnp.max(s, axis=-1, keepdims=True)
    m = jnp.where(m == NEG_INF, 0.0, m)
    p = jnp.exp(s - m)
    return p / jnp.maximum(jnp.sum(p, axis=-1, keepdims=True), 1e-30)


def _nsa_attn_kernel(q_ref, gate_ref, c_ref, s1_ref, s2_ref, ck_ref, cv_ref, ks_ref, vs_ref, kn_ref, kw_ref, vw_ref,
                     o_ref, m_ref, l_ref, acc_ref, s_ref, *, q_start, win_start, ns, n_new):
    tq = q_ref.shape[1]
    n_rows = NSA_HEADS * tq
    n_keys = ks_ref.shape[1]
    i = pl.program_id(1)
    q0 = q_start + i * tq
    scale = NSA_HD ** -0.5
    q = q_ref[0] * scale
    c = jnp.concatenate([c_ref[...]] * 4, axis=1)
    s1 = jnp.concatenate([s1_ref[...]] * 4, axis=1)
    s2 = jnp.concatenate([s2_ref[...]] * 4, axis=1)
    q_raw = _pad_heads(q).astype(MXU_DT)
    q_rot = _pad_heads(_rope(q, c, s1, s2)).astype(MXU_DT)

    row = lax.broadcasted_iota(jnp.int32, (n_rows, 1), 0)
    qpos = q0 + (row & (tq - 1))

    nc = ck_ref.shape[1]
    s_c = _dot_nt(q_raw, ck_ref[0])
    c_end = lax.broadcasted_iota(jnp.int32, (1, nc), 1) * CMP_STRIDE + (CMP_BLOCK - 1)
    p_c = _softmax_rows(jnp.where(c_end <= qpos, s_c, NEG_INF))
    o_c = _dot(p_c.astype(MXU_DT), cv_ref[0])

    psum = jnp.concatenate(
        [sum(p_c[(g * NSA_GROUP + j) * tq:(g * NSA_GROUP + j + 1) * tq] for j in range(NSA_GROUP))
         for g in range(NSA_KV_HEADS)], axis=0)
    n_i = lax.broadcasted_iota(jnp.int32, (nc, ns), 0) * CMP_STRIDE
    m_i = lax.broadcasted_iota(jnp.int32, (nc, ns), 1) * SEL_BLOCK
    overlap = ((n_i <= m_i + (SEL_BLOCK - 1)) & (n_i + (CMP_BLOCK - 1) >= m_i)).astype(jnp.float32)
    imp = _dot(psum, overlap, precision=lax.Precision.HIGHEST)
    r2 = lax.broadcasted_iota(jnp.int32, (NSA_KV_HEADS * tq, 1), 0)
    qpos2 = q0 + (r2 & (tq - 1))
    cur = qpos2 >> 6
    blk = lax.broadcasted_iota(jnp.int32, (1, ns), 1)
    forced = (blk == 0) | (blk == cur) | (blk == cur - 1)
    valid = blk * SEL_BLOCK <= qpos2
    v = jnp.where(forced, jnp.inf, jnp.where(valid, imp, NEG_INF))
    blk_f = blk.astype(jnp.float32)
    sel = jnp.zeros(v.shape, jnp.float32)
    for _ in range(SEL_TOPN):
        mx = jnp.max(v, axis=-1, keepdims=True)
        first = jnp.min(jnp.where(v == mx, blk_f, float(ns)), axis=-1, keepdims=True)
        pick = blk_f == first
        sel = jnp.where(pick, 1.0, sel)
        v = jnp.where(pick, NEG_INF, v)
    sel_b = sel.astype(MXU_DT)

    m_ref[...] = jnp.full(m_ref.shape, NEG_INF, jnp.float32)
    l_ref[...] = jnp.zeros(l_ref.shape, jnp.float32)
    acc_ref[...] = jnp.zeros(acc_ref.shape, jnp.float32)
    n_kt = jnp.minimum((q0 + tq + NSA_TK - 1) // NSA_TK, n_keys // NSA_TK)

    def scores(k):
        return _dot_nt(q_rot, k.astype(MXU_DT))

    def update(s, vv, tok0):
        nk = s.shape[1]
        tok = tok0 + lax.broadcasted_iota(jnp.int32, (1, nk), 1)
        e_m = lax.broadcasted_iota(jnp.int32, (ns, nk), 0)
        e_t = (tok0 + lax.broadcasted_iota(jnp.int32, (ns, nk), 1)) >> 6
        chosen = _dot(sel_b, (e_m == e_t).astype(MXU_DT))
        bias2 = jnp.where((chosen > 0.5) & (tok <= qpos2), 0.0, NEG_INF)
        bias = jnp.concatenate([bias2[0:tq]] * NSA_GROUP + [bias2[tq:2 * tq]] * NSA_GROUP, axis=0)
        s = s + bias
        m_old = m_ref[...]
        m_new = jnp.maximum(m_old, jnp.max(s, axis=-1, keepdims=True))
        m_safe = jnp.where(m_new == NEG_INF, 0.0, m_new)
        alpha = jnp.exp(m_old - m_safe)
        p = jnp.exp(s - m_safe)
        l_ref[...] = alpha * l_ref[...] + jnp.sum(p, axis=-1, keepdims=True)
        acc_ref[...] = alpha * acc_ref[...] + _dot(p.astype(MXU_DT), vv.astype(MXU_DT))
        m_ref[...] = m_new

    def body(kt, carry):
        koff = pl.multiple_of(kt * NSA_TK, NSA_TK)
        knext = pl.multiple_of(jnp.minimum(kt + 1, n_kt - 1) * NSA_TK, NSA_TK)
        s = s_ref[...]
        s_next = scores(ks_ref[0, pl.ds(knext, NSA_TK), :])
        update(s, vs_ref[0, pl.ds(koff, NSA_TK), :], koff)
        s_ref[...] = s_next
        return carry

    s_ref[...] = scores(ks_ref[0, 0:NSA_TK, :])
    lax.fori_loop(0, n_kt, body, 0)
    if n_new:
        update(scores(kn_ref[0, :, 0:LANE]), kn_ref[0, :, LANE:2 * LANE], n_keys)
    o_s = acc_ref[...] / jnp.maximum(l_ref[...], 1e-30)

    w0 = jnp.clip(q0 - win_start - WINDOW, 0, kw_ref.shape[1] - NSA_WSPAN)
    w0 = pl.multiple_of(w0, tq)
    kw = kw_ref[0, pl.ds(w0, NSA_WSPAN), :]
    vw = vw_ref[0, pl.ds(w0, NSA_WSPAN), :]
    s_w = _dot_nt(q_rot, kw.astype(MXU_DT))
    kpos = win_start + w0 + lax.broadcasted_iota(jnp.int32, (1, NSA_WSPAN), 1)
    w_ok = (kpos <= qpos) & (kpos > qpos - WINDOW)
    p_w = _softmax_rows(jnp.where(w_ok, s_w, NEG_INF))
    o_w = _dot(p_w.astype(MXU_DT), vw.astype(MXU_DT))

    gates = jax.nn.sigmoid(gate_ref[0])
    lane = lax.broadcasted_iota(jnp.int32, (tq, LANE), 1)
    outs = []
    for h in range(NSA_HEADS):
        sl = slice(h * tq, (h + 1) * tq)
        k0 = 2 * DN_HEADS + 3 * h
        o_h = (gates[:, k0:k0 + 1] * o_c[sl] + gates[:, k0 + 1:k0 + 2] * o_s[sl] + gates[:, k0 + 2:k0 + 3] * o_w[sl])
        if h % 2 != h // NSA_GROUP:
            o_h = pltpu.roll(o_h, NSA_HD, 1)
        outs.append(o_h)
    for p2 in range(NSA_HEADS // 2):
        o_ref[0, :, p2 * LANE:(p2 + 1) * LANE] = jnp.where(lane < NSA_HD, outs[2 * p2], outs[2 * p2 + 1])


def nsa_attend(c_all, q_col0, misc_col0, tabs, ckv, sel_kv, sel_cols, new_kv, n_new, win_kv, win_cols, win_start,
               batch, seq_len, q_start):
    tq = NSA_TQ if seq_len % NSA_TQ == 0 else seq_len
    n_rows = NSA_HEADS * tq
    qb = q_col0 // (4 * LANE)
    mb = misc_col0 // LANE
    c3 = c_all.reshape(batch, seq_len, c_all.shape[1])
    n_blocks = -(-(sel_kv.shape[1] + n_new) // SEL_BLOCK)
    ns = -(-n_blocks // LANE) * LANE
    tab_spec = pl.BlockSpec((tq, LANE), lambda b, i: (i, 0))
    col_spec = lambda a, k: pl.BlockSpec((1, a.shape[1], LANE), lambda b, i, k=k: (b, 0, k))
    nc = ckv.shape[2]
    out = pl.pallas_call(
        functools.partial(_nsa_attn_kernel, q_start=q_start, win_start=win_start, ns=ns, n_new=n_new),
        grid=(batch, seq_len // tq),
        in_specs=[pl.BlockSpec((1, tq, 4 * LANE), lambda b, i: (b, i, qb)),
                  pl.BlockSpec((1, tq, LANE), lambda b, i: (b, i, mb)),
                  tab_spec, tab_spec, tab_spec,
                  pl.BlockSpec((None, 1, nc, LANE), lambda b, i: (b, 0, 0, 0)),
                  pl.BlockSpec((None, 1, nc, LANE), lambda b, i: (b, 1, 0, 0)),
                  col_spec(sel_kv, sel_cols[0]), col_spec(sel_kv, sel_cols[1]),
                  pl.BlockSpec((1, new_kv.shape[1], 2 * LANE), lambda b, i: (b, 0, 0)),
                  col_spec(win_kv, win_cols[0]), col_spec(win_kv, win_cols[1])],
        out_specs=pl.BlockSpec((1, tq, 4 * LANE), lambda b, i: (b, i, 0)),
        out_shape=jax.ShapeDtypeStruct((batch, seq_len, 4 * LANE), jnp.float32),
        scratch_shapes=[pltpu.VMEM((n_rows, 1), jnp.float32), pltpu.VMEM((n_rows, 1), jnp.float32),
                        pltpu.VMEM((n_rows, LANE), jnp.float32), pltpu.VMEM((n_rows, NSA_TK), jnp.float32)],
        compiler_params=pltpu.CompilerParams(dimension_semantics=("arbitrary", "arbitrary"), vmem_limit_bytes=VMEM_LIMIT),
        name="nsa_attn",
    )(c3, c3, *tabs, ckv, ckv, sel_kv, sel_kv, new_kv, win_kv, win_kv)
    return out.reshape(batch * seq_len, 4 * LANE)


SCAN_C = 64
SCAN_TT = 512
HALO = 8


def _dot_tn(a, b):
    return lax.dot_general(a, b, (((0,), (0,)), ((), ())), preferred_element_type=jnp.float32)


def _mx(x):
    return x.astype(MXU_DT)


def _dot3(a, b):
    a_hi, b_hi = _mx(a), _mx(b)
    a_lo = _mx(a - a_hi.astype(jnp.float32))
    b_lo = _mx(b - b_hi.astype(jnp.float32))
    return _dot(a_hi, b_hi) + (_dot(a_hi, b_lo) + _dot(a_lo, b_hi))


def _chunk_cumsum(x):
    c = x.shape[0]
    row = lax.broadcasted_iota(jnp.int32, x.shape, 0)
    d = 1
    while d < c:
        x = x + jnp.where(row >= d, pltpu.roll(x, d, 0), 0.0)
        d *= 2
    return x


def _silu(x):
    return x * jax.nn.sigmoid(x)


def _conv_tile(x, halo, w_ref):
    kw = w_ref.shape[0]
    r8 = lax.broadcasted_iota(jnp.int32, (HALO, x.shape[1]), 0)
    y = x * w_ref[kw - 1:kw, :]
    for d in range(1, kw):
        xs = pltpu.roll(x, d, 0)
        head = jnp.where(r8 < d, pltpu.roll(halo, d, 0), xs[0:HALO])
        xs = jnp.concatenate([head, xs[HALO:]], axis=0)
        y = y + xs * w_ref[kw - 1 - d:kw - d, :]
    return y


def _halo_spec(tt, width, col_block):
    return pl.BlockSpec((HALO, width), lambda i: (jnp.maximum(i * (tt // HALO) - 1, 0), col_block))


def _pad_buf(buf):
    return jnp.pad(buf.astype(jnp.float32), ((0, 0), (HALO - buf.shape[1], 0), (0, 0)))


def _hgrn_prep_kernel(q_ref, f_ref, v_ref, la_ref, l1_ref, lbc_ref, oi_ref, qb_ref, kb_ref, eb_ref):
    c = SCAN_C
    tt = q_ref.shape[0]
    row = lax.broadcasted_iota(jnp.int32, (c, LANE), 0)
    ti = lax.broadcasted_iota(jnp.int32, (c, c), 0)
    si = lax.broadcasted_iota(jnp.int32, (c, c), 1)
    for h in range(HG_HEADS):
        ls = slice(h * LANE, (h + 1) * LANE)
        la, l1, lbc = la_ref[:, ls], l1_ref[:, ls], lbc_ref[:, ls]
        for ci in range(tt // c):
            rs = slice(ci * c, (ci + 1) * c)
            q = _silu(q_ref[rs, ls])
            z = f_ref[rs, ls]
            v = v_ref[rs, ls]
            lsig = jnp.minimum(z, 0.0) - jnp.log1p(jnp.exp(-jnp.abs(z)))
            t2 = l1 + lsig
            hi = jnp.maximum(la, t2)
            logf = hi + jnp.log1p(jnp.exp(-jnp.abs(la - t2)))
            k = lbc * jax.nn.sigmoid(-z)
            b = _chunk_cumsum(logf)
            b_last = b[c - 1:c, :]
            qb_ref[rs, ls] = q * jnp.exp(b)
            kb_ref[rs, ls] = k * jnp.exp(b_last - b)
            eb_ref[ci:ci + 1, ls] = jnp.exp(b_last)
            att = jnp.zeros((c, c), jnp.float32)
            n = c // 2
            while n >= 8:
                blk = 2 * n
                ref_rows = jnp.concatenate(
                    [jnp.broadcast_to(b[j * blk + n - 1:j * blk + n, :], (blk, LANE)) for j in range(c // blk)], axis=0)
                upper = (row & (blk - 1)) >= n
                qs = q * jnp.exp(jnp.where(upper, b - ref_rows, 0.0))
                ks = k * jnp.exp(jnp.where(upper, 0.0, ref_rows - b))
                lvl = _dot_nt(_mx(qs), _mx(ks))
                ok = ((ti & ~(blk - 1)) == (si & ~(blk - 1))) & ((ti & (blk - 1)) >= n) & ((si & (blk - 1)) < n)
                att = att + jnp.where(ok, lvl, 0.0)
                n //= 2
            o = _dot(_mx(att), _mx(v))
            for d in range(8):
                if d == 0:
                    w = jnp.sum(q * k, axis=-1, keepdims=True)
                    o = o + w * v
                else:
                    e = jnp.exp(jnp.minimum(b - pltpu.roll(b, d, 0), 0.0))
                    w = jnp.sum(q * pltpu.roll(k, d, 0) * e, axis=-1, keepdims=True)
                    w = jnp.where((row[:, 0:1] & 7) >= d, w, 0.0)
                    o = o + w * pltpu.roll(v, d, 0)
            oi_ref[rs, ls] = o


def hgrn_prep(c_all, col_q, col_f, col_v, lb):
    m = c_all.shape[0]
    tt = SCAN_TT
    lbf = lb.reshape(1, BRANCH_W).astype(jnp.float32)
    la, l1, lbc = jnp.log(lbf), jnp.log1p(-lbf), 1.0 - lbf
    blk = lambda col: pl.BlockSpec((tt, BRANCH_W), lambda i, col=col: (i, col // BRANCH_W))
    vec = pl.BlockSpec((1, BRANCH_W), lambda i: (0, 0))
    out = pl.BlockSpec((tt, BRANCH_W), lambda i: (i, 0))
    return pl.pallas_call(
        _hgrn_prep_kernel,
        grid=(m // tt,),
        in_specs=[blk(col_q), blk(col_f), blk(col_v), vec, vec, vec],
        out_specs=[out, out, out, pl.BlockSpec((tt // SCAN_C, BRANCH_W), lambda i: (i, 0))],
        out_shape=[jax.ShapeDtypeStruct((m, BRANCH_W), jnp.float32)] * 3
                  + [jax.ShapeDtypeStruct((m // SCAN_C, BRANCH_W), jnp.float32)],
        compiler_params=pltpu.CompilerParams(dimension_semantics=("arbitrary",), vmem_limit_bytes=VMEM_LIMIT),
        name="hgrn_prep",
    )(c_all, c_all, c_all, la, l1, lbc)


def _hgrn_scan_kernel(oi_ref, qb_ref, kb_ref, eb_ref, v_ref, g_ref, nw_ref, s0_ref, o_ref, sT_out_ref, sT_ref):
    c = SCAN_C
    nb = oi_ref.shape[0]
    tt = oi_ref.shape[1]
    i = pl.program_id(0)

    @pl.when(i == 0)
    def _():
        sT_ref[...] = s0_ref[...]

    nw = nw_ref[...]
    for ci in range(tt // c):
        rs = slice(ci * c, (ci + 1) * c)
        for b in range(nb):
            for h in range(HG_HEADS):
                ls = slice(h * LANE, (h + 1) * LANE)
                sT = sT_ref[b, h]
                o = oi_ref[b, rs, ls] + _dot_nt(_mx(qb_ref[b, rs, ls]), _mx(sT))
                sT_ref[b, h] = eb_ref[b, ci:ci + 1, ls] * sT + _dot_tn(_mx(v_ref[b, rs, ls]), _mx(kb_ref[b, rs, ls]))
                y = o * lax.rsqrt(jnp.mean(o * o, axis=-1, keepdims=True) + EPS) * nw
                o_ref[b, rs, ls] = y * _silu(g_ref[b, rs, ls])

    @pl.when(i == pl.num_programs(0) - 1)
    def _():
        sT_out_ref[...] = sT_ref[...]


def hgrn_scan(oi, qb, kb, eb, c_all, col_v, col_g, norm_w, s0, batch, seq_len):
    tt = SCAN_TT
    n3 = lambda a: a.reshape(batch, seq_len, a.shape[-1])
    c3 = n3(c_all)
    tok = pl.BlockSpec((batch, tt, BRANCH_W), lambda i: (0, i, 0))
    ctok = lambda col: pl.BlockSpec((batch, tt, BRANCH_W), lambda i, col=col: (0, i, col // BRANCH_W))
    st = pl.BlockSpec((batch, HG_HEADS, LANE, LANE), lambda i: (0, 0, 0, 0))
    o, sT = pl.pallas_call(
        _hgrn_scan_kernel,
        grid=(seq_len // tt,),
        in_specs=[tok, tok, tok, pl.BlockSpec((batch, tt // SCAN_C, BRANCH_W), lambda i: (0, i, 0)),
                  ctok(col_v), ctok(col_g), pl.BlockSpec((1, LANE), lambda i: (0, 0)), st],
        out_specs=[tok, st],
        out_shape=[jax.ShapeDtypeStruct((batch, seq_len, BRANCH_W), jnp.float32),
                   jax.ShapeDtypeStruct((batch, HG_HEADS, LANE, LANE), jnp.float32)],
        scratch_shapes=[pltpu.VMEM((batch, HG_HEADS, LANE, LANE), jnp.float32)],
        compiler_params=pltpu.CompilerParams(dimension_semantics=("arbitrary",), vmem_limit_bytes=VMEM_LIMIT),
        name="hgrn_scan",
    )(n3(oi), n3(qb), n3(kb), eb.reshape(batch, seq_len // SCAN_C, BRANCH_W), c3, c3,
      norm_w.reshape(1, LANE), jnp.swapaxes(s0, -1, -2))
    return o.reshape(batch * seq_len, BRANCH_W), jnp.swapaxes(sT, -1, -2)


def _gdn_prep_kernel(x_ref, halo_ref, buf_ref, misc_ref, cw_ref, alog_ref, dtb_ref,
                     u_ref, w_ref, qg_ref, kg_ref, qk_ref, eg_ref, *, tiles_per_seq):
    c = SCAN_C
    tt = x_ref.shape[0]
    first = (pl.program_id(0) % tiles_per_seq) == 0
    halo = jnp.where(first, buf_ref[0], halo_ref[...])
    y = _silu(_conv_tile(x_ref[...], halo, cw_ref))
    misc = misc_ref[...]
    beta_all = jax.nn.sigmoid(misc)
    sp_in = misc + dtb_ref[...]
    sp = jnp.maximum(sp_in, 0.0) + jnp.log1p(jnp.exp(-jnp.abs(sp_in)))
    g_all = -jnp.exp(alog_ref[...]) * sp
    ti = lax.broadcasted_iota(jnp.int32, (c, c), 0)
    si = lax.broadcasted_iota(jnp.int32, (c, c), 1)
    eye = (ti == si).astype(jnp.float32)
    for ci in range(tt // c):
        rs = slice(ci * c, (ci + 1) * c)
        gam_all = _chunk_cumsum(g_all[rs])
        gam_t = gam_all.T
        heads = []
        for h in range(DN_HEADS):
            q = y[rs, h * LANE:(h + 1) * LANE]
            k = y[rs, BRANCH_W + h * LANE:BRANCH_W + (h + 1) * LANE]
            v = y[rs, 2 * BRANCH_W + h * LANE:2 * BRANCH_W + (h + 1) * LANE]
            q = q * lax.rsqrt(jnp.sum(q * q, axis=-1, keepdims=True) + EPS) * (DN_DK ** -0.5)
            k = k * lax.rsqrt(jnp.sum(k * k, axis=-1, keepdims=True) + EPS)
            beta = beta_all[rs, h:h + 1]
            gam = gam_all[:, DN_HEADS + h:DN_HEADS + h + 1]
            gam_r = gam_t[DN_HEADS + h:DN_HEADS + h + 1, :]
            decay = jnp.exp(jnp.where(si <= ti, gam - gam_r, NEG_INF))
            kb16 = _mx(k)
            kk = _dot_nt(kb16, kb16)
            a = jnp.where(si < ti, beta * kk * decay, 0.0)
            heads.append((q, k, v, beta, gam, decay, kb16, a))
        tinvs = [eye] * DN_HEADS
        s = 1
        while s < c:
            blk = 2 * s
            off = (((ti & ~(blk - 1)) == (si & ~(blk - 1))) & ((ti & (blk - 1)) >= s) & ((si & (blk - 1)) < s))
            a_offs = [jnp.where(off, hd[7], 0.0) for hd in heads]
            if s == 1:
                tinvs = [t - ao for t, ao in zip(tinvs, a_offs)]
            else:
                mids = [_dot3(t, ao) for t, ao in zip(tinvs, a_offs)]
                tinvs = [t - _dot3(md, t) for t, md in zip(tinvs, mids)]
            s = blk
        for h, (q, k, v, beta, gam, decay, kb16, a) in enumerate(heads):
            ls = slice(h * LANE, (h + 1) * LANE)
            e_gam = jnp.exp(gam)
            rhs = jnp.concatenate([v * beta, k * (beta * e_gam)], axis=1)
            sol = rhs + _dot3(tinvs[h] - eye, rhs)
            u_ref[rs, ls] = sol[:, :LANE]
            w_ref[rs, ls] = sol[:, LANE:]
            qg_ref[rs, ls] = q * e_gam
            g_last = gam[c - 1:c, :]
            kg_ref[rs, ls] = k * jnp.exp(g_last - gam)
            qk_ref[rs, h * c:(h + 1) * c] = _dot_nt(_mx(q), kb16) * decay
            eg_ref[ci:ci + 1, ls] = jnp.broadcast_to(jnp.exp(g_last), (1, LANE))


def gdn_prep(c_all, col_x, col_misc, conv_buf, conv_w, a_log, dt_bias, batch, seq_len):
    m = c_all.shape[0]
    tt = SCAN_TT
    tps = seq_len // tt
    xw = 3 * BRANCH_W
    lanes = jnp.zeros((1, LANE), jnp.float32)
    alog = lanes.at[0, DN_HEADS:2 * DN_HEADS].set(a_log.astype(jnp.float32))
    dtb = lanes.at[0, DN_HEADS:2 * DN_HEADS].set(dt_bias.astype(jnp.float32))
    out = pl.BlockSpec((tt, BRANCH_W), lambda i: (i, 0))
    vec = pl.BlockSpec((1, LANE), lambda i: (0, 0))
    return pl.pallas_call(
        functools.partial(_gdn_prep_kernel, tiles_per_seq=tps),
        grid=(m // tt,),
        in_specs=[pl.BlockSpec((tt, xw), lambda i: (i, col_x // xw)),
                  _halo_spec(tt, xw, col_x // xw),
                  pl.BlockSpec((1, HALO, xw), lambda i: (i // tps, 0, 0)),
                  pl.BlockSpec((tt, LANE), lambda i: (i, col_misc // LANE)),
                  pl.BlockSpec((DN_CONV, xw), lambda i: (0, 0)), vec, vec],
        out_specs=[out, out, out, out, pl.BlockSpec((tt, DN_HEADS * SCAN_C), lambda i: (i, 0)),
                   pl.BlockSpec((tt // SCAN_C, BRANCH_W), lambda i: (i, 0))],
        out_shape=[jax.ShapeDtypeStruct((m, BRANCH_W), jnp.float32)] * 4
                  + [jax.ShapeDtypeStruct((m, DN_HEADS * SCAN_C), jnp.float32),
                     jax.ShapeDtypeStruct((m // SCAN_C, BRANCH_W), jnp.float32)],
        compiler_params=pltpu.CompilerParams(dimension_semantics=("arbitrary",), vmem_limit_bytes=VMEM_LIMIT),
        name="gdn_prep",
    )(c_all, c_all, _pad_buf(conv_buf), c_all, conv_w.astype(jnp.float32), alog, dtb)


def _gdn_scan_kernel(u_ref, w_ref, qg_ref, kg_ref, qk_ref, eg_ref, z_ref, nw_ref, s0_ref, o_ref, s_out_ref, s_ref):
    c = SCAN_C
    nb = u_ref.shape[0]
    tt = u_ref.shape[1]
    i = pl.program_id(0)

    @pl.when(i == 0)
    def _():
        s_ref[...] = s0_ref[...]

    nw = nw_ref[...]
    for ci in range(tt // c):
        rs = slice(ci * c, (ci + 1) * c)
        for b in range(nb):
            for h in range(DN_HEADS):
                ls = slice(h * LANE, (h + 1) * LANE)
                s = s_ref[b, h]
                s16 = _mx(s)
                both = _dot(_mx(jnp.concatenate([qg_ref[b, rs, ls], w_ref[b, rs, ls]], axis=0)), s16)
                v_new = u_ref[b, rs, ls] - both[c:]
                v16 = _mx(v_new)
                o = both[:c] + _dot(_mx(qk_ref[b, rs, h * c:(h + 1) * c]), v16)
                s_ref[b, h] = eg_ref[b, ci:ci + 1, ls][:, 0:1] * s + _dot_tn(_mx(kg_ref[b, rs, ls]), v16)
                y = o * lax.rsqrt(jnp.mean(o * o, axis=-1, keepdims=True) + EPS) * nw
                o_ref[b, rs, ls] = y * _silu(z_ref[b, rs, ls])

    @pl.when(i == pl.num_programs(0) - 1)
    def _():
        s_out_ref[...] = s_ref[...]


def gdn_scan(u, w, qg, kg, qk, eg, c_all, col_z, norm_w, s0, batch, seq_len):
    tt = SCAN_TT
    n3 = lambda a: a.reshape(batch, seq_len, a.shape[-1])
    tok = pl.BlockSpec((batch, tt, BRANCH_W), lambda i: (0, i, 0))
    st = pl.BlockSpec((batch, DN_HEADS, LANE, LANE), lambda i: (0, 0, 0, 0))
    o, s = pl.pallas_call(
        _gdn_scan_kernel,
        grid=(seq_len // tt,),
        in_specs=[tok, tok, tok, tok, pl.BlockSpec((batch, tt, DN_HEADS * SCAN_C), lambda i: (0, i, 0)),
                  pl.BlockSpec((batch, tt // SCAN_C, BRANCH_W), lambda i: (0, i, 0)),
                  pl.BlockSpec((batch, tt, BRANCH_W), lambda i: (0, i, col_z // BRANCH_W)),
                  pl.BlockSpec((1, LANE), lambda i: (0, 0)), st],
        out_specs=[tok, st],
        out_shape=[jax.ShapeDtypeStruct((batch, seq_len, BRANCH_W), jnp.float32),
                   jax.ShapeDtypeStruct((batch, DN_HEADS, LANE, LANE), jnp.float32)],
        scratch_shapes=[pltpu.VMEM((batch, DN_HEADS, LANE, LANE), jnp.float32)],
        compiler_params=pltpu.CompilerParams(dimension_semantics=("arbitrary",), vmem_limit_bytes=VMEM_LIMIT),
        name="gdn_scan",
    )(n3(u), n3(w), n3(qg), n3(kg), n3(qk), eg.reshape(batch, seq_len // SCAN_C, BRANCH_W), n3(c_all),
      norm_w.reshape(1, LANE), s0.astype(jnp.float32))
    return o.reshape(batch * seq_len, BRANCH_W), s


def _sc_branch_kernel(x_ref, halo_ref, buf_ref, w_ref, o_ref, *, tiles_per_seq):
    first = (pl.program_id(0) % tiles_per_seq) == 0
    x = x_ref[...]
    hr = halo_ref[...]
    halo = jnp.where(first, buf_ref[0], hr[:, SC_W:2 * SC_W] * hr[:, 2 * SC_W:3 * SC_W])
    o_ref[...] = x[:, 0:SC_W] * _conv_tile(x[:, SC_W:2 * SC_W] * x[:, 2 * SC_W:3 * SC_W], halo, w_ref)


def sc_branch(c_all, col_x, conv_buf, conv_w, seq_len):
    m = c_all.shape[0]
    tt = SCAN_TT
    tps = seq_len // tt
    xw = 3 * SC_W
    return pl.pallas_call(
        functools.partial(_sc_branch_kernel, tiles_per_seq=tps),
        grid=(m // tt,),
        in_specs=[pl.BlockSpec((tt, xw), lambda i: (i, col_x // xw)),
                  _halo_spec(tt, xw, col_x // xw),
                  pl.BlockSpec((1, HALO, SC_W), lambda i: (i // tps, 0, 0)),
                  pl.BlockSpec((SC_CONV, SC_W), lambda i: (0, 0))],
        out_specs=pl.BlockSpec((tt, SC_W), lambda i: (i, 0)),
        out_shape=jax.ShapeDtypeStruct((m, SC_W), jnp.float32),
        compiler_params=pltpu.CompilerParams(dimension_semantics=("arbitrary",), vmem_limit_bytes=VMEM_LIMIT),
        name="sc_branch",
    )(c_all, c_all, _pad_buf(conv_buf), conv_w.astype(jnp.float32))


def rms_norm(x, g):
    xf = x.astype(jnp.float32)
    y = xf * lax.rsqrt(jnp.mean(xf * xf, axis=-1, keepdims=True) + EPS)
    return (y * g.astype(jnp.float32)).astype(x.dtype)


def l2_normalize(x):
    xf = x.astype(jnp.float32)
    return xf * lax.rsqrt(jnp.sum(xf * xf, axis=-1, keepdims=True) + EPS)


def causal_conv(x, buf, w):
    k_w = w.shape[0]
    seq_len = x.shape[1]
    xp = jnp.concatenate([buf.astype(x.dtype), x], axis=1)
    y = sum(xp[:, j:j + seq_len] * w[j] for j in range(k_w))
    return y, xp[:, seq_len:]


def to_chunks(a, c):
    b, seq_len = a.shape[:2]
    n = -(-seq_len // c)
    a = jnp.pad(a, [(0, 0), (0, n * c - seq_len)] + [(0, 0)] * (a.ndim - 2))
    return jnp.moveaxis(a.reshape((b, n, c) + a.shape[2:]), 1, 0)


def from_chunks(a, seq_len):
    n, b, c = a.shape[:3]
    return jnp.moveaxis(a, 0, 1).reshape((b, n * c) + a.shape[3:])[:, :seq_len]


def hgrn2_scan(q, k, v, logf, s0):
    seq_len = q.shape[1]
    c = min(HG_CHUNK, seq_len)
    tri = jnp.tril(jnp.ones((c, c), bool))[None, :, :, None, None]

    def step(s, inp):
        qc, kc, vc, gc = inp
        b = jnp.cumsum(gc, axis=1)
        decay = jnp.exp(jnp.where(tri, b[:, :, None] - b[:, None], -jnp.inf))
        att = jnp.einsum('bthk,btshk->btsh', qc, decay * kc[:, None])
        o = jnp.einsum('btsh,bshv->bthv', att, vc) + jnp.einsum('bthk,bhkv->bthv', qc * jnp.exp(b), s)
        b_last = b[:, -1]
        s = jnp.exp(b_last)[..., None] * s + jnp.einsum('bshk,bshv->bhkv', kc * jnp.exp(b_last[:, None] - b), vc)
        return s, o

    xs = tuple(to_chunks(a.astype(jnp.float32), c) for a in (q, k, v, logf))
    s, o = lax.scan(step, s0.astype(jnp.float32), xs)
    return from_chunks(o, seq_len), s


def gated_delta_scan(q, k, v, beta, g, s0):
    seq_len = q.shape[1]
    c = min(DN_CHUNK, seq_len)
    incl = jnp.tril(jnp.ones((c, c), bool))
    strict = jnp.tril(jnp.ones((c, c), bool), -1)
    eye = jnp.eye(c, dtype=jnp.float32)

    def step(s, inp):
        qc, kc, vc, bc, gc = inp
        qh, kh, vh = (jnp.swapaxes(a, 1, 2) for a in (qc, kc, vc))
        bh = jnp.swapaxes(bc, 1, 2)
        gam = jnp.cumsum(jnp.swapaxes(gc, 1, 2), axis=-1)
        decay = jnp.exp(jnp.where(incl, gam[..., :, None] - gam[..., None, :], -jnp.inf))
        kk = jnp.einsum('bhtk,bhsk->bhts', kh, kh)
        t_mat = eye + jnp.where(strict, bh[..., :, None] * kk * decay, 0.0)
        u = lax.linalg.triangular_solve(t_mat, vh * bh[..., None], left_side=True, lower=True)
        w = lax.linalg.triangular_solve(t_mat, kh * (bh * jnp.exp(gam))[..., None], left_side=True, lower=True)
        v_new = u - jnp.einsum('bhtk,bhkv->bhtv', w, s)
        qk = jnp.einsum('bhtk,bhsk->bhts', qh, kh) * decay
        o = jnp.einsum('bhtk,bhkv->bhtv', qh * jnp.exp(gam)[..., None], s) + jnp.einsum('bhts,bhsv->bhtv', qk, v_new)
        g_last = gam[..., -1]
        s = jnp.exp(g_last)[..., None, None] * s + jnp.einsum('bhtk,bhtv->bhkv', kh * jnp.exp(g_last[..., None] - gam)[..., None], v_new)
        return s, jnp.swapaxes(o, 1, 2)

    xs = tuple(to_chunks(a.astype(jnp.float32), c) for a in (q, k, v, beta, g))
    s, o = lax.scan(step, s0.astype(jnp.float32), xs)
    return from_chunks(o, seq_len), s


def trunk_layer(x, start, hg_s, dn_s, dn_buf, sc_buf, past_rows, win_buf, w):
    b, seq_len, _ = x.shape
    f32 = jnp.float32
    x2 = x.reshape(b * seq_len, D_MODEL)
    m = b * seq_len
    c2 = _norm_proj(x2, w["norm_mix"], w["w_in"])
    c_all = c2.reshape(b, seq_len, N_IN_PAD)
    lb = w["lb"]
    if past_rows is None:
        oi, qb, kb, eb = hgrn_prep(c2, C_OFF["hg_q"], C_OFF["hg_f"], C_OFF["hg_i"], lb)
        o_a, hg_s = hgrn_scan(oi, qb, kb, eb, c2, C_OFF["hg_i"], C_OFF["hg_g"], w["hg_norm"], hg_s, b, seq_len)
        u, wy, qg, kg, qk, eg = gdn_prep(c2, C_OFF["dn_qkv"], C_OFF["misc"], dn_buf, w["dn_conv"], w["dn_a_log"],
                                         w["dn_dt_bias"], b, seq_len)
        o_b, dn_s = gdn_scan(u, wy, qg, kg, qk, eg, c2, C_OFF["dn_z"], w["dn_norm"], dn_s, b, seq_len)
        o_c = sc_branch(c2, C_OFF["sc_bch"], sc_buf, w["sc_conv"], seq_len)
        tail = c_all[:, seq_len - (DN_CONV - 1):]
        dn_buf = tail[..., C_OFF["dn_qkv"]:C_OFF["dn_qkv"] + 3 * BRANCH_W]
        sc_t = tail[:, DN_CONV - SC_CONV:, C_OFF["sc_bch"] + SC_W:C_OFF["sc_bch"] + 3 * SC_W]
        sc_buf = sc_t[..., :SC_W] * sc_t[..., SC_W:]
    else:
        o_a, o_b, o_c, (hg_s, dn_s, dn_buf, sc_buf) = cached_mixers(x, c_all, hg_s, dn_s, dn_buf, sc_buf, w)
    tabs = rope_tables(start + jnp.arange(seq_len))
    rows, win_rows, kvb = nsa_prep(c2, C_OFF["nsa_kv"], tabs, seq_len)
    rows = rows.reshape(b, seq_len, 4 * LANE)
    win_rows = win_rows.reshape(b, seq_len, 2 * LANE)
    kvb = kvb.reshape(b, seq_len, 4 * LANE)
    nsa_cols = (c2, C_OFF["nsa_q"], C_OFF["misc"], tabs)
    cmp_w = (w["cmp_pos"], w["cmp_w1"], w["cmp_w2"])
    if past_rows is None:
        ckv = nsa_compress(rows, *cmp_w)
        no_new = jnp.zeros((b, LANE, 2 * LANE), MXU_DT)
        o_d = nsa_attend(*nsa_cols, ckv, kvb, (0, 1), no_new, 0, kvb, (2, 3), 0, b, seq_len, 0)
        new_win = win_rows[:, max(seq_len - WINDOW, 0):]
    else:
        n_past = past_rows.shape[1]
        assert n_past % CMP_STRIDE == 0 and seq_len < CMP_STRIDE
        past = past_rows.reshape(b, n_past, 4 * LANE)
        ckv = nsa_compress(past, *cmp_w)
        new_kv = jnp.pad(kvb[:, :, 0:2 * LANE], ((0, 0), (0, LANE - seq_len), (0, 0)))
        nb = win_buf.shape[1]
        w_all = jnp.concatenate([win_buf.reshape(b, nb, 2 * LANE).astype(jnp.float32), win_rows], axis=1)
        w_pad = jnp.pad(w_all, ((0, 0), (0, max(NSA_WSPAN - nb - seq_len, 0)), (0, 0)))
        o_d = nsa_attend(*nsa_cols, ckv, past, (2, 3), new_kv, seq_len, w_pad, (0, 1), start - nb, b, seq_len, start)
        new_win = w_all[:, seq_len:]
    rows = rows.reshape(b, seq_len, 4, NSA_KV_HEADS, NSA_HD)
    new_win = new_win.reshape(b, new_win.shape[1], 2, NSA_KV_HEADS, NSA_HD)
    branches = [o_a.reshape(m, BRANCH_W), o_b.reshape(m, BRANCH_W), o_c.reshape(m, BRANCH_W), o_d.reshape(m, BRANCH_W)]
    x2 = _merge_out(x2, branches, c2, C_OFF["merge_gate"], w["w_branch"], w["w_out"])
    x2 = _mlp(x2, w["norm_mlp"], w["w_up"], w["w_down"])
    return x2.reshape(b, seq_len, D_MODEL), (hg_s, dn_s, dn_buf, sc_buf, new_win, rows)


def cached_mixers(x, c_all, hg_s, dn_s, dn_buf, sc_buf, w):
    b, seq_len, _ = x.shape
    f32 = jnp.float32
    c = {n: c_all[..., C_OFF[n]:C_OFF[n] + wd] for n, wd in C_ORDER}
    for n, o0, wd in MISC_COLS:
        c[n] = c["misc"][..., o0:o0 + wd]
    lb = w["lb"]
    hq = jax.nn.silu(c["hg_q"]).reshape(b, seq_len, HG_HEADS, HG_DK)
    z = c["hg_f"].astype(f32).reshape(b, seq_len, HG_HEADS, HG_DK)
    logf = jnp.logaddexp(jnp.log(lb), jnp.log1p(-lb) + jax.nn.log_sigmoid(z))
    hk = (1.0 - lb) * jax.nn.sigmoid(-z)
    hv = c["hg_i"].reshape(b, seq_len, HG_HEADS, HG_DV)
    o_a, hg_s = hgrn2_scan(hq, hk, hv, logf, hg_s)
    o_a = rms_norm(o_a.astype(x.dtype), w["hg_norm"]) * jax.nn.silu(c["hg_g"].reshape(b, seq_len, HG_HEADS, HG_DV))
    qkv, dn_buf = causal_conv(c["dn_qkv"], dn_buf, w["dn_conv"])
    dq, dk, dv = jnp.split(jax.nn.silu(qkv), 3, axis=-1)
    dq = l2_normalize(dq.reshape(b, seq_len, DN_HEADS, DN_DK)) * DN_DK ** -0.5
    dk = l2_normalize(dk.reshape(b, seq_len, DN_HEADS, DN_DK))
    beta = jax.nn.sigmoid(c["dn_b"].astype(f32))
    g = -jnp.exp(w["dn_a_log"].astype(f32)) * jax.nn.softplus(c["dn_a"].astype(f32) + w["dn_dt_bias"])
    o_b, dn_s = gated_delta_scan(dq, dk, dv.reshape(b, seq_len, DN_HEADS, DN_DV), beta, g, dn_s)
    o_b = rms_norm(o_b.astype(x.dtype), w["dn_norm"]) * jax.nn.silu(c["dn_z"].reshape(b, seq_len, DN_HEADS, DN_DV))
    gb, gc, hx = jnp.split(c["sc_bch"], 3, axis=-1)
    conv, sc_buf = causal_conv(gc * hx, sc_buf, w["sc_conv"])
    o_c = gb * conv
    return o_a, o_b, o_c, (hg_s, dn_s, dn_buf, sc_buf)


def kernel(x_prompt, x_sample, state_hgrn, state_dn, state_dn_conv, state_sc_conv, state_win_kv, cache_kv, page_table, norm_mix, norm_mlp, norm_final, w_in, hg_lb_logits, hg_norm, dn_conv, dn_a_log, dn_dt_bias, dn_norm, sc_conv, cmp_pos, cmp_w1, cmp_w2, w_branch, w_out, w_up, w_down):
    f32 = jnp.float32
    bf16 = jnp.bfloat16
    lbs = jnp.cumsum(jax.nn.softmax(hg_lb_logits.astype(f32), axis=0), axis=0)
    lbs = lbs - lbs[:1]
    w_in_b = _permute_w_in(w_in).astype(bf16)
    w_branch_b, w_out_b, w_up_b, w_down_b = (a.astype(bf16) for a in (w_branch, w_out, w_up, w_down))

    def layer_w(l):
        return dict(norm_mix=norm_mix[l], norm_mlp=norm_mlp[l], w_in=w_in_b[l], lb=lbs[l], hg_norm=hg_norm[l],
                    dn_conv=dn_conv[l], dn_a_log=dn_a_log[l], dn_dt_bias=dn_dt_bias[l], dn_norm=dn_norm[l],
                    sc_conv=sc_conv[l], cmp_pos=cmp_pos[l], cmp_w1=cmp_w1[l], cmp_w2=cmp_w2[l],
                    w_branch=w_branch_b[l], w_out=w_out_b[l], w_up=w_up_b[l], w_down=w_down_b[l])

    bp = x_prompt.shape[0]
    yp = x_prompt
    p_st = []
    for l in range(DEPTH):
        yp, st = trunk_layer(yp, 0,
                             jnp.zeros((bp, HG_HEADS, HG_DK, HG_DV), f32),
                             jnp.zeros((bp, DN_HEADS, DN_DK, DN_DV), f32),
                             jnp.zeros((bp, DN_CONV - 1, 3 * BRANCH_W), x_prompt.dtype),
                             jnp.zeros((bp, SC_CONV - 1, SC_W), x_prompt.dtype),
                             None, None, layer_w(l))
        p_st.append(st)
    ys = x_sample
    s_st = []
    n_pool = cache_kv.shape[1]
    pool = cache_kv.reshape(DEPTH * n_pool, *cache_kv.shape[2:])
    for l in range(DEPTH):
        past = pool[l * n_pool + page_table]
        past = past.reshape(page_table.shape[0], -1, *past.shape[3:])
        ys, st = trunk_layer(ys, past.shape[1], state_hgrn[l], state_dn[l], state_dn_conv[l], state_sc_conv[l],
                             past, state_win_kv[l], layer_w(l))
        s_st.append(st)
    p = [jnp.stack([st[i] for st in p_st]) for i in range(6)]
    s = [jnp.stack([st[i] for st in s_st]) for i in range(6)]
    return (rms_norm(yp, norm_final), rms_norm(ys, norm_final),
            p[0], p[1], p[2], p[3], p[4], p[5],
            s[0], s[1], s[2], s[3], s[4], s[5])
```

```python
import math, functools
import jax, jax.numpy as jnp
from jax import lax
import numpy as np
from jax.experimental import pallas as pl
from jax.experimental.pallas import tpu as pltpu

D_MODEL = 1024
DEPTH = 4
PAGE_SIZE = 128
N_BRANCH = 4
BRANCH_W = D_MODEL // 2
HG_HEADS = 4
HG_DK = BRANCH_W // HG_HEADS
HG_DV = BRANCH_W // HG_HEADS
HG_CHUNK = 64
DN_HEADS = 4
DN_DK = BRANCH_W // DN_HEADS
DN_DV = BRANCH_W // DN_HEADS
DN_CONV = 4
DN_CHUNK = 64
SC_W = BRANCH_W
SC_CONV = 3
NSA_HEADS = 8
NSA_KV_HEADS = 2
NSA_HD = BRANCH_W // NSA_HEADS
NSA_GROUP = NSA_HEADS // NSA_KV_HEADS
ROPE_DIM = NSA_HD // 4
ROPE_THETA = 500000.0
CMP_BLOCK = 32
CMP_STRIDE = 16
CMP_HIDDEN = 4 * NSA_HD
SEL_BLOCK = 64
SEL_TOPN = 16
WINDOW = 512
Q_BLOCK = 128
D_FF = 4 * D_MODEL
EPS = 1e-6

IN_SPLITS = (
    ("hg_q", BRANCH_W), ("hg_f", BRANCH_W), ("hg_i", BRANCH_W), ("hg_g", BRANCH_W),
    ("dn_qkv", 3 * BRANCH_W), ("dn_b", DN_HEADS), ("dn_a", DN_HEADS), ("dn_z", BRANCH_W),
    ("sc_bch", 3 * SC_W),
    ("nsa_q", NSA_HEADS * NSA_HD), ("nsa_kv", 6 * NSA_KV_HEADS * NSA_HD), ("nsa_gate", 3 * NSA_HEADS),
    ("merge_gate", N_BRANCH * D_MODEL),
)
IN_NAMES = tuple(n for n, _ in IN_SPLITS)
IN_CUTS = tuple(int(c) for c in np.cumsum([s for _, s in IN_SPLITS])[:-1])
N_IN = sum(s for _, s in IN_SPLITS)

LANE = 128
PROJ_TM = 1024
PROJ_TN = 1024
VMEM_LIMIT = 48 * 1024 * 1024
MXU_DT = jnp.bfloat16

MISC_W = 2 * LANE
C_ORDER = (("merge_gate", N_BRANCH * D_MODEL), ("hg_q", BRANCH_W), ("hg_f", BRANCH_W), ("hg_i", BRANCH_W),
           ("hg_g", BRANCH_W), ("dn_qkv", 3 * BRANCH_W), ("sc_bch", 3 * SC_W), ("dn_z", BRANCH_W),
           ("nsa_q", NSA_HEADS * NSA_HD), ("nsa_kv", 6 * NSA_KV_HEADS * NSA_HD), ("misc", MISC_W))
C_OFF = {}
_o = 0
for _n, _w in C_ORDER:
    C_OFF[_n] = _o
    _o += _w
N_IN_PAD = _o
assert N_IN_PAD % PROJ_TN == 0
MISC_COLS = (("dn_b", 0, DN_HEADS), ("dn_a", DN_HEADS, DN_HEADS), ("nsa_gate", 2 * DN_HEADS, 3 * NSA_HEADS))
NEG_INF = float("-inf")


def _permute_w_in(w_in):
    src = dict(zip(IN_NAMES, jnp.split(w_in, IN_CUTS, axis=-1)))
    misc = jnp.concatenate([src[n] for n, _, _ in MISC_COLS], axis=-1)
    src["misc"] = jnp.pad(misc, ((0, 0), (0, 0), (0, MISC_W - misc.shape[-1])))
    return jnp.concatenate([src[n] for n, _ in C_ORDER], axis=-1)


def _row_tile(m):
    return 512 if m % 512 == 0 else m


def _norm_proj_kernel(x_ref, g_ref, w_ref, o_ref, h_ref):
    @pl.when(pl.program_id(1) == 0)
    def _():
        x = x_ref[...]
        y = x * lax.rsqrt(jnp.mean(x * x, axis=-1, keepdims=True) + EPS)
        h_ref[...] = (y * g_ref[...]).astype(jnp.bfloat16)

    o_ref[...] = jnp.dot(h_ref[...], w_ref[...], preferred_element_type=jnp.float32)


def _norm_proj(x, g, w_bf16):
    m, d = x.shape
    n = w_bf16.shape[1]
    tm = PROJ_TM if m % PROJ_TM == 0 else m
    return pl.pallas_call(
        _norm_proj_kernel,
        grid=(m // tm, n // PROJ_TN),
        in_specs=[pl.BlockSpec((tm, d), lambda i, j: (i, 0)),
                  pl.BlockSpec((1, d), lambda i, j: (0, 0)),
                  pl.BlockSpec((d, PROJ_TN), lambda i, j: (0, j))],
        out_specs=pl.BlockSpec((tm, PROJ_TN), lambda i, j: (i, j)),
        out_shape=jax.ShapeDtypeStruct((m, n), jnp.float32),
        scratch_shapes=[pltpu.VMEM((tm, d), jnp.bfloat16)],
        compiler_params=pltpu.CompilerParams(dimension_semantics=("arbitrary", "arbitrary"),
                                             vmem_limit_bytes=VMEM_LIMIT),
        name="norm_proj",
    )(x, g.reshape(1, d), w_bf16)


def _merge_kernel(x_ref, ba_ref, bb_ref, bc_ref, bd_ref, gate_ref, wb_ref, wo_ref, o_ref):
    acc = None
    for n, b_ref in enumerate((ba_ref, bb_ref, bc_ref, bd_ref)):
        p = jnp.dot(b_ref[...].astype(jnp.bfloat16), wb_ref[n], preferred_element_type=jnp.float32)
        t = jax.nn.sigmoid(gate_ref[:, n * D_MODEL:(n + 1) * D_MODEL]) * p
        acc = t if acc is None else acc + t
    o_ref[...] = x_ref[...] + jnp.dot(acc.astype(jnp.bfloat16), wo_ref[...], preferred_element_type=jnp.float32)


def _merge_out(x, branches, c_all, gate_col, wb_bf16, wo_bf16):
    m, d = x.shape
    tm = 256 if m % 256 == 0 else m
    row = lambda w: pl.BlockSpec((tm, w), lambda i: (i, 0))
    return pl.pallas_call(
        _merge_kernel,
        grid=(m // tm,),
        in_specs=[row(d)] + [row(BRANCH_W)] * N_BRANCH + [
                  pl.BlockSpec((tm, N_BRANCH * d), lambda i: (i, gate_col // (N_BRANCH * d))),
                  pl.BlockSpec((N_BRANCH, BRANCH_W, d), lambda i: (0, 0, 0)),
                  pl.BlockSpec((d, d), lambda i: (0, 0))],
        out_specs=row(d),
        out_shape=jax.ShapeDtypeStruct((m, d), jnp.float32),
        compiler_params=pltpu.CompilerParams(dimension_semantics=("arbitrary",),
                                             vmem_limit_bytes=VMEM_LIMIT),
        name="merge_out",
    )(x, *branches, c_all, wb_bf16, wo_bf16)


MLP_TF = 1024


def _mlp_kernel(x_ref, g_ref, wu_ref, wd_ref, o_ref, h_ref, acc_ref):
    j = pl.program_id(1)

    @pl.when(j == 0)
    def _():
        x = x_ref[...]
        y = x * lax.rsqrt(jnp.mean(x * x, axis=-1, keepdims=True) + EPS)
        h_ref[...] = (y * g_ref[...]).astype(jnp.bfloat16)
        acc_ref[...] = jnp.zeros_like(acc_ref)

    u = jnp.maximum(jnp.dot(h_ref[...], wu_ref[...], preferred_element_type=jnp.float32), 0.0)
    acc_ref[...] += jnp.dot((u * u).astype(jnp.bfloat16), wd_ref[...], preferred_element_type=jnp.float32)

    @pl.when(j == pl.num_programs(1) - 1)
    def _():
        o_ref[...] = x_ref[...] + acc_ref[...]


def _mlp(x, g, wu_bf16, wd_bf16):
    m, d = x.shape
    f = wu_bf16.shape[1]
    tm = _row_tile(m)
    return pl.pallas_call(
        _mlp_kernel,
        grid=(m // tm, f // MLP_TF),
        in_specs=[pl.BlockSpec((tm, d), lambda i, j: (i, 0)),
                  pl.BlockSpec((1, d), lambda i, j: (0, 0)),
                  pl.BlockSpec((d, MLP_TF), lambda i, j: (0, j)),
                  pl.BlockSpec((MLP_TF, d), lambda i, j: (j, 0))],
        out_specs=pl.BlockSpec((tm, d), lambda i, j: (i, 0)),
        out_shape=jax.ShapeDtypeStruct((m, d), jnp.float32),
        scratch_shapes=[pltpu.VMEM((tm, d), jnp.bfloat16), pltpu.VMEM((tm, d), jnp.float32)],
        compiler_params=pltpu.CompilerParams(dimension_semantics=("arbitrary", "arbitrary"),
                                             vmem_limit_bytes=VMEM_LIMIT),
        name="mlp",
    )(x, g.reshape(1, d), wu_bf16, wd_bf16)


NSA_TQ = 128
NSA_TK = 512
NSA_ROWS = NSA_HEADS * NSA_TQ
NSA_WSPAN = WINDOW + NSA_TQ


def _dot(a, b, **kw):
    return jnp.dot(a, b, preferred_element_type=jnp.float32, **kw)


def _dot_nt(a, b):
    return lax.dot_general(a, b, (((1,), (1,)), ((), ())), preferred_element_type=jnp.float32)


def rope_tables(pos):
    half = ROPE_DIM // 2
    inv_freq = ROPE_THETA ** (-jnp.arange(half, dtype=jnp.float32) / half)
    ang = pos.astype(jnp.float32)[:, None] * inv_freq
    cos, sin = jnp.cos(ang), jnp.sin(ang)
    n = pos.shape[0]
    one = jnp.ones((n, NSA_HD - ROPE_DIM), jnp.float32)
    zero = jnp.zeros((n, NSA_HD - ROPE_DIM), jnp.float32)
    z8 = jnp.zeros((n, half), jnp.float32)
    c = jnp.concatenate([cos, cos, one], axis=1)
    s1 = jnp.concatenate([-sin, z8, zero], axis=1)
    s2 = jnp.concatenate([z8, sin, zero], axis=1)
    two = lambda a: jnp.concatenate([a, a], axis=1)
    return two(c), two(s1), two(s2)


def _rope(x, c, s1, s2):
    n = x.shape[-1]
    return x * c + pltpu.roll(x, n - ROPE_DIM // 2, 1) * s1 + pltpu.roll(x, ROPE_DIM // 2, 1) * s2


def _nsa_prep_kernel(kv0_ref, kv1_ref, kv2_ref, c_ref, s1_ref, s2_ref, rows_ref, win_ref, kvb_ref):
    c, s1, s2 = c_ref[...], s1_ref[...], s2_ref[...]
    cmp_kv = kv0_ref[...]
    sel = kv1_ref[...]
    wnd = kv2_ref[...]
    ks = _rope(sel[:, :LANE], c, s1, s2)
    kw = _rope(wnd[:, :LANE], c, s1, s2)
    rows_ref[:, 0:2 * LANE] = cmp_kv
    rows_ref[:, 2 * LANE:3 * LANE] = ks
    rows_ref[:, 3 * LANE:4 * LANE] = sel[:, LANE:]
    win_ref[:, 0:LANE] = kw
    win_ref[:, LANE:2 * LANE] = wnd[:, LANE:]
    kvb_ref[:, 0:LANE] = ks.astype(MXU_DT)
    kvb_ref[:, LANE:2 * LANE] = sel[:, LANE:].astype(MXU_DT)
    kvb_ref[:, 2 * LANE:3 * LANE] = kw.astype(MXU_DT)
    kvb_ref[:, 3 * LANE:4 * LANE] = wnd[:, LANE:].astype(MXU_DT)


def nsa_prep(c_all, kv_col0, tabs, seq_len):
    m = c_all.shape[0]
    tm = 512 if seq_len % 512 == 0 else seq_len
    nlt = seq_len // tm
    cb = kv_col0 // (2 * LANE)
    kv_spec = lambda k: pl.BlockSpec((tm, 2 * LANE), lambda i, k=k: (i, cb + k))
    tab_spec = pl.BlockSpec((tm, LANE), lambda i: (i % nlt, 0))
    return pl.pallas_call(
        _nsa_prep_kernel,
        grid=(m // tm,),
        in_specs=[kv_spec(0), kv_spec(1), kv_spec(2), tab_spec, tab_spec, tab_spec],
        out_specs=[pl.BlockSpec((tm, 4 * LANE), lambda i: (i, 0)),
                   pl.BlockSpec((tm, 2 * LANE), lambda i: (i, 0)),
                   pl.BlockSpec((tm, 4 * LANE), lambda i: (i, 0))],
        out_shape=[jax.ShapeDtypeStruct((m, 4 * LANE), jnp.float32),
                   jax.ShapeDtypeStruct((m, 2 * LANE), jnp.float32),
                   jax.ShapeDtypeStruct((m, 4 * LANE), MXU_DT)],
        compiler_params=pltpu.CompilerParams(dimension_semantics=("arbitrary",), vmem_limit_bytes=VMEM_LIMIT),
        name="nsa_prep",
    )(c_all, c_all, c_all, *tabs)


def _nsa_compress_kernel(x_ref, pos_ref, w1_ref, w2_ref, o_ref):
    nh = x_ref.shape[0] // CMP_STRIDE
    top = jnp.zeros((nh, NSA_KV_HEADS * CMP_HIDDEN), jnp.float32)
    bot = jnp.zeros((nh, NSA_KV_HEADS * CMP_HIDDEN), jnp.float32)
    for t in range(CMP_STRIDE):
        x = x_ref[pl.ds(t, nh, stride=CMP_STRIDE), :]
        top = top + _dot((x + pos_ref[0, t:t + 1, :]).astype(MXU_DT), w1_ref[0, 0, t])
        bot = bot + _dot((x + pos_ref[0, CMP_STRIDE + t:CMP_STRIDE + t + 1, :]).astype(MXU_DT), w1_ref[0, 1, t])
    h = top + pltpu.roll(bot, nh - 1, 0)
    h = h * jax.nn.sigmoid(h)
    o_ref[0, 0] = _dot(h.astype(MXU_DT), w2_ref[0]).astype(o_ref.dtype)


def _block_diag2(w):
    z = jnp.zeros_like(w)
    return jnp.concatenate([jnp.concatenate([w, z], axis=-1), jnp.concatenate([z, w], axis=-1)], axis=-2)


def nsa_compress(seq_rows, cmp_pos, cmp_w1, cmp_w2):
    b, seq_len = seq_rows.shape[:2]
    nh = seq_len // CMP_STRIDE
    pos = jnp.concatenate([cmp_pos, cmp_pos], axis=-1)
    w1 = cmp_w1.reshape(2, 2, CMP_STRIDE, NSA_HD, CMP_HIDDEN)
    w1 = _block_diag2(w1).astype(MXU_DT)
    w2 = _block_diag2(cmp_w2).astype(MXU_DT)
    return pl.pallas_call(
        _nsa_compress_kernel,
        grid=(b, 2),
        in_specs=[pl.BlockSpec((None, seq_len, LANE), lambda i, j: (i, 0, j)),
                  pl.BlockSpec((1, CMP_BLOCK, LANE), lambda i, j: (j, 0, 0)),
                  pl.BlockSpec((1, 2, CMP_STRIDE, LANE, NSA_KV_HEADS * CMP_HIDDEN), lambda i, j: (j, 0, 0, 0, 0)),
                  pl.BlockSpec((1, NSA_KV_HEADS * CMP_HIDDEN, LANE), lambda i, j: (j, 0, 0))],
        out_specs=pl.BlockSpec((1, 1, nh, LANE), lambda i, j: (i, j, 0, 0)),
        out_shape=jax.ShapeDtypeStruct((b, 2, nh, LANE), MXU_DT),
        compiler_params=pltpu.CompilerParams(dimension_semantics=("arbitrary", "arbitrary"), vmem_limit_bytes=VMEM_LIMIT),
        name="nsa_compress",
    )(seq_rows, pos, w1, w2)


def _pad_heads(q):
    lane = lax.broadcasted_iota(jnp.int32, (q.shape[0], LANE), 1)
    blocks = []
    for h in range(NSA_HEADS):
        blk = q[:, (h // 2) * LANE:(h // 2 + 1) * LANE]
        g = h // NSA_GROUP
        if h % 2 != g:
            blk = pltpu.roll(blk, NSA_HD, 1)
        keep = (lane < NSA_HD) if g == 0 else (lane >= NSA_HD)
        blocks.append(jnp.where(keep, blk, 0.0))
    return jnp.concatenate(blocks, axis=0)


def _softmax_rows(s):
    m = jnp.max(s, axis=-1, keepdims=True)
    m = jnp.where(m == NEG_INF, 0.0, m)
    p = jnp.exp(s - m)
    return p / jnp.maximum(jnp.sum(p, axis=-1, keepdims=True), 1e-30)


def _nsa_attn_kernel(q_ref, gate_ref, c_ref, s1_ref, s2_ref, ck_ref, cv_ref, ks_ref, vs_ref, kn_ref, kw_ref, vw_ref,
                     o_ref, m_ref, l_ref, acc_ref, s_ref, *, q_start, win_start, ns, n_new):
    tq = q_ref.shape[1]
    n_rows = NSA_HEADS * tq
    n_keys = ks_ref.shape[1]
    i = pl.program_id(1)
    q0 = q_start + i * tq
    scale = NSA_HD ** -0.5
    q = q_ref[0] * scale
    c = jnp.concatenate([c_ref[...]] * 4, axis=1)
    s1 = jnp.concatenate([s1_ref[...]] * 4, axis=1)
    s2 = jnp.concatenate([s2_ref[...]] * 4, axis=1)
    q_raw = _pad_heads(q).astype(MXU_DT)
    q_rot = _pad_heads(_rope(q, c, s1, s2)).astype(MXU_DT)

    row = lax.broadcasted_iota(jnp.int32, (n_rows, 1), 0)
    qpos = q0 + (row & (tq - 1))

    nc = ck_ref.shape[1]
    s_c = _dot_nt(q_raw, ck_ref[0])
    c_end = lax.broadcasted_iota(jnp.int32, (1, nc), 1) * CMP_STRIDE + (CMP_BLOCK - 1)
    p_c = _softmax_rows(jnp.where(c_end <= qpos, s_c, NEG_INF))
    o_c = _dot(p_c.astype(MXU_DT), cv_ref[0])

    psum = jnp.concatenate(
        [sum(p_c[(g * NSA_GROUP + j) * tq:(g * NSA_GROUP + j + 1) * tq] for j in range(NSA_GROUP))
         for g in range(NSA_KV_HEADS)], axis=0)
    n_i = lax.broadcasted_iota(jnp.int32, (nc, ns), 0) * CMP_STRIDE
    m_i = lax.broadcasted_iota(jnp.int32, (nc, ns), 1) * SEL_BLOCK
    overlap = ((n_i <= m_i + (SEL_BLOCK - 1)) & (n_i + (CMP_BLOCK - 1) >= m_i)).astype(jnp.float32)
    imp = _dot(psum, overlap, precision=lax.Precision.HIGHEST)
    r2 = lax.broadcasted_iota(jnp.int32, (NSA_KV_HEADS * tq, 1), 0)
    qpos2 = q0 + (r2 & (tq - 1))
    cur = qpos2 >> 6
    blk = lax.broadcasted_iota(jnp.int32, (1, ns), 1)
    forced = (blk == 0) | (blk == cur) | (blk == cur - 1)
    valid = blk * SEL_BLOCK <= qpos2
    v = jnp.where(forced, jnp.inf, jnp.where(valid, imp, NEG_INF))
    blk_f = blk.astype(jnp.float32)
    sel = jnp.zeros(v.shape, jnp.float32)
    for _ in range(SEL_TOPN):
        mx = jnp.max(v, axis=-1, keepdims=True)
        first = jnp.min(jnp.where(v == mx, blk_f, float(ns)), axis=-1, keepdims=True)
        pick = blk_f == first
        sel = jnp.where(pick, 1.0, sel)
        v = jnp.where(pick, NEG_INF, v)
    sel_b = sel.astype(MXU_DT)

    m_ref[...] = jnp.full(m_ref.shape, NEG_INF, jnp.float32)
    l_ref[...] = jnp.zeros(l_ref.shape, jnp.float32)
    acc_ref[...] = jnp.zeros(acc_ref.shape, jnp.float32)
    n_kt = jnp.minimum((q0 + tq + NSA_TK - 1) // NSA_TK, n_keys // NSA_TK)

    def scores(k):
        return _dot_nt(q_rot, k.astype(MXU_DT))

    def update(s, vv, tok0):
        nk = s.shape[1]
        tok = tok0 + lax.broadcasted_iota(jnp.int32, (1, nk), 1)
        e_m = lax.broadcasted_iota(jnp.int32, (ns, nk), 0)
        e_t = (tok0 + lax.broadcasted_iota(jnp.int32, (ns, nk), 1)) >> 6
        chosen = _dot(sel_b, (e_m == e_t).astype(MXU_DT))
        bias2 = jnp.where((chosen > 0.5) & (tok <= qpos2), 0.0, NEG_INF)
        bias = jnp.concatenate([bias2[0:tq]] * NSA_GROUP + [bias2[tq:2 * tq]] * NSA_GROUP, axis=0)
        s = s + bias
        m_old = m_ref[...]
        m_new = jnp.maximum(m_old, jnp.max(s, axis=-1, keepdims=True))
        m_safe = jnp.where(m_new == NEG_INF, 0.0, m_new)
        alpha = jnp.exp(m_old - m_safe)
        p = jnp.exp(s - m_safe)
        l_ref[...] = alpha * l_ref[...] + jnp.sum(p, axis=-1, keepdims=True)
        acc_ref[...] = alpha * acc_ref[...] + _dot(p.astype(MXU_DT), vv.astype(MXU_DT))
        m_ref[...] = m_new

    def body(kt, carry):
        koff = pl.multiple_of(kt * NSA_TK, NSA_TK)
        knext = pl.multiple_of(jnp.minimum(kt + 1, n_kt - 1) * NSA_TK, NSA_TK)
        s = s_ref[...]
        s_next = scores(ks_ref[0, pl.ds(knext, NSA_TK), :])
        update(s, vs_ref[0, pl.ds(koff, NSA_TK), :], koff)
        s_ref[...] = s_next
        return carry

    s_ref[...] = scores(ks_ref[0, 0:NSA_TK, :])
    lax.fori_loop(0, n_kt, body, 0)
    if n_new:
        update(scores(kn_ref[0, :, 0:LANE]), kn_ref[0, :, LANE:2 * LANE], n_keys)
    o_s = acc_ref[...] / jnp.maximum(l_ref[...], 1e-30)

    w0 = jnp.clip(q0 - win_start - WINDOW, 0, kw_ref.shape[1] - NSA_WSPAN)
    w0 = pl.multiple_of(w0, tq)
    kw = kw_ref[0, pl.ds(w0, NSA_WSPAN), :]
    vw = vw_ref[0, pl.ds(w0, NSA_WSPAN), :]
    s_w = _dot_nt(q_rot, kw.astype(MXU_DT))
    kpos = win_start + w0 + lax.broadcasted_iota(jnp.int32, (1, NSA_WSPAN), 1)
    w_ok = (kpos <= qpos) & (kpos > qpos - WINDOW)
    p_w = _softmax_rows(jnp.where(w_ok, s_w, NEG_INF))
    o_w = _dot(p_w.astype(MXU_DT), vw.astype(MXU_DT))

    gates = jax.nn.sigmoid(gate_ref[0])
    lane = lax.broadcasted_iota(jnp.int32, (tq, LANE), 1)
    outs = []
    for h in range(NSA_HEADS):
        sl = slice(h * tq, (h + 1) * tq)
        k0 = 2 * DN_HEADS + 3 * h
        o_h = (gates[:, k0:k0 + 1] * o_c[sl] + gates[:, k0 + 1:k0 + 2] * o_s[sl] + gates[:, k0 + 2:k0 + 3] * o_w[sl])
        if h % 2 != h // NSA_GROUP:
            o_h = pltpu.roll(o_h, NSA_HD, 1)
        outs.append(o_h)
    for p2 in range(NSA_HEADS // 2):
        o_ref[0, :, p2 * LANE:(p2 + 1) * LANE] = jnp.where(lane < NSA_HD, outs[2 * p2], outs[2 * p2 + 1])


def nsa_attend(c_all, q_col0, misc_col0, tabs, ckv, sel_kv, sel_cols, new_kv, n_new, win_kv, win_cols, win_start,
               batch, seq_len, q_start):
    tq = NSA_TQ if seq_len % NSA_TQ == 0 else seq_len
    n_rows = NSA_HEADS * tq
    qb = q_col0 // (4 * LANE)
    mb = misc_col0 // LANE
    c3 = c_all.reshape(batch, seq_len, c_all.shape[1])
    n_blocks = -(-(sel_kv.shape[1] + n_new) // SEL_BLOCK)
    ns = -(-n_blocks // LANE) * LANE
    tab_spec = pl.BlockSpec((tq, LANE), lambda b, i: (i, 0))
    col_spec = lambda a, k: pl.BlockSpec((1, a.shape[1], LANE), lambda b, i, k=k: (b, 0, k))
    nc = ckv.shape[2]
    out = pl.pallas_call(
        functools.partial(_nsa_attn_kernel, q_start=q_start, win_start=win_start, ns=ns, n_new=n_new),
        grid=(batch, seq_len // tq),
        in_specs=[pl.BlockSpec((1, tq, 4 * LANE), lambda b, i: (b, i, qb)),
                  pl.BlockSpec((1, tq, LANE), lambda b, i: (b, i, mb)),
                  tab_spec, tab_spec, tab_spec,
                  pl.BlockSpec((None, 1, nc, LANE), lambda b, i: (b, 0, 0, 0)),
                  pl.BlockSpec((None, 1, nc, LANE), lambda b, i: (b, 1, 0, 0)),
                  col_spec(sel_kv, sel_cols[0]), col_spec(sel_kv, sel_cols[1]),
                  pl.BlockSpec((1, new_kv.shape[1], 2 * LANE), lambda b, i: (b, 0, 0)),
                  col_spec(win_kv, win_cols[0]), col_spec(win_kv, win_cols[1])],
        out_specs=pl.BlockSpec((1, tq, 4 * LANE), lambda b, i: (b, i, 0)),
        out_shape=jax.ShapeDtypeStruct((batch, seq_len, 4 * LANE), jnp.float32),
        scratch_shapes=[pltpu.VMEM((n_rows, 1), jnp.float32), pltpu.VMEM((n_rows, 1), jnp.float32),
                        pltpu.VMEM((n_rows, LANE), jnp.float32), pltpu.VMEM((n_rows, NSA_TK), jnp.float32)],
        compiler_params=pltpu.CompilerParams(dimension_semantics=("arbitrary", "arbitrary"), vmem_limit_bytes=VMEM_LIMIT),
        name="nsa_attn",
    )(c3, c3, *tabs, ckv, ckv, sel_kv, sel_kv, new_kv, win_kv, win_kv)
    return out.reshape(batch * seq_len, 4 * LANE)


def _page_rows_kernel(pt_ref, x_ref, o_ref):
    n_idx, n_g = x_ref.shape[1], x_ref.shape[2]
    for i in range(n_idx):
        for g in range(n_g):
            c0 = (i * n_g + g) * NSA_HD
            o_ref[0, :, c0:c0 + NSA_HD] = x_ref[:, i, g, :]


def page_rows(cache_kv, page_table, layer):
    _, _, page, n_idx, n_g, hd = cache_kv.shape
    b, n_pages = page_table.shape
    w = n_idx * n_g * hd
    return pl.pallas_call(
        _page_rows_kernel,
        grid_spec=pltpu.PrefetchScalarGridSpec(
            num_scalar_prefetch=1,
            grid=(b, n_pages),
            in_specs=[pl.BlockSpec((None, None, page, n_idx, n_g, hd),
                                   lambda i, p, pt: (layer, pt[i, p], 0, 0, 0, 0))],
            out_specs=pl.BlockSpec((1, page, w), lambda i, p, pt: (i, p, 0))),
        out_shape=jax.ShapeDtypeStruct((b, n_pages * page, w), cache_kv.dtype),
        compiler_params=pltpu.CompilerParams(dimension_semantics=("arbitrary", "arbitrary")),
        name="page_rows",
    )(page_table, cache_kv)


SCAN_C = 64
SCAN_TT = 512
HALO = 8


def _dot_tn(a, b):
    return lax.dot_general(a, b, (((0,), (0,)), ((), ())), preferred_element_type=jnp.float32)


def _mx(x):
    return x.astype(MXU_DT)


def _dot3(a, b):
    a_hi, b_hi = _mx(a), _mx(b)
    a_lo = _mx(a - a_hi.astype(jnp.float32))
    b_lo = _mx(b - b_hi.astype(jnp.float32))
    return _dot(a_hi, b_hi) + (_dot(a_hi, b_lo) + _dot(a_lo, b_hi))


def _chunk_cumsum(x):
    c = x.shape[0]
    row = lax.broadcasted_iota(jnp.int32, x.shape, 0)
    d = 1
    while d < c:
        x = x + jnp.where(row >= d, pltpu.roll(x, d, 0), 0.0)
        d *= 2
    return x


def _silu(x):
    return x * jax.nn.sigmoid(x)


def _conv_tile(x, halo, w_ref):
    kw = w_ref.shape[0]
    r8 = lax.broadcasted_iota(jnp.int32, (HALO, x.shape[1]), 0)
    y = x * w_ref[kw - 1:kw, :]
    for d in range(1, kw):
        xs = pltpu.roll(x, d, 0)
        head = jnp.where(r8 < d, pltpu.roll(halo, d, 0), xs[0:HALO])
        xs = jnp.concatenate([head, xs[HALO:]], axis=0)
        y = y + xs * w_ref[kw - 1 - d:kw - d, :]
    return y


def _halo_spec(tt, width, col_block):
    return pl.BlockSpec((HALO, width), lambda i: (jnp.maximum(i * (tt // HALO) - 1, 0), col_block))


def _pad_buf(buf):
    return jnp.pad(buf.astype(jnp.float32), ((0, 0), (HALO - buf.shape[1], 0), (0, 0)))


def _hgrn_prep_kernel(q_ref, f_ref, v_ref, la_ref, l1_ref, lbc_ref, oi_ref, qb_ref, kb_ref, eb_ref):
    c = SCAN_C
    tt = q_ref.shape[0]
    row = lax.broadcasted_iota(jnp.int32, (c, LANE), 0)
    ti = lax.broadcasted_iota(jnp.int32, (c, c), 0)
    si = lax.broadcasted_iota(jnp.int32, (c, c), 1)
    for h in range(HG_HEADS):
        ls = slice(h * LANE, (h + 1) * LANE)
        la, l1, lbc = la_ref[:, ls], l1_ref[:, ls], lbc_ref[:, ls]
        for ci in range(tt // c):
            rs = slice(ci * c, (ci + 1) * c)
            q = _silu(q_ref[rs, ls])
            z = f_ref[rs, ls]
            v = v_ref[rs, ls]
            lsig = jnp.minimum(z, 0.0) - jnp.log1p(jnp.exp(-jnp.abs(z)))
            t2 = l1 + lsig
            hi = jnp.maximum(la, t2)
            logf = hi + jnp.log1p(jnp.exp(-jnp.abs(la - t2)))
            k = lbc * jax.nn.sigmoid(-z)
            b = _chunk_cumsum(logf)
            b_last = b[c - 1:c, :]
            qb_ref[rs, ls] = q * jnp.exp(b)
            kb_ref[rs, ls] = k * jnp.exp(b_last - b)
            eb_ref[ci:ci + 1, ls] = jnp.exp(b_last)
            att = jnp.zeros((c, c), jnp.float32)
            n = c // 2
            while n >= 8:
                blk = 2 * n
                ref_rows = jnp.concatenate(
                    [jnp.broadcast_to(b[j * blk + n - 1:j * blk + n, :], (blk, LANE)) for j in range(c // blk)], axis=0)
                upper = (row & (blk - 1)) >= n
                qs = q * jnp.exp(jnp.where(upper, b - ref_rows, 0.0))
                ks = k * jnp.exp(jnp.where(upper, 0.0, ref_rows - b))
                lvl = _dot_nt(_mx(qs), _mx(ks))
                ok = ((ti & ~(blk - 1)) == (si & ~(blk - 1))) & ((ti & (blk - 1)) >= n) & ((si & (blk - 1)) < n)
                att = att + jnp.where(ok, lvl, 0.0)
                n //= 2
            o = _dot(_mx(att), _mx(v))
            for d in range(8):
                if d == 0:
                    w = jnp.sum(q * k, axis=-1, keepdims=True)
                    o = o + w * v
                else:
                    e = jnp.exp(jnp.minimum(b - pltpu.roll(b, d, 0), 0.0))
                    w = jnp.sum(q * pltpu.roll(k, d, 0) * e, axis=-1, keepdims=True)
                    w = jnp.where((row[:, 0:1] & 7) >= d, w, 0.0)
                    o = o + w * pltpu.roll(v, d, 0)
            oi_ref[rs, ls] = o


def hgrn_prep(c_all, col_q, col_f, col_v, lb):
    m = c_all.shape[0]
    tt = SCAN_TT
    lbf = lb.reshape(1, BRANCH_W).astype(jnp.float32)
    la, l1, lbc = jnp.log(lbf), jnp.log1p(-lbf), 1.0 - lbf
    blk = lambda col: pl.BlockSpec((tt, BRANCH_W), lambda i, col=col: (i, col // BRANCH_W))
    vec = pl.BlockSpec((1, BRANCH_W), lambda i: (0, 0))
    out = pl.BlockSpec((tt, BRANCH_W), lambda i: (i, 0))
    return pl.pallas_call(
        _hgrn_prep_kernel,
        grid=(m // tt,),
        in_specs=[blk(col_q), blk(col_f), blk(col_v), vec, vec, vec],
        out_specs=[out, out, out, pl.BlockSpec((tt // SCAN_C, BRANCH_W), lambda i: (i, 0))],
        out_shape=[jax.ShapeDtypeStruct((m, BRANCH_W), jnp.float32)] * 3
                  + [jax.ShapeDtypeStruct((m // SCAN_C, BRANCH_W), jnp.float32)],
        compiler_params=pltpu.CompilerParams(dimension_semantics=("arbitrary",), vmem_limit_bytes=VMEM_LIMIT),
        name="hgrn_prep",
    )(c_all, c_all, c_all, la, l1, lbc)


def _hgrn_scan_kernel(oi_ref, qb_ref, kb_ref, eb_ref, v_ref, g_ref, nw_ref, s0_ref, o_ref, sT_out_ref, sT_ref):
    c = SCAN_C
    nb = oi_ref.shape[0]
    tt = oi_ref.shape[1]
    i = pl.program_id(0)

    @pl.when(i == 0)
    def _():
        sT_ref[...] = s0_ref[...]

    nw = nw_ref[...]
    for ci in range(tt // c):
        rs = slice(ci * c, (ci + 1) * c)
        for b in range(nb):
            for h in range(HG_HEADS):
                ls = slice(h * LANE, (h + 1) * LANE)
                sT = sT_ref[b, h]
                o = oi_ref[b, rs, ls] + _dot_nt(_mx(qb_ref[b, rs, ls]), _mx(sT))
                sT_ref[b, h] = eb_ref[b, ci:ci + 1, ls] * sT + _dot_tn(_mx(v_ref[b, rs, ls]), _mx(kb_ref[b, rs, ls]))
                y = o * lax.rsqrt(jnp.mean(o * o, axis=-1, keepdims=True) + EPS) * nw
                o_ref[b, rs, ls] = y * _silu(g_ref[b, rs, ls])

    @pl.when(i == pl.num_programs(0) - 1)
    def _():
        sT_out_ref[...] = sT_ref[...]


def hgrn_scan(oi, qb, kb, eb, c_all, col_v, col_g, norm_w, s0, batch, seq_len):
    tt = SCAN_TT
    n3 = lambda a: a.reshape(batch, seq_len, a.shape[-1])
    c3 = n3(c_all)
    tok = pl.BlockSpec((batch, tt, BRANCH_W), lambda i: (0, i, 0))
    ctok = lambda col: pl.BlockSpec((batch, tt, BRANCH_W), lambda i, col=col: (0, i, col // BRANCH_W))
    st = pl.BlockSpec((batch, HG_HEADS, LANE, LANE), lambda i: (0, 0, 0, 0))
    o, sT = pl.pallas_call(
        _hgrn_scan_kernel,
        grid=(seq_len // tt,),
        in_specs=[tok, tok, tok, pl.BlockSpec((batch, tt // SCAN_C, BRANCH_W), lambda i: (0, i, 0)),
                  ctok(col_v), ctok(col_g), pl.BlockSpec((1, LANE), lambda i: (0, 0)), st],
        out_specs=[tok, st],
        out_shape=[jax.ShapeDtypeStruct((batch, seq_len, BRANCH_W), jnp.float32),
                   jax.ShapeDtypeStruct((batch, HG_HEADS, LANE, LANE), jnp.float32)],
        scratch_shapes=[pltpu.VMEM((batch, HG_HEADS, LANE, LANE), jnp.float32)],
        compiler_params=pltpu.CompilerParams(dimension_semantics=("arbitrary",), vmem_limit_bytes=VMEM_LIMIT),
        name="hgrn_scan",
    )(n3(oi), n3(qb), n3(kb), eb.reshape(batch, seq_len // SCAN_C, BRANCH_W), c3, c3,
      norm_w.reshape(1, LANE), jnp.swapaxes(s0, -1, -2))
    return o.reshape(batch * seq_len, BRANCH_W), jnp.swapaxes(sT, -1, -2)


def _gdn_prep_kernel(x_ref, halo_ref, buf_ref, misc_ref, cw_ref, alog_ref, dtb_ref,
                     u_ref, w_ref, qg_ref, kg_ref, qk_ref, eg_ref, *, tiles_per_seq):
    c = SCAN_C
    tt = x_ref.shape[0]
    first = (pl.program_id(0) % tiles_per_seq) == 0
    halo = jnp.where(first, buf_ref[0], halo_ref[...])
    y = _silu(_conv_tile(x_ref[...], halo, cw_ref))
    misc = misc_ref[...]
    beta_all = jax.nn.sigmoid(misc)
    sp_in = misc + dtb_ref[...]
    sp = jnp.maximum(sp_in, 0.0) + jnp.log1p(jnp.exp(-jnp.abs(sp_in)))
    g_all = -jnp.exp(alog_ref[...]) * sp
    ti = lax.broadcasted_iota(jnp.int32, (c, c), 0)
    si = lax.broadcasted_iota(jnp.int32, (c, c), 1)
    eye = (ti == si).astype(jnp.float32)
    for ci in range(tt // c):
        rs = slice(ci * c, (ci + 1) * c)
        gam_all = _chunk_cumsum(g_all[rs])
        gam_t = gam_all.T
        heads = []
        for h in range(DN_HEADS):
            q = y[rs, h * LANE:(h + 1) * LANE]
            k = y[rs, BRANCH_W + h * LANE:BRANCH_W + (h + 1) * LANE]
            v = y[rs, 2 * BRANCH_W + h * LANE:2 * BRANCH_W + (h + 1) * LANE]
            q = q * lax.rsqrt(jnp.sum(q * q, axis=-1, keepdims=True) + EPS) * (DN_DK ** -0.5)
            k = k * lax.rsqrt(jnp.sum(k * k, axis=-1, keepdims=True) + EPS)
            beta = beta_all[rs, h:h + 1]
            gam = gam_all[:, DN_HEADS + h:DN_HEADS + h + 1]
            gam_r = gam_t[DN_HEADS + h:DN_HEADS + h + 1, :]
            decay = jnp.exp(jnp.where(si <= ti, gam - gam_r, NEG_INF))
            kb16 = _mx(k)
            kk = _dot_nt(kb16, kb16)
            a = jnp.where(si < ti, beta * kk * decay, 0.0)
            heads.append((q, k, v, beta, gam, decay, kb16, a))
        tinvs = [eye] * DN_HEADS
        s = 1
        while s < c:
            blk = 2 * s
            off = (((ti & ~(blk - 1)) == (si & ~(blk - 1))) & ((ti & (blk - 1)) >= s) & ((si & (blk - 1)) < s))
            a_offs = [jnp.where(off, hd[7], 0.0) for hd in heads]
            if s == 1:
                tinvs = [t - ao for t, ao in zip(tinvs, a_offs)]
            else:
                mids = [_dot3(t, ao) for t, ao in zip(tinvs, a_offs)]
                tinvs = [t - _dot3(md, t) for t, md in zip(tinvs, mids)]
            s = blk
        for h, (q, k, v, beta, gam, decay, kb16, a) in enumerate(heads):
            ls = slice(h * LANE, (h + 1) * LANE)
            e_gam = jnp.exp(gam)
            rhs = jnp.concatenate([v * beta, k * (beta * e_gam)], axis=1)
            sol = rhs + _dot3(tinvs[h] - eye, rhs)
            u_ref[rs, ls] = sol[:, :LANE]
            w_ref[rs, ls] = sol[:, LANE:]
            qg_ref[rs, ls] = q * e_gam
            g_last = gam[c - 1:c, :]
            kg_ref[rs, ls] = k * jnp.exp(g_last - gam)
            qk_ref[rs, h * c:(h + 1) * c] = _dot_nt(_mx(q), kb16) * decay
            eg_ref[ci:ci + 1, ls] = jnp.broadcast_to(jnp.exp(g_last), (1, LANE))


def gdn_prep(c_all, col_x, col_misc, conv_buf, conv_w, a_log, dt_bias, batch, seq_len):
    m = c_all.shape[0]
    tt = SCAN_TT
    tps = seq_len // tt
    xw = 3 * BRANCH_W
    lanes = jnp.zeros((1, LANE), jnp.float32)
    alog = lanes.at[0, DN_HEADS:2 * DN_HEADS].set(a_log.astype(jnp.float32))
    dtb = lanes.at[0, DN_HEADS:2 * DN_HEADS].set(dt_bias.astype(jnp.float32))
    out = pl.BlockSpec((tt, BRANCH_W), lambda i: (i, 0))
    vec = pl.BlockSpec((1, LANE), lambda i: (0, 0))
    return pl.pallas_call(
        functools.partial(_gdn_prep_kernel, tiles_per_seq=tps),
        grid=(m // tt,),
        in_specs=[pl.BlockSpec((tt, xw), lambda i: (i, col_x // xw)),
                  _halo_spec(tt, xw, col_x // xw),
                  pl.BlockSpec((1, HALO, xw), lambda i: (i // tps, 0, 0)),
                  pl.BlockSpec((tt, LANE), lambda i: (i, col_misc // LANE)),
                  pl.BlockSpec((DN_CONV, xw), lambda i: (0, 0)), vec, vec],
        out_specs=[out, out, out, out, pl.BlockSpec((tt, DN_HEADS * SCAN_C), lambda i: (i, 0)),
                   pl.BlockSpec((tt // SCAN_C, BRANCH_W), lambda i: (i, 0))],
        out_shape=[jax.ShapeDtypeStruct((m, BRANCH_W), jnp.float32)] * 4
                  + [jax.ShapeDtypeStruct((m, DN_HEADS * SCAN_C), jnp.float32),
                     jax.ShapeDtypeStruct((m // SCAN_C, BRANCH_W), jnp.float32)],
        compiler_params=pltpu.CompilerParams(dimension_semantics=("arbitrary",), vmem_limit_bytes=VMEM_LIMIT),
        name="gdn_prep",
    )(c_all, c_all, _pad_buf(conv_buf), c_all, conv_w.astype(jnp.float32), alog, dtb)


def _gdn_scan_kernel(u_ref, w_ref, qg_ref, kg_ref, qk_ref, eg_ref, z_ref, nw_ref, s0_ref, o_ref, s_out_ref, s_ref):
    c = SCAN_C
    nb = u_ref.shape[0]
    tt = u_ref.shape[1]
    i = pl.program_id(0)

    @pl.when(i == 0)
    def _():
        s_ref[...] = s0_ref[...]

    nw = nw_ref[...]
    for ci in range(tt // c):
        rs = slice(ci * c, (ci + 1) * c)
        for b in range(nb):
            for h in range(DN_HEADS):
                ls = slice(h * LANE, (h + 1) * LANE)
                s = s_ref[b, h]
                s16 = _mx(s)
                both = _dot(_mx(jnp.concatenate([qg_ref[b, rs, ls], w_ref[b, rs, ls]], axis=0)), s16)
                v_new = u_ref[b, rs, ls] - both[c:]
                v16 = _mx(v_new)
                o = both[:c] + _dot(_mx(qk_ref[b, rs, h * c:(h + 1) * c]), v16)
                s_ref[b, h] = eg_ref[b, ci:ci + 1, ls][:, 0:1] * s + _dot_tn(_mx(kg_ref[b, rs, ls]), v16)
                y = o * lax.rsqrt(jnp.mean(o * o, axis=-1, keepdims=True) + EPS) * nw
                o_ref[b, rs, ls] = y * _silu(z_ref[b, rs, ls])

    @pl.when(i == pl.num_programs(0) - 1)
    def _():
        s_out_ref[...] = s_ref[...]


def gdn_scan(u, w, qg, kg, qk, eg, c_all, col_z, norm_w, s0, batch, seq_len):
    tt = SCAN_TT
    n3 = lambda a: a.reshape(batch, seq_len, a.shape[-1])
    tok = pl.BlockSpec((batch, tt, BRANCH_W), lambda i: (0, i, 0))
    st = pl.BlockSpec((batch, DN_HEADS, LANE, LANE), lambda i: (0, 0, 0, 0))
    o, s = pl.pallas_call(
        _gdn_scan_kernel,
        grid=(seq_len // tt,),
        in_specs=[tok, tok, tok, tok, pl.BlockSpec((batch, tt, DN_HEADS * SCAN_C), lambda i: (0, i, 0)),
                  pl.BlockSpec((batch, tt // SCAN_C, BRANCH_W), lambda i: (0, i, 0)),
                  pl.BlockSpec((batch, tt, BRANCH_W), lambda i: (0, i, col_z // BRANCH_W)),
                  pl.BlockSpec((1, LANE), lambda i: (0, 0)), st],
        out_specs=[tok, st],
        out_shape=[jax.ShapeDtypeStruct((batch, seq_len, BRANCH_W), jnp.float32),
                   jax.ShapeDtypeStruct((batch, DN_HEADS, LANE, LANE), jnp.float32)],
        scratch_shapes=[pltpu.VMEM((batch, DN_HEADS, LANE, LANE), jnp.float32)],
        compiler_params=pltpu.CompilerParams(dimension_semantics=("arbitrary",), vmem_limit_bytes=VMEM_LIMIT),
        name="gdn_scan",
    )(n3(u), n3(w), n3(qg), n3(kg), n3(qk), eg.reshape(batch, seq_len // SCAN_C, BRANCH_W), n3(c_all),
      norm_w.reshape(1, LANE), s0.astype(jnp.float32))
    return o.reshape(batch * seq_len, BRANCH_W), s


def _sc_branch_kernel(x_ref, halo_ref, buf_ref, w_ref, o_ref, *, tiles_per_seq):
    first = (pl.program_id(0) % tiles_per_seq) == 0
    x = x_ref[...]
    hr = halo_ref[...]
    halo = jnp.where(first, buf_ref[0], hr[:, SC_W:2 * SC_W] * hr[:, 2 * SC_W:3 * SC_W])
    o_ref[...] = x[:, 0:SC_W] * _conv_tile(x[:, SC_W:2 * SC_W] * x[:, 2 * SC_W:3 * SC_W], halo, w_ref)


def sc_branch(c_all, col_x, conv_buf, conv_w, seq_len):
    m = c_all.shape[0]
    tt = SCAN_TT
    tps = seq_len // tt
    xw = 3 * SC_W
    return pl.pallas_call(
        functools.partial(_sc_branch_kernel, tiles_per_seq=tps),
        grid=(m // tt,),
        in_specs=[pl.BlockSpec((tt, xw), lambda i: (i, col_x // xw)),
                  _halo_spec(tt, xw, col_x // xw),
                  pl.BlockSpec((1, HALO, SC_W), lambda i: (i // tps, 0, 0)),
                  pl.BlockSpec((SC_CONV, SC_W), lambda i: (0, 0))],
        out_specs=pl.BlockSpec((tt, SC_W), lambda i: (i, 0)),
        out_shape=jax.ShapeDtypeStruct((m, SC_W), jnp.float32),
        compiler_params=pltpu.CompilerParams(dimension_semantics=("arbitrary",), vmem_limit_bytes=VMEM_LIMIT),
        name="sc_branch",
    )(c_all, c_all, _pad_buf(conv_buf), conv_w.astype(jnp.float32))


def rms_norm(x, g):
    xf = x.astype(jnp.float32)
    y = xf * lax.rsqrt(jnp.mean(xf * xf, axis=-1, keepdims=True) + EPS)
    return (y * g.astype(jnp.float32)).astype(x.dtype)


def l2_normalize(x):
    xf = x.astype(jnp.float32)
    return xf * lax.rsqrt(jnp.sum(xf * xf, axis=-1, keepdims=True) + EPS)


def causal_conv(x, buf, w):
    k_w = w.shape[0]
    seq_len = x.shape[1]
    xp = jnp.concatenate([buf.astype(x.dtype), x], axis=1)
    y = sum(xp[:, j:j + seq_len] * w[j] for j in range(k_w))
    return y, xp[:, seq_len:]


def to_chunks(a, c):
    b, seq_len = a.shape[:2]
    n = -(-seq_len // c)
    a = jnp.pad(a, [(0, 0), (0, n * c - seq_len)] + [(0, 0)] * (a.ndim - 2))
    return jnp.moveaxis(a.reshape((b, n, c) + a.shape[2:]), 1, 0)


def from_chunks(a, seq_len):
    n, b, c = a.shape[:3]
    return jnp.moveaxis(a, 0, 1).reshape((b, n * c) + a.shape[3:])[:, :seq_len]


def hgrn2_scan(q, k, v, logf, s0):
    seq_len = q.shape[1]
    c = min(HG_CHUNK, seq_len)
    tri = jnp.tril(jnp.ones((c, c), bool))[None, :, :, None, None]

    def step(s, inp):
        qc, kc, vc, gc = inp
        b = jnp.cumsum(gc, axis=1)
        decay = jnp.exp(jnp.where(tri, b[:, :, None] - b[:, None], -jnp.inf))
        att = jnp.einsum('bthk,btshk->btsh', qc, decay * kc[:, None])
        o = jnp.einsum('btsh,bshv->bthv', att, vc) + jnp.einsum('bthk,bhkv->bthv', qc * jnp.exp(b), s)
        b_last = b[:, -1]
        s = jnp.exp(b_last)[..., None] * s + jnp.einsum('bshk,bshv->bhkv', kc * jnp.exp(b_last[:, None] - b), vc)
        return s, o

    xs = tuple(to_chunks(a.astype(jnp.float32), c) for a in (q, k, v, logf))
    s, o = lax.scan(step, s0.astype(jnp.float32), xs)
    return from_chunks(o, seq_len), s


def gated_delta_scan(q, k, v, beta, g, s0):
    seq_len = q.shape[1]
    c = min(DN_CHUNK, seq_len)
    incl = jnp.tril(jnp.ones((c, c), bool))
    strict = jnp.tril(jnp.ones((c, c), bool), -1)
    eye = jnp.eye(c, dtype=jnp.float32)

    def step(s, inp):
        qc, kc, vc, bc, gc = inp
        qh, kh, vh = (jnp.swapaxes(a, 1, 2) for a in (qc, kc, vc))
        bh = jnp.swapaxes(bc, 1, 2)
        gam = jnp.cumsum(jnp.swapaxes(gc, 1, 2), axis=-1)
        decay = jnp.exp(jnp.where(incl, gam[..., :, None] - gam[..., None, :], -jnp.inf))
        kk = jnp.einsum('bhtk,bhsk->bhts', kh, kh)
        t_mat = eye + jnp.where(strict, bh[..., :, None] * kk * decay, 0.0)
        u = lax.linalg.triangular_solve(t_mat, vh * bh[..., None], left_side=True, lower=True)
        w = lax.linalg.triangular_solve(t_mat, kh * (bh * jnp.exp(gam))[..., None], left_side=True, lower=True)
        v_new = u - jnp.einsum('bhtk,bhkv->bhtv', w, s)
        qk = jnp.einsum('bhtk,bhsk->bhts', qh, kh) * decay
        o = jnp.einsum('bhtk,bhkv->bhtv', qh * jnp.exp(gam)[..., None], s) + jnp.einsum('bhts,bhsv->bhtv', qk, v_new)
        g_last = gam[..., -1]
        s = jnp.exp(g_last)[..., None, None] * s + jnp.einsum('bhtk,bhtv->bhkv', kh * jnp.exp(g_last[..., None] - gam)[..., None], v_new)
        return s, jnp.swapaxes(o, 1, 2)

    xs = tuple(to_chunks(a.astype(jnp.float32), c) for a in (q, k, v, beta, g))
    s, o = lax.scan(step, s0.astype(jnp.float32), xs)
    return from_chunks(o, seq_len), s


def trunk_layer(x, start, hg_s, dn_s, dn_buf, sc_buf, past_rows, win_buf, w):
    b, seq_len, _ = x.shape
    f32 = jnp.float32
    x2 = x.reshape(b * seq_len, D_MODEL)
    m = b * seq_len
    c2 = _norm_proj(x2, w["norm_mix"], w["w_in"])
    c_all = c2.reshape(b, seq_len, N_IN_PAD)
    lb = w["lb"]
    if past_rows is None:
        oi, qb, kb, eb = hgrn_prep(c2, C_OFF["hg_q"], C_OFF["hg_f"], C_OFF["hg_i"], lb)
        o_a, hg_s = hgrn_scan(oi, qb, kb, eb, c2, C_OFF["hg_i"], C_OFF["hg_g"], w["hg_norm"], hg_s, b, seq_len)
        u, wy, qg, kg, qk, eg = gdn_prep(c2, C_OFF["dn_qkv"], C_OFF["misc"], dn_buf, w["dn_conv"], w["dn_a_log"],
                                         w["dn_dt_bias"], b, seq_len)
        o_b, dn_s = gdn_scan(u, wy, qg, kg, qk, eg, c2, C_OFF["dn_z"], w["dn_norm"], dn_s, b, seq_len)
        o_c = sc_branch(c2, C_OFF["sc_bch"], sc_buf, w["sc_conv"], seq_len)
        tail = c_all[:, seq_len - (DN_CONV - 1):]
        dn_buf = tail[..., C_OFF["dn_qkv"]:C_OFF["dn_qkv"] + 3 * BRANCH_W]
        sc_t = tail[:, DN_CONV - SC_CONV:, C_OFF["sc_bch"] + SC_W:C_OFF["sc_bch"] + 3 * SC_W]
        sc_buf = sc_t[..., :SC_W] * sc_t[..., SC_W:]
    else:
        o_a, o_b, o_c, (hg_s, dn_s, dn_buf, sc_buf) = cached_mixers(x, c_all, hg_s, dn_s, dn_buf, sc_buf, w)
    tabs = rope_tables(start + jnp.arange(seq_len))
    rows, win_rows, kvb = nsa_prep(c2, C_OFF["nsa_kv"], tabs, seq_len)
    rows = rows.reshape(b, seq_len, 4 * LANE)
    win_rows = win_rows.reshape(b, seq_len, 2 * LANE)
    kvb = kvb.reshape(b, seq_len, 4 * LANE)
    nsa_cols = (c2, C_OFF["nsa_q"], C_OFF["misc"], tabs)
    cmp_w = (w["cmp_pos"], w["cmp_w1"], w["cmp_w2"])
    if past_rows is None:
        ckv = nsa_compress(rows, *cmp_w)
        no_new = jnp.zeros((b, LANE, 2 * LANE), MXU_DT)
        o_d = nsa_attend(*nsa_cols, ckv, kvb, (0, 1), no_new, 0, kvb, (2, 3), 0, b, seq_len, 0)
        new_win = win_rows[:, max(seq_len - WINDOW, 0):]
    else:
        n_past = past_rows.shape[1]
        assert n_past % CMP_STRIDE == 0 and seq_len < CMP_STRIDE
        past = past_rows.reshape(b, n_past, 4 * LANE)
        ckv = nsa_compress(past, *cmp_w)
        new_kv = jnp.pad(kvb[:, :, 0:2 * LANE], ((0, 0), (0, LANE - seq_len), (0, 0)))
        nb = win_buf.shape[1]
        w_all = jnp.concatenate([win_buf.reshape(b, nb, 2 * LANE).astype(jnp.float32), win_rows], axis=1)
        w_pad = jnp.pad(w_all, ((0, 0), (0, max(NSA_WSPAN - nb - seq_len, 0)), (0, 0)))
        o_d = nsa_attend(*nsa_cols, ckv, past, (2, 3), new_kv, seq_len, w_pad, (0, 1), start - nb, b, seq_len, start)
        new_win = w_all[:, seq_len:]
    rows = rows.reshape(b, seq_len, 4, NSA_KV_HEADS, NSA_HD)
    new_win = new_win.reshape(b, new_win.shape[1], 2, NSA_KV_HEADS, NSA_HD)
    branches = [o_a.reshape(m, BRANCH_W), o_b.reshape(m, BRANCH_W), o_c.reshape(m, BRANCH_W), o_d.reshape(m, BRANCH_W)]
    x2 = _merge_out(x2, branches, c2, C_OFF["merge_gate"], w["w_branch"], w["w_out"])
    x2 = _mlp(x2, w["norm_mlp"], w["w_up"], w["w_down"])
    return x2.reshape(b, seq_len, D_MODEL), (hg_s, dn_s, dn_buf, sc_buf, new_win, rows)


def cached_mixers(x, c_all, hg_s, dn_s, dn_buf, sc_buf, w):
    b, seq_len, _ = x.shape
    f32 = jnp.float32
    c = {n: c_all[..., C_OFF[n]:C_OFF[n] + wd] for n, wd in C_ORDER}
    for n, o0, wd in MISC_COLS:
        c[n] = c["misc"][..., o0:o0 + wd]
    lb = w["lb"]
    hq = jax.nn.silu(c["hg_q"]).reshape(b, seq_len, HG_HEADS, HG_DK)
    z = c["hg_f"].astype(f32).reshape(b, seq_len, HG_HEADS, HG_DK)
    logf = jnp.logaddexp(jnp.log(lb), jnp.log1p(-lb) + jax.nn.log_sigmoid(z))
    hk = (1.0 - lb) * jax.nn.sigmoid(-z)
    hv = c["hg_i"].reshape(b, seq_len, HG_HEADS, HG_DV)
    o_a, hg_s = hgrn2_scan(hq, hk, hv, logf, hg_s)
    o_a = rms_norm(o_a.astype(x.dtype), w["hg_norm"]) * jax.nn.silu(c["hg_g"].reshape(b, seq_len, HG_HEADS, HG_DV))
    qkv, dn_buf = causal_conv(c["dn_qkv"], dn_buf, w["dn_conv"])
    dq, dk, dv = jnp.split(jax.nn.silu(qkv), 3, axis=-1)
    dq = l2_normalize(dq.reshape(b, seq_len, DN_HEADS, DN_DK)) * DN_DK ** -0.5
    dk = l2_normalize(dk.reshape(b, seq_len, DN_HEADS, DN_DK))
    beta = jax.nn.sigmoid(c["dn_b"].astype(f32))
    g = -jnp.exp(w["dn_a_log"].astype(f32)) * jax.nn.softplus(c["dn_a"].astype(f32) + w["dn_dt_bias"])
    o_b, dn_s = gated_delta_scan(dq, dk, dv.reshape(b, seq_len, DN_HEADS, DN_DV), beta, g, dn_s)
    o_b = rms_norm(o_b.astype(x.dtype), w["dn_norm"]) * jax.nn.silu(c["dn_z"].reshape(b, seq_len, DN_HEADS, DN_DV))
    gb, gc, hx = jnp.split(c["sc_bch"], 3, axis=-1)
    conv, sc_buf = causal_conv(gc * hx, sc_buf, w["sc_conv"])
    o_c = gb * conv
    return o_a, o_b, o_c, (hg_s, dn_s, dn_buf, sc_buf)


def kernel(x_prompt, x_sample, state_hgrn, state_dn, state_dn_conv, state_sc_conv, state_win_kv, cache_kv, page_table, norm_mix, norm_mlp, norm_final, w_in, hg_lb_logits, hg_norm, dn_conv, dn_a_log, dn_dt_bias, dn_norm, sc_conv, cmp_pos, cmp_w1, cmp_w2, w_branch, w_out, w_up, w_down):
    f32 = jnp.float32
    bf16 = jnp.bfloat16
    lbs = jnp.cumsum(jax.nn.softmax(hg_lb_logits.astype(f32), axis=0), axis=0)
    lbs = lbs - lbs[:1]
    w_in_b = _permute_w_in(w_in).astype(bf16)
    w_branch_b, w_out_b, w_up_b, w_down_b = (a.astype(bf16) for a in (w_branch, w_out, w_up, w_down))

    def layer_w(l):
        return dict(norm_mix=norm_mix[l], norm_mlp=norm_mlp[l], w_in=w_in_b[l], lb=lbs[l], hg_norm=hg_norm[l],
                    dn_conv=dn_conv[l], dn_a_log=dn_a_log[l], dn_dt_bias=dn_dt_bias[l], dn_norm=dn_norm[l],
                    sc_conv=sc_conv[l], cmp_pos=cmp_pos[l], cmp_w1=cmp_w1[l], cmp_w2=cmp_w2[l],
                    w_branch=w_branch_b[l], w_out=w_out_b[l], w_up=w_up_b[l], w_down=w_down_b[l])

    bp = x_prompt.shape[0]
    yp = x_prompt
    p_st = []
    for l in range(DEPTH):
        yp, st = trunk_layer(yp, 0,
                             jnp.zeros((bp, HG_HEADS, HG_DK, HG_DV), f32),
                             jnp.zeros((bp, DN_HEADS, DN_DK, DN_DV), f32),
                             jnp.zeros((bp, DN_CONV - 1, 3 * BRANCH_W), x_prompt.dtype),
                             jnp.zeros((bp, SC_CONV - 1, SC_W), x_prompt.dtype),
                             None, None, layer_w(l))
        p_st.append(st)
    ys = x_sample
    s_st = []
    for l in range(DEPTH):
        past = page_rows(cache_kv, page_table, l)
        ys, st = trunk_layer(ys, past.shape[1], state_hgrn[l], state_dn[l], state_dn_conv[l], state_sc_conv[l],
                             past, state_win_kv[l], layer_w(l))
        s_st.append(st)
    p = [jnp.stack([st[i] for st in p_st]) for i in range(6)]
    s = [jnp.stack([st[i] for st in s_st]) for i in range(6)]
    return (rms_norm(yp, norm_final), rms_norm(ys, norm_final),
            p[0], p[1], p[2], p[3], p[4], p[5],
            s[0], s[1], s[2], s[3], s[4], s[5])
```

```python
import math, functools
import jax, jax.numpy as jnp
from jax import lax
import numpy as np
from jax.experimental import pallas as pl
from jax.experimental.pallas import tpu as pltpu

D_MODEL = 1024
DEPTH = 4
PAGE_SIZE = 128
N_BRANCH = 4
BRANCH_W = D_MODEL // 2
HG_HEADS = 4
HG_DK = BRANCH_W // HG_HEADS
HG_DV = BRANCH_W // HG_HEADS
HG_CHUNK = 64
DN_HEADS = 4
DN_DK = BRANCH_W // DN_HEADS
DN_DV = BRANCH_W // DN_HEADS
DN_CONV = 4
DN_CHUNK = 64
SC_W = BRANCH_W
SC_CONV = 3
NSA_HEADS = 8
NSA_KV_HEADS = 2
NSA_HD = BRANCH_W // NSA_HEADS
NSA_GROUP = NSA_HEADS // NSA_KV_HEADS
ROPE_DIM = NSA_HD // 4
ROPE_THETA = 500000.0
CMP_BLOCK = 32
CMP_STRIDE = 16
CMP_HIDDEN = 4 * NSA_HD
SEL_BLOCK = 64
SEL_TOPN = 16
WINDOW = 512
Q_BLOCK = 128
D_FF = 4 * D_MODEL
EPS = 1e-6

IN_SPLITS = (
    ("hg_q", BRANCH_W), ("hg_f", BRANCH_W), ("hg_i", BRANCH_W), ("hg_g", BRANCH_W),
    ("dn_qkv", 3 * BRANCH_W), ("dn_b", DN_HEADS), ("dn_a", DN_HEADS), ("dn_z", BRANCH_W),
    ("sc_bch", 3 * SC_W),
    ("nsa_q", NSA_HEADS * NSA_HD), ("nsa_kv", 6 * NSA_KV_HEADS * NSA_HD), ("nsa_gate", 3 * NSA_HEADS),
    ("merge_gate", N_BRANCH * D_MODEL),
)
IN_NAMES = tuple(n for n, _ in IN_SPLITS)
IN_CUTS = tuple(int(c) for c in np.cumsum([s for _, s in IN_SPLITS])[:-1])
N_IN = sum(s for _, s in IN_SPLITS)

LANE = 128
PROJ_TM = 1024
PROJ_TN = 1024
VMEM_LIMIT = 48 * 1024 * 1024
MXU_DT = jnp.bfloat16

MISC_W = 2 * LANE
C_ORDER = (("merge_gate", N_BRANCH * D_MODEL), ("hg_q", BRANCH_W), ("hg_f", BRANCH_W), ("hg_i", BRANCH_W),
           ("hg_g", BRANCH_W), ("dn_qkv", 3 * BRANCH_W), ("sc_bch", 3 * SC_W), ("dn_z", BRANCH_W),
           ("nsa_q", NSA_HEADS * NSA_HD), ("nsa_kv", 6 * NSA_KV_HEADS * NSA_HD), ("misc", MISC_W))
C_OFF = {}
_o = 0
for _n, _w in C_ORDER:
    C_OFF[_n] = _o
    _o += _w
N_IN_PAD = _o
assert N_IN_PAD % PROJ_TN == 0
MISC_COLS = (("dn_b", 0, DN_HEADS), ("dn_a", DN_HEADS, DN_HEADS), ("nsa_gate", 2 * DN_HEADS, 3 * NSA_HEADS))
NEG_INF = float("-inf")


def _permute_w_in(w_in):
    src = dict(zip(IN_NAMES, jnp.split(w_in, IN_CUTS, axis=-1)))
    misc = jnp.concatenate([src[n] for n, _, _ in MISC_COLS], axis=-1)
    src["misc"] = jnp.pad(misc, ((0, 0), (0, 0), (0, MISC_W - misc.shape[-1])))
    return jnp.concatenate([src[n] for n, _ in C_ORDER], axis=-1)


def _row_tile(m):
    return 512 if m % 512 == 0 else m


def _norm_proj_kernel(x_ref, g_ref, w_ref, o_ref, h_ref):
    @pl.when(pl.program_id(1) == 0)
    def _():
        x = x_ref[...]
        y = x * lax.rsqrt(jnp.mean(x * x, axis=-1, keepdims=True) + EPS)
        h_ref[...] = (y * g_ref[...]).astype(jnp.bfloat16)

    o_ref[...] = jnp.dot(h_ref[...], w_ref[...], preferred_element_type=jnp.float32)


def _norm_proj(x, g, w_bf16):
    m, d = x.shape
    n = w_bf16.shape[1]
    tm = PROJ_TM if m % PROJ_TM == 0 else m
    return pl.pallas_call(
        _norm_proj_kernel,
        grid=(m // tm, n // PROJ_TN),
        in_specs=[pl.BlockSpec((tm, d), lambda i, j: (i, 0)),
                  pl.BlockSpec((1, d), lambda i, j: (0, 0)),
                  pl.BlockSpec((d, PROJ_TN), lambda i, j: (0, j))],
        out_specs=pl.BlockSpec((tm, PROJ_TN), lambda i, j: (i, j)),
        out_shape=jax.ShapeDtypeStruct((m, n), jnp.float32),
        scratch_shapes=[pltpu.VMEM((tm, d), jnp.bfloat16)],
        compiler_params=pltpu.CompilerParams(dimension_semantics=("arbitrary", "arbitrary"),
                                             vmem_limit_bytes=VMEM_LIMIT),
        name="norm_proj",
    )(x, g.reshape(1, d), w_bf16)


def _merge_kernel(x_ref, ba_ref, bb_ref, bc_ref, bd_ref, gate_ref, wb_ref, wo_ref, o_ref):
    acc = None
    for n, b_ref in enumerate((ba_ref, bb_ref, bc_ref, bd_ref)):
        p = jnp.dot(b_ref[...].astype(jnp.bfloat16), wb_ref[n], preferred_element_type=jnp.float32)
        t = jax.nn.sigmoid(gate_ref[:, n * D_MODEL:(n + 1) * D_MODEL]) * p
        acc = t if acc is None else acc + t
    o_ref[...] = x_ref[...] + jnp.dot(acc.astype(jnp.bfloat16), wo_ref[...], preferred_element_type=jnp.float32)


def _merge_out(x, branches, c_all, gate_col, wb_bf16, wo_bf16):
    m, d = x.shape
    tm = 256 if m % 256 == 0 else m
    row = lambda w: pl.BlockSpec((tm, w), lambda i: (i, 0))
    return pl.pallas_call(
        _merge_kernel,
        grid=(m // tm,),
        in_specs=[row(d)] + [row(BRANCH_W)] * N_BRANCH + [
                  pl.BlockSpec((tm, N_BRANCH * d), lambda i: (i, gate_col // (N_BRANCH * d))),
                  pl.BlockSpec((N_BRANCH, BRANCH_W, d), lambda i: (0, 0, 0)),
                  pl.BlockSpec((d, d), lambda i: (0, 0))],
        out_specs=row(d),
        out_shape=jax.ShapeDtypeStruct((m, d), jnp.float32),
        compiler_params=pltpu.CompilerParams(dimension_semantics=("arbitrary",),
                                             vmem_limit_bytes=VMEM_LIMIT),
        name="merge_out",
    )(x, *branches, c_all, wb_bf16, wo_bf16)


MLP_TF = 1024


def _mlp_kernel(x_ref, g_ref, wu_ref, wd_ref, o_ref, h_ref, acc_ref):
    j = pl.program_id(1)

    @pl.when(j == 0)
    def _():
        x = x_ref[...]
        y = x * lax.rsqrt(jnp.mean(x * x, axis=-1, keepdims=True) + EPS)
        h_ref[...] = (y * g_ref[...]).astype(jnp.bfloat16)
        acc_ref[...] = jnp.zeros_like(acc_ref)

    u = jnp.maximum(jnp.dot(h_ref[...], wu_ref[...], preferred_element_type=jnp.float32), 0.0)
    acc_ref[...] += jnp.dot((u * u).astype(jnp.bfloat16), wd_ref[...], preferred_element_type=jnp.float32)

    @pl.when(j == pl.num_programs(1) - 1)
    def _():
        o_ref[...] = x_ref[...] + acc_ref[...]


def _mlp(x, g, wu_bf16, wd_bf16):
    m, d = x.shape
    f = wu_bf16.shape[1]
    tm = _row_tile(m)
    return pl.pallas_call(
        _mlp_kernel,
        grid=(m // tm, f // MLP_TF),
        in_specs=[pl.BlockSpec((tm, d), lambda i, j: (i, 0)),
                  pl.BlockSpec((1, d), lambda i, j: (0, 0)),
                  pl.BlockSpec((d, MLP_TF), lambda i, j: (0, j)),
                  pl.BlockSpec((MLP_TF, d), lambda i, j: (j, 0))],
        out_specs=pl.BlockSpec((tm, d), lambda i, j: (i, 0)),
        out_shape=jax.ShapeDtypeStruct((m, d), jnp.float32),
        scratch_shapes=[pltpu.VMEM((tm, d), jnp.bfloat16), pltpu.VMEM((tm, d), jnp.float32)],
        compiler_params=pltpu.CompilerParams(dimension_semantics=("arbitrary", "arbitrary"),
                                             vmem_limit_bytes=VMEM_LIMIT),
        name="mlp",
    )(x, g.reshape(1, d), wu_bf16, wd_bf16)


NSA_TQ = 128
NSA_TK = 512
NSA_ROWS = NSA_HEADS * NSA_TQ
NSA_WSPAN = WINDOW + NSA_TQ


def _dot(a, b, **kw):
    return jnp.dot(a, b, preferred_element_type=jnp.float32, **kw)


def _dot_nt(a, b):
    return lax.dot_general(a, b, (((1,), (1,)), ((), ())), preferred_element_type=jnp.float32)


def rope_tables(pos):
    half = ROPE_DIM // 2
    inv_freq = ROPE_THETA ** (-jnp.arange(half, dtype=jnp.float32) / half)
    ang = pos.astype(jnp.float32)[:, None] * inv_freq
    cos, sin = jnp.cos(ang), jnp.sin(ang)
    n = pos.shape[0]
    one = jnp.ones((n, NSA_HD - ROPE_DIM), jnp.float32)
    zero = jnp.zeros((n, NSA_HD - ROPE_DIM), jnp.float32)
    z8 = jnp.zeros((n, half), jnp.float32)
    c = jnp.concatenate([cos, cos, one], axis=1)
    s1 = jnp.concatenate([-sin, z8, zero], axis=1)
    s2 = jnp.concatenate([z8, sin, zero], axis=1)
    two = lambda a: jnp.concatenate([a, a], axis=1)
    return two(c), two(s1), two(s2)


def _rope(x, c, s1, s2):
    n = x.shape[-1]
    return x * c + pltpu.roll(x, n - ROPE_DIM // 2, 1) * s1 + pltpu.roll(x, ROPE_DIM // 2, 1) * s2


def _nsa_prep_kernel(kv0_ref, kv1_ref, kv2_ref, c_ref, s1_ref, s2_ref, rows_ref, win_ref, kvb_ref):
    c, s1, s2 = c_ref[...], s1_ref[...], s2_ref[...]
    cmp_kv = kv0_ref[...]
    sel = kv1_ref[...]
    wnd = kv2_ref[...]
    ks = _rope(sel[:, :LANE], c, s1, s2)
    kw = _rope(wnd[:, :LANE], c, s1, s2)
    rows_ref[:, 0:2 * LANE] = cmp_kv
    rows_ref[:, 2 * LANE:3 * LANE] = ks
    rows_ref[:, 3 * LANE:4 * LANE] = sel[:, LANE:]
    win_ref[:, 0:LANE] = kw
    win_ref[:, LANE:2 * LANE] = wnd[:, LANE:]
    kvb_ref[:, 0:LANE] = ks.astype(MXU_DT)
    kvb_ref[:, LANE:2 * LANE] = sel[:, LANE:].astype(MXU_DT)
    kvb_ref[:, 2 * LANE:3 * LANE] = kw.astype(MXU_DT)
    kvb_ref[:, 3 * LANE:4 * LANE] = wnd[:, LANE:].astype(MXU_DT)


def nsa_prep(c_all, kv_col0, tabs, seq_len):
    m = c_all.shape[0]
    tm = 512 if seq_len % 512 == 0 else seq_len
    nlt = seq_len // tm
    cb = kv_col0 // (2 * LANE)
    kv_spec = lambda k: pl.BlockSpec((tm, 2 * LANE), lambda i, k=k: (i, cb + k))
    tab_spec = pl.BlockSpec((tm, LANE), lambda i: (i % nlt, 0))
    return pl.pallas_call(
        _nsa_prep_kernel,
        grid=(m // tm,),
        in_specs=[kv_spec(0), kv_spec(1), kv_spec(2), tab_spec, tab_spec, tab_spec],
        out_specs=[pl.BlockSpec((tm, 4 * LANE), lambda i: (i, 0)),
                   pl.BlockSpec((tm, 2 * LANE), lambda i: (i, 0)),
                   pl.BlockSpec((tm, 4 * LANE), lambda i: (i, 0))],
        out_shape=[jax.ShapeDtypeStruct((m, 4 * LANE), jnp.float32),
                   jax.ShapeDtypeStruct((m, 2 * LANE), jnp.float32),
                   jax.ShapeDtypeStruct((m, 4 * LANE), MXU_DT)],
        compiler_params=pltpu.CompilerParams(dimension_semantics=("arbitrary",), vmem_limit_bytes=VMEM_LIMIT),
        name="nsa_prep",
    )(c_all, c_all, c_all, *tabs)


def _nsa_compress_kernel(x_ref, pos_ref, w1_ref, w2_ref, o_ref):
    nh = x_ref.shape[0] // CMP_STRIDE
    top = jnp.zeros((nh, NSA_KV_HEADS * CMP_HIDDEN), jnp.float32)
    bot = jnp.zeros((nh, NSA_KV_HEADS * CMP_HIDDEN), jnp.float32)
    for t in range(CMP_STRIDE):
        x = x_ref[pl.ds(t, nh, stride=CMP_STRIDE), :]
        top = top + _dot((x + pos_ref[0, t:t + 1, :]).astype(MXU_DT), w1_ref[0, 0, t])
        bot = bot + _dot((x + pos_ref[0, CMP_STRIDE + t:CMP_STRIDE + t + 1, :]).astype(MXU_DT), w1_ref[0, 1, t])
    h = top + pltpu.roll(bot, nh - 1, 0)
    h = h * jax.nn.sigmoid(h)
    o_ref[0, 0] = _dot(h.astype(MXU_DT), w2_ref[0]).astype(o_ref.dtype)


def _block_diag2(w):
    z = jnp.zeros_like(w)
    return jnp.concatenate([jnp.concatenate([w, z], axis=-1), jnp.concatenate([z, w], axis=-1)], axis=-2)


def nsa_compress(seq_rows, cmp_pos, cmp_w1, cmp_w2):
    b, seq_len = seq_rows.shape[:2]
    nh = seq_len // CMP_STRIDE
    pos = jnp.concatenate([cmp_pos, cmp_pos], axis=-1)
    w1 = cmp_w1.reshape(2, 2, CMP_STRIDE, NSA_HD, CMP_HIDDEN)
    w1 = _block_diag2(w1).astype(MXU_DT)
    w2 = _block_diag2(cmp_w2).astype(MXU_DT)
    return pl.pallas_call(
        _nsa_compress_kernel,
        grid=(b, 2),
        in_specs=[pl.BlockSpec((None, seq_len, LANE), lambda i, j: (i, 0, j)),
                  pl.BlockSpec((1, CMP_BLOCK, LANE), lambda i, j: (j, 0, 0)),
                  pl.BlockSpec((1, 2, CMP_STRIDE, LANE, NSA_KV_HEADS * CMP_HIDDEN), lambda i, j: (j, 0, 0, 0, 0)),
                  pl.BlockSpec((1, NSA_KV_HEADS * CMP_HIDDEN, LANE), lambda i, j: (j, 0, 0))],
        out_specs=pl.BlockSpec((1, 1, nh, LANE), lambda i, j: (i, j, 0, 0)),
        out_shape=jax.ShapeDtypeStruct((b, 2, nh, LANE), MXU_DT),
        compiler_params=pltpu.CompilerParams(dimension_semantics=("arbitrary", "arbitrary"), vmem_limit_bytes=VMEM_LIMIT),
        name="nsa_compress",
    )(seq_rows, pos, w1, w2)


def _pad_heads(q):
    lane = lax.broadcasted_iota(jnp.int32, (q.shape[0], LANE), 1)
    blocks = []
    for h in range(NSA_HEADS):
        blk = q[:, (h // 2) * LANE:(h // 2 + 1) * LANE]
        g = h // NSA_GROUP
        if h % 2 != g:
            blk = pltpu.roll(blk, NSA_HD, 1)
        keep = (lane < NSA_HD) if g == 0 else (lane >= NSA_HD)
        blocks.append(jnp.where(keep, blk, 0.0))
    return jnp.concatenate(blocks, axis=0)


def _softmax_rows(s):
    m = jnp.max(s, axis=-1, keepdims=True)
    m = jnp.where(m == NEG_INF, 0.0, m)
    p = jnp.exp(s - m)
    return p / jnp.maximum(jnp.sum(p, axis=-1, keepdims=True), 1e-30)


def _nsa_attn_kernel(q_ref, gate_ref, c_ref, s1_ref, s2_ref, ck_ref, cv_ref, ks_ref, vs_ref, kn_ref, kw_ref, vw_ref,
                     o_ref, m_ref, l_ref, acc_ref, s_ref, *, q_start, win_start, ns, n_new):
    tq = q_ref.shape[1]
    n_rows = NSA_HEADS * tq
    n_keys = ks_ref.shape[1]
    i = pl.program_id(1)
    q0 = q_start + i * tq
    scale = NSA_HD ** -0.5
    q = q_ref[0] * scale
    c = jnp.concatenate([c_ref[...]] * 4, axis=1)
    s1 = jnp.concatenate([s1_ref[...]] * 4, axis=1)
    s2 = jnp.concatenate([s2_ref[...]] * 4, axis=1)
    q_raw = _pad_heads(q).astype(MXU_DT)
    q_rot = _pad_heads(_rope(q, c, s1, s2)).astype(MXU_DT)

    row = lax.broadcasted_iota(jnp.int32, (n_rows, 1), 0)
    qpos = q0 + (row & (tq - 1))

    nc = ck_ref.shape[1]
    s_c = _dot_nt(q_raw, ck_ref[0])
    c_end = lax.broadcasted_iota(jnp.int32, (1, nc), 1) * CMP_STRIDE + (CMP_BLOCK - 1)
    p_c = _softmax_rows(jnp.where(c_end <= qpos, s_c, NEG_INF))
    o_c = _dot(p_c.astype(MXU_DT), cv_ref[0])

    psum = jnp.concatenate(
        [sum(p_c[(g * NSA_GROUP + j) * tq:(g * NSA_GROUP + j + 1) * tq] for j in range(NSA_GROUP))
         for g in range(NSA_KV_HEADS)], axis=0)
    n_i = lax.broadcasted_iota(jnp.int32, (nc, ns), 0) * CMP_STRIDE
    m_i = lax.broadcasted_iota(jnp.int32, (nc, ns), 1) * SEL_BLOCK
    overlap = ((n_i <= m_i + (SEL_BLOCK - 1)) & (n_i + (CMP_BLOCK - 1) >= m_i)).astype(jnp.float32)
    imp = _dot(psum, overlap, precision=lax.Precision.HIGHEST)
    r2 = lax.broadcasted_iota(jnp.int32, (NSA_KV_HEADS * tq, 1), 0)
    qpos2 = q0 + (r2 & (tq - 1))
    cur = qpos2 >> 6
    blk = lax.broadcasted_iota(jnp.int32, (1, ns), 1)
    forced = (blk == 0) | (blk == cur) | (blk == cur - 1)
    valid = blk * SEL_BLOCK <= qpos2
    v = jnp.where(forced, jnp.inf, jnp.where(valid, imp, NEG_INF))
    blk_f = blk.astype(jnp.float32)
    sel = jnp.zeros(v.shape, jnp.float32)
    for _ in range(SEL_TOPN):
        mx = jnp.max(v, axis=-1, keepdims=True)
        first = jnp.min(jnp.where(v == mx, blk_f, float(ns)), axis=-1, keepdims=True)
        pick = blk_f == first
        sel = jnp.where(pick, 1.0, sel)
        v = jnp.where(pick, NEG_INF, v)
    sel_b = sel.astype(MXU_DT)

    m_ref[...] = jnp.full(m_ref.shape, NEG_INF, jnp.float32)
    l_ref[...] = jnp.zeros(l_ref.shape, jnp.float32)
    acc_ref[...] = jnp.zeros(acc_ref.shape, jnp.float32)
    n_kt = jnp.minimum((q0 + tq + NSA_TK - 1) // NSA_TK, n_keys // NSA_TK)

    def scores(k):
        return _dot_nt(q_rot, k.astype(MXU_DT))

    def update(s, vv, tok0):
        nk = s.shape[1]
        tok = tok0 + lax.broadcasted_iota(jnp.int32, (1, nk), 1)
        e_m = lax.broadcasted_iota(jnp.int32, (ns, nk), 0)
        e_t = (tok0 + lax.broadcasted_iota(jnp.int32, (ns, nk), 1)) >> 6
        chosen = _dot(sel_b, (e_m == e_t).astype(MXU_DT))
        bias2 = jnp.where((chosen > 0.5) & (tok <= qpos2), 0.0, NEG_INF)
        bias = jnp.concatenate([bias2[0:tq]] * NSA_GROUP + [bias2[tq:2 * tq]] * NSA_GROUP, axis=0)
        s = s + bias
        m_old = m_ref[...]
        m_new = jnp.maximum(m_old, jnp.max(s, axis=-1, keepdims=True))
        m_safe = jnp.where(m_new == NEG_INF, 0.0, m_new)
        alpha = jnp.exp(m_old - m_safe)
        p = jnp.exp(s - m_safe)
        l_ref[...] = alpha * l_ref[...] + jnp.sum(p, axis=-1, keepdims=True)
        acc_ref[...] = alpha * acc_ref[...] + _dot(p.astype(MXU_DT), vv.astype(MXU_DT))
        m_ref[...] = m_new

    def body(kt, carry):
        koff = pl.multiple_of(kt * NSA_TK, NSA_TK)
        knext = pl.multiple_of(jnp.minimum(kt + 1, n_kt - 1) * NSA_TK, NSA_TK)
        s = s_ref[...]
        s_next = scores(ks_ref[0, pl.ds(knext, NSA_TK), :])
        update(s, vs_ref[0, pl.ds(koff, NSA_TK), :], koff)
        s_ref[...] = s_next
        return carry

    s_ref[...] = scores(ks_ref[0, 0:NSA_TK, :])
    lax.fori_loop(0, n_kt, body, 0)
    if n_new:
        update(scores(kn_ref[0, :, 0:LANE]), kn_ref[0, :, LANE:2 * LANE], n_keys)
    o_s = acc_ref[...] / jnp.maximum(l_ref[...], 1e-30)

    w0 = jnp.clip(q0 - win_start - WINDOW, 0, kw_ref.shape[1] - NSA_WSPAN)
    w0 = pl.multiple_of(w0, tq)
    kw = kw_ref[0, pl.ds(w0, NSA_WSPAN), :]
    vw = vw_ref[0, pl.ds(w0, NSA_WSPAN), :]
    s_w = _dot_nt(q_rot, kw.astype(MXU_DT))
    kpos = win_start + w0 + lax.broadcasted_iota(jnp.int32, (1, NSA_WSPAN), 1)
    w_ok = (kpos <= qpos) & (kpos > qpos - WINDOW)
    p_w = _softmax_rows(jnp.where(w_ok, s_w, NEG_INF))
    o_w = _dot(p_w.astype(MXU_DT), vw.astype(MXU_DT))

    gates = jax.nn.sigmoid(gate_ref[0])
    lane = lax.broadcasted_iota(jnp.int32, (tq, LANE), 1)
    outs = []
    for h in range(NSA_HEADS):
        sl = slice(h * tq, (h + 1) * tq)
        k0 = 2 * DN_HEADS + 3 * h
        o_h = (gates[:, k0:k0 + 1] * o_c[sl] + gates[:, k0 + 1:k0 + 2] * o_s[sl] + gates[:, k0 + 2:k0 + 3] * o_w[sl])
        if h % 2 != h // NSA_GROUP:
            o_h = pltpu.roll(o_h, NSA_HD, 1)
        outs.append(o_h)
    for p2 in range(NSA_HEADS // 2):
        o_ref[0, :, p2 * LANE:(p2 + 1) * LANE] = jnp.where(lane < NSA_HD, outs[2 * p2], outs[2 * p2 + 1])


def nsa_attend(c_all, q_col0, misc_col0, tabs, ckv, sel_kv, sel_cols, new_kv, n_new, win_kv, win_cols, win_start,
               batch, seq_len, q_start):
    tq = NSA_TQ if seq_len % NSA_TQ == 0 else seq_len
    n_rows = NSA_HEADS * tq
    qb = q_col0 // (4 * LANE)
    mb = misc_col0 // LANE
    c3 = c_all.reshape(batch, seq_len, c_all.shape[1])
    n_blocks = -(-(sel_kv.shape[1] + n_new) // SEL_BLOCK)
    ns = -(-n_blocks // LANE) * LANE
    tab_spec = pl.BlockSpec((tq, LANE), lambda b, i: (i, 0))
    col_spec = lambda a, k: pl.BlockSpec((1, a.shape[1], LANE), lambda b, i, k=k: (b, 0, k))
    nc = ckv.shape[2]
    out = pl.pallas_call(
        functools.partial(_nsa_attn_kernel, q_start=q_start, win_start=win_start, ns=ns, n_new=n_new),
        grid=(batch, seq_len // tq),
        in_specs=[pl.BlockSpec((1, tq, 4 * LANE), lambda b, i: (b, i, qb)),
                  pl.BlockSpec((1, tq, LANE), lambda b, i: (b, i, mb)),
                  tab_spec, tab_spec, tab_spec,
                  pl.BlockSpec((None, 1, nc, LANE), lambda b, i: (b, 0, 0, 0)),
                  pl.BlockSpec((None, 1, nc, LANE), lambda b, i: (b, 1, 0, 0)),
                  col_spec(sel_kv, sel_cols[0]), col_spec(sel_kv, sel_cols[1]),
                  pl.BlockSpec((1, new_kv.shape[1], 2 * LANE), lambda b, i: (b, 0, 0)),
                  col_spec(win_kv, win_cols[0]), col_spec(win_kv, win_cols[1])],
        out_specs=pl.BlockSpec((1, tq, 4 * LANE), lambda b, i: (b, i, 0)),
        out_shape=jax.ShapeDtypeStruct((batch, seq_len, 4 * LANE), jnp.float32),
        scratch_shapes=[pltpu.VMEM((n_rows, 1), jnp.float32), pltpu.VMEM((n_rows, 1), jnp.float32),
                        pltpu.VMEM((n_rows, LANE), jnp.float32), pltpu.VMEM((n_rows, NSA_TK), jnp.float32)],
        compiler_params=pltpu.CompilerParams(dimension_semantics=("arbitrary", "arbitrary"), vmem_limit_bytes=VMEM_LIMIT),
        name="nsa_attn",
    )(c3, c3, *tabs, ckv, ckv, sel_kv, sel_kv, new_kv, win_kv, win_kv)
    return out.reshape(batch * seq_len, 4 * LANE)


PAGES_PER_STEP = 4


def _page_rows_kernel(pt_ref, *refs):
    o_ref = refs[-1]
    for j, x_ref in enumerate(refs[:-1]):
        n_idx, n_g, hd, page = x_ref.shape
        for i in range(n_idx):
            o_ref[0, j * page:(j + 1) * page, i * n_g * hd:(i + 1) * n_g * hd] = x_ref[i].reshape(n_g * hd, page).T


def page_rows(cache_t, page_table, layer):
    _, _, n_idx, n_g, hd, page = cache_t.shape
    b, n_pages = page_table.shape
    w = n_idx * n_g * hd
    pps = PAGES_PER_STEP if n_pages % PAGES_PER_STEP == 0 else 1
    page_spec = lambda j: pl.BlockSpec((None, None, n_idx, n_g, hd, page),
                                       lambda i, p, pt, j=j: (layer, pt[i, p * pps + j], 0, 0, 0, 0))
    return pl.pallas_call(
        _page_rows_kernel,
        grid_spec=pltpu.PrefetchScalarGridSpec(
            num_scalar_prefetch=1,
            grid=(b, n_pages // pps),
            in_specs=[page_spec(j) for j in range(pps)],
            out_specs=pl.BlockSpec((1, pps * page, w), lambda i, p, pt: (i, p, 0))),
        out_shape=jax.ShapeDtypeStruct((b, n_pages * page, w), cache_t.dtype),
        compiler_params=pltpu.CompilerParams(dimension_semantics=("arbitrary", "arbitrary")),
        name="page_rows",
    )(page_table, *([cache_t] * pps))


SCAN_C = 64
SCAN_TT = 512
HALO = 8


def _dot_tn(a, b):
    return lax.dot_general(a, b, (((0,), (0,)), ((), ())), preferred_element_type=jnp.float32)


def _mx(x):
    return x.astype(MXU_DT)


def _dot3(a, b):
    a_hi, b_hi = _mx(a), _mx(b)
    a_lo = _mx(a - a_hi.astype(jnp.float32))
    b_lo = _mx(b - b_hi.astype(jnp.float32))
    return _dot(a_hi, b_hi) + (_dot(a_hi, b_lo) + _dot(a_lo, b_hi))


def _chunk_cumsum(x):
    c = x.shape[0]
    row = lax.broadcasted_iota(jnp.int32, x.shape, 0)
    d = 1
    while d < c:
        x = x + jnp.where(row >= d, pltpu.roll(x, d, 0), 0.0)
        d *= 2
    return x


def _silu(x):
    return x * jax.nn.sigmoid(x)


def _conv_tile(x, halo, w_ref):
    kw = w_ref.shape[0]
    r8 = lax.broadcasted_iota(jnp.int32, (HALO, x.shape[1]), 0)
    y = x * w_ref[kw - 1:kw, :]
    for d in range(1, kw):
        xs = pltpu.roll(x, d, 0)
        head = jnp.where(r8 < d, pltpu.roll(halo, d, 0), xs[0:HALO])
        xs = jnp.concatenate([head, xs[HALO:]], axis=0)
        y = y + xs * w_ref[kw - 1 - d:kw - d, :]
    return y


def _halo_spec(tt, width, col_block):
    return pl.BlockSpec((HALO, width), lambda i: (jnp.maximum(i * (tt // HALO) - 1, 0), col_block))


def _pad_buf(buf):
    return jnp.pad(buf.astype(jnp.float32), ((0, 0), (HALO - buf.shape[1], 0), (0, 0)))


def _hgrn_prep_kernel(q_ref, f_ref, v_ref, la_ref, l1_ref, lbc_ref, oi_ref, qb_ref, kb_ref, eb_ref):
    c = SCAN_C
    tt = q_ref.shape[0]
    row = lax.broadcasted_iota(jnp.int32, (c, LANE), 0)
    ti = lax.broadcasted_iota(jnp.int32, (c, c), 0)
    si = lax.broadcasted_iota(jnp.int32, (c, c), 1)
    for h in range(HG_HEADS):
        ls = slice(h * LANE, (h + 1) * LANE)
        la, l1, lbc = la_ref[:, ls], l1_ref[:, ls], lbc_ref[:, ls]
        for ci in range(tt // c):
            rs = slice(ci * c, (ci + 1) * c)
            q = _silu(q_ref[rs, ls])
            z = f_ref[rs, ls]
            v = v_ref[rs, ls]
            lsig = jnp.minimum(z, 0.0) - jnp.log1p(jnp.exp(-jnp.abs(z)))
            t2 = l1 + lsig
            hi = jnp.maximum(la, t2)
            logf = hi + jnp.log1p(jnp.exp(-jnp.abs(la - t2)))
            k = lbc * jax.nn.sigmoid(-z)
            b = _chunk_cumsum(logf)
            b_last = b[c - 1:c, :]
            qb_ref[rs, ls] = q * jnp.exp(b)
            kb_ref[rs, ls] = k * jnp.exp(b_last - b)
            eb_ref[ci:ci + 1, ls] = jnp.exp(b_last)
            att = jnp.zeros((c, c), jnp.float32)
            n = c // 2
            while n >= 8:
                blk = 2 * n
                ref_rows = jnp.concatenate(
                    [jnp.broadcast_to(b[j * blk + n - 1:j * blk + n, :], (blk, LANE)) for j in range(c // blk)], axis=0)
                upper = (row & (blk - 1)) >= n
                qs = q * jnp.exp(jnp.where(upper, b - ref_rows, 0.0))
                ks = k * jnp.exp(jnp.where(upper, 0.0, ref_rows - b))
                lvl = _dot_nt(_mx(qs), _mx(ks))
                ok = ((ti & ~(blk - 1)) == (si & ~(blk - 1))) & ((ti & (blk - 1)) >= n) & ((si & (blk - 1)) < n)
                att = att + jnp.where(ok, lvl, 0.0)
                n //= 2
            o = _dot(_mx(att), _mx(v))
            for d in range(8):
                if d == 0:
                    w = jnp.sum(q * k, axis=-1, keepdims=True)
                    o = o + w * v
                else:
                    e = jnp.exp(jnp.minimum(b - pltpu.roll(b, d, 0), 0.0))
                    w = jnp.sum(q * pltpu.roll(k, d, 0) * e, axis=-1, keepdims=True)
                    w = jnp.where((row[:, 0:1] & 7) >= d, w, 0.0)
                    o = o + w * pltpu.roll(v, d, 0)
            oi_ref[rs, ls] = o


def hgrn_prep(c_all, col_q, col_f, col_v, lb):
    m = c_all.shape[0]
    tt = SCAN_TT
    lbf = lb.reshape(1, BRANCH_W).astype(jnp.float32)
    la, l1, lbc = jnp.log(lbf), jnp.log1p(-lbf), 1.0 - lbf
    blk = lambda col: pl.BlockSpec((tt, BRANCH_W), lambda i, col=col: (i, col // BRANCH_W))
    vec = pl.BlockSpec((1, BRANCH_W), lambda i: (0, 0))
    out = pl.BlockSpec((tt, BRANCH_W), lambda i: (i, 0))
    return pl.pallas_call(
        _hgrn_prep_kernel,
        grid=(m // tt,),
        in_specs=[blk(col_q), blk(col_f), blk(col_v), vec, vec, vec],
        out_specs=[out, out, out, pl.BlockSpec((tt // SCAN_C, BRANCH_W), lambda i: (i, 0))],
        out_shape=[jax.ShapeDtypeStruct((m, BRANCH_W), jnp.float32)] * 3
                  + [jax.ShapeDtypeStruct((m // SCAN_C, BRANCH_W), jnp.float32)],
        compiler_params=pltpu.CompilerParams(dimension_semantics=("arbitrary",), vmem_limit_bytes=VMEM_LIMIT),
        name="hgrn_prep",
    )(c_all, c_all, c_all, la, l1, lbc)


def _hgrn_scan_kernel(oi_ref, qb_ref, kb_ref, eb_ref, v_ref, g_ref, nw_ref, s0_ref, o_ref, sT_out_ref, sT_ref):
    c = SCAN_C
    nb = oi_ref.shape[0]
    tt = oi_ref.shape[1]
    i = pl.program_id(0)

    @pl.when(i == 0)
    def _():
        sT_ref[...] = s0_ref[...]

    nw = nw_ref[...]
    for ci in range(tt // c):
        rs = slice(ci * c, (ci + 1) * c)
        for b in range(nb):
            for h in range(HG_HEADS):
                ls = slice(h * LANE, (h + 1) * LANE)
                sT = sT_ref[b, h]
                o = oi_ref[b, rs, ls] + _dot_nt(_mx(qb_ref[b, rs, ls]), _mx(sT))
                sT_ref[b, h] = eb_ref[b, ci:ci + 1, ls] * sT + _dot_tn(_mx(v_ref[b, rs, ls]), _mx(kb_ref[b, rs, ls]))
                y = o * lax.rsqrt(jnp.mean(o * o, axis=-1, keepdims=True) + EPS) * nw
                o_ref[b, rs, ls] = y * _silu(g_ref[b, rs, ls])

    @pl.when(i == pl.num_programs(0) - 1)
    def _():
        sT_out_ref[...] = sT_ref[...]


def hgrn_scan(oi, qb, kb, eb, c_all, col_v, col_g, norm_w, s0, batch, seq_len):
    tt = SCAN_TT
    n3 = lambda a: a.reshape(batch, seq_len, a.shape[-1])
    c3 = n3(c_all)
    tok = pl.BlockSpec((batch, tt, BRANCH_W), lambda i: (0, i, 0))
    ctok = lambda col: pl.BlockSpec((batch, tt, BRANCH_W), lambda i, col=col: (0, i, col // BRANCH_W))
    st = pl.BlockSpec((batch, HG_HEADS, LANE, LANE), lambda i: (0, 0, 0, 0))
    o, sT = pl.pallas_call(
        _hgrn_scan_kernel,
        grid=(seq_len // tt,),
        in_specs=[tok, tok, tok, pl.BlockSpec((batch, tt // SCAN_C, BRANCH_W), lambda i: (0, i, 0)),
                  ctok(col_v), ctok(col_g), pl.BlockSpec((1, LANE), lambda i: (0, 0)), st],
        out_specs=[tok, st],
        out_shape=[jax.ShapeDtypeStruct((batch, seq_len, BRANCH_W), jnp.float32),
                   jax.ShapeDtypeStruct((batch, HG_HEADS, LANE, LANE), jnp.float32)],
        scratch_shapes=[pltpu.VMEM((batch, HG_HEADS, LANE, LANE), jnp.float32)],
        compiler_params=pltpu.CompilerParams(dimension_semantics=("arbitrary",), vmem_limit_bytes=VMEM_LIMIT),
        name="hgrn_scan",
    )(n3(oi), n3(qb), n3(kb), eb.reshape(batch, seq_len // SCAN_C, BRANCH_W), c3, c3,
      norm_w.reshape(1, LANE), jnp.swapaxes(s0, -1, -2))
    return o.reshape(batch * seq_len, BRANCH_W), jnp.swapaxes(sT, -1, -2)


def _gdn_prep_kernel(x_ref, halo_ref, buf_ref, misc_ref, cw_ref, alog_ref, dtb_ref,
                     u_ref, w_ref, qg_ref, kg_ref, qk_ref, eg_ref, *, tiles_per_seq):
    c = SCAN_C
    tt = x_ref.shape[0]
    first = (pl.program_id(0) % tiles_per_seq) == 0
    halo = jnp.where(first, buf_ref[0], halo_ref[...])
    y = _silu(_conv_tile(x_ref[...], halo, cw_ref))
    misc = misc_ref[...]
    beta_all = jax.nn.sigmoid(misc)
    sp_in = misc + dtb_ref[...]
    sp = jnp.maximum(sp_in, 0.0) + jnp.log1p(jnp.exp(-jnp.abs(sp_in)))
    g_all = -jnp.exp(alog_ref[...]) * sp
    ti = lax.broadcasted_iota(jnp.int32, (c, c), 0)
    si = lax.broadcasted_iota(jnp.int32, (c, c), 1)
    eye = (ti == si).astype(jnp.float32)
    for ci in range(tt // c):
        rs = slice(ci * c, (ci + 1) * c)
        gam_all = _chunk_cumsum(g_all[rs])
        gam_t = gam_all.T
        heads = []
        for h in range(DN_HEADS):
            q = y[rs, h * LANE:(h + 1) * LANE]
            k = y[rs, BRANCH_W + h * LANE:BRANCH_W + (h + 1) * LANE]
            v = y[rs, 2 * BRANCH_W + h * LANE:2 * BRANCH_W + (h + 1) * LANE]
            q = q * lax.rsqrt(jnp.sum(q * q, axis=-1, keepdims=True) + EPS) * (DN_DK ** -0.5)
            k = k * lax.rsqrt(jnp.sum(k * k, axis=-1, keepdims=True) + EPS)
            beta = beta_all[rs, h:h + 1]
            gam = gam_all[:, DN_HEADS + h:DN_HEADS + h + 1]
            gam_r = gam_t[DN_HEADS + h:DN_HEADS + h + 1, :]
            decay = jnp.exp(jnp.where(si <= ti, gam - gam_r, NEG_INF))
            kb16 = _mx(k)
            kk = _dot_nt(kb16, kb16)
            a = jnp.where(si < ti, beta * kk * decay, 0.0)
            heads.append((q, k, v, beta, gam, decay, kb16, a))
        tinvs = [eye] * DN_HEADS
        s = 1
        while s < c:
            blk = 2 * s
            off = (((ti & ~(blk - 1)) == (si & ~(blk - 1))) & ((ti & (blk - 1)) >= s) & ((si & (blk - 1)) < s))
            a_offs = [jnp.where(off, hd[7], 0.0) for hd in heads]
            if s == 1:
                tinvs = [t - ao for t, ao in zip(tinvs, a_offs)]
            else:
                mids = [_dot3(t, ao) for t, ao in zip(tinvs, a_offs)]
                tinvs = [t - _dot3(md, t) for t, md in zip(tinvs, mids)]
            s = blk
        for h, (q, k, v, beta, gam, decay, kb16, a) in enumerate(heads):
            ls = slice(h * LANE, (h + 1) * LANE)
            e_gam = jnp.exp(gam)
            rhs = jnp.concatenate([v * beta, k * (beta * e_gam)], axis=1)
            sol = rhs + _dot3(tinvs[h] - eye, rhs)
            u_ref[rs, ls] = sol[:, :LANE]
            w_ref[rs, ls] = sol[:, LANE:]
            qg_ref[rs, ls] = q * e_gam
            g_last = gam[c - 1:c, :]
            kg_ref[rs, ls] = k * jnp.exp(g_last - gam)
            qk_ref[rs, h * c:(h + 1) * c] = _dot_nt(_mx(q), kb16) * decay
            eg_ref[ci:ci + 1, ls] = jnp.broadcast_to(jnp.exp(g_last), (1, LANE))


def gdn_prep(c_all, col_x, col_misc, conv_buf, conv_w, a_log, dt_bias, batch, seq_len):
    m = c_all.shape[0]
    tt = SCAN_TT
    tps = seq_len // tt
    xw = 3 * BRANCH_W
    lanes = jnp.zeros((1, LANE), jnp.float32)
    alog = lanes.at[0, DN_HEADS:2 * DN_HEADS].set(a_log.astype(jnp.float32))
    dtb = lanes.at[0, DN_HEADS:2 * DN_HEADS].set(dt_bias.astype(jnp.float32))
    out = pl.BlockSpec((tt, BRANCH_W), lambda i: (i, 0))
    vec = pl.BlockSpec((1, LANE), lambda i: (0, 0))
    return pl.pallas_call(
        functools.partial(_gdn_prep_kernel, tiles_per_seq=tps),
        grid=(m // tt,),
        in_specs=[pl.BlockSpec((tt, xw), lambda i: (i, col_x // xw)),
                  _halo_spec(tt, xw, col_x // xw),
                  pl.BlockSpec((1, HALO, xw), lambda i: (i // tps, 0, 0)),
                  pl.BlockSpec((tt, LANE), lambda i: (i, col_misc // LANE)),
                  pl.BlockSpec((DN_CONV, xw), lambda i: (0, 0)), vec, vec],
        out_specs=[out, out, out, out, pl.BlockSpec((tt, DN_HEADS * SCAN_C), lambda i: (i, 0)),
                   pl.BlockSpec((tt // SCAN_C, BRANCH_W), lambda i: (i, 0))],
        out_shape=[jax.ShapeDtypeStruct((m, BRANCH_W), jnp.float32)] * 4
                  + [jax.ShapeDtypeStruct((m, DN_HEADS * SCAN_C), jnp.float32),
                     jax.ShapeDtypeStruct((m // SCAN_C, BRANCH_W), jnp.float32)],
        compiler_params=pltpu.CompilerParams(dimension_semantics=("arbitrary",), vmem_limit_bytes=VMEM_LIMIT),
        name="gdn_prep",
    )(c_all, c_all, _pad_buf(conv_buf), c_all, conv_w.astype(jnp.float32), alog, dtb)


def _gdn_scan_kernel(u_ref, w_ref, qg_ref, kg_ref, qk_ref, eg_ref, z_ref, nw_ref, s0_ref, o_ref, s_out_ref, s_ref):
    c = SCAN_C
    nb = u_ref.shape[0]
    tt = u_ref.shape[1]
    i = pl.program_id(0)

    @pl.when(i == 0)
    def _():
        s_ref[...] = s0_ref[...]

    nw = nw_ref[...]
    for ci in range(tt // c):
        rs = slice(ci * c, (ci + 1) * c)
        for b in range(nb):
            for h in range(DN_HEADS):
                ls = slice(h * LANE, (h + 1) * LANE)
                s = s_ref[b, h]
                s16 = _mx(s)
                both = _dot(_mx(jnp.concatenate([qg_ref[b, rs, ls], w_ref[b, rs, ls]], axis=0)), s16)
                v_new = u_ref[b, rs, ls] - both[c:]
                v16 = _mx(v_new)
                o = both[:c] + _dot(_mx(qk_ref[b, rs, h * c:(h + 1) * c]), v16)
                s_ref[b, h] = eg_ref[b, ci:ci + 1, ls][:, 0:1] * s + _dot_tn(_mx(kg_ref[b, rs, ls]), v16)
                y = o * lax.rsqrt(jnp.mean(o * o, axis=-1, keepdims=True) + EPS) * nw
                o_ref[b, rs, ls] = y * _silu(z_ref[b, rs, ls])

    @pl.when(i == pl.num_programs(0) - 1)
    def _():
        s_out_ref[...] = s_ref[...]


def gdn_scan(u, w, qg, kg, qk, eg, c_all, col_z, norm_w, s0, batch, seq_len):
    tt = SCAN_TT
    n3 = lambda a: a.reshape(batch, seq_len, a.shape[-1])
    tok = pl.BlockSpec((batch, tt, BRANCH_W), lambda i: (0, i, 0))
    st = pl.BlockSpec((batch, DN_HEADS, LANE, LANE), lambda i: (0, 0, 0, 0))
    o, s = pl.pallas_call(
        _gdn_scan_kernel,
        grid=(seq_len // tt,),
        in_specs=[tok, tok, tok, tok, pl.BlockSpec((batch, tt, DN_HEADS * SCAN_C), lambda i: (0, i, 0)),
                  pl.BlockSpec((batch, tt // SCAN_C, BRANCH_W), lambda i: (0, i, 0)),
                  pl.BlockSpec((batch, tt, BRANCH_W), lambda i: (0, i, col_z // BRANCH_W)),
                  pl.BlockSpec((1, LANE), lambda i: (0, 0)), st],
        out_specs=[tok, st],
        out_shape=[jax.ShapeDtypeStruct((batch, seq_len, BRANCH_W), jnp.float32),
                   jax.ShapeDtypeStruct((batch, DN_HEADS, LANE, LANE), jnp.float32)],
        scratch_shapes=[pltpu.VMEM((batch, DN_HEADS, LANE, LANE), jnp.float32)],
        compiler_params=pltpu.CompilerParams(dimension_semantics=("arbitrary",), vmem_limit_bytes=VMEM_LIMIT),
        name="gdn_scan",
    )(n3(u), n3(w), n3(qg), n3(kg), n3(qk), eg.reshape(batch, seq_len // SCAN_C, BRANCH_W), n3(c_all),
      norm_w.reshape(1, LANE), s0.astype(jnp.float32))
    return o.reshape(batch * seq_len, BRANCH_W), s


def _sc_branch_kernel(x_ref, halo_ref, buf_ref, w_ref, o_ref, *, tiles_per_seq):
    first = (pl.program_id(0) % tiles_per_seq) == 0
    x = x_ref[...]
    hr = halo_ref[...]
    halo = jnp.where(first, buf_ref[0], hr[:, SC_W:2 * SC_W] * hr[:, 2 * SC_W:3 * SC_W])
    o_ref[...] = x[:, 0:SC_W] * _conv_tile(x[:, SC_W:2 * SC_W] * x[:, 2 * SC_W:3 * SC_W], halo, w_ref)


def sc_branch(c_all, col_x, conv_buf, conv_w, seq_len):
    m = c_all.shape[0]
    tt = SCAN_TT
    tps = seq_len // tt
    xw = 3 * SC_W
    return pl.pallas_call(
        functools.partial(_sc_branch_kernel, tiles_per_seq=tps),
        grid=(m // tt,),
        in_specs=[pl.BlockSpec((tt, xw), lambda i: (i, col_x // xw)),
                  _halo_spec(tt, xw, col_x // xw),
                  pl.BlockSpec((1, HALO, SC_W), lambda i: (i // tps, 0, 0)),
                  pl.BlockSpec((SC_CONV, SC_W), lambda i: (0, 0))],
        out_specs=pl.BlockSpec((tt, SC_W), lambda i: (i, 0)),
        out_shape=jax.ShapeDtypeStruct((m, SC_W), jnp.float32),
        compiler_params=pltpu.CompilerParams(dimension_semantics=("arbitrary",), vmem_limit_bytes=VMEM_LIMIT),
        name="sc_branch",
    )(c_all, c_all, _pad_buf(conv_buf), conv_w.astype(jnp.float32))


def rms_norm(x, g):
    xf = x.astype(jnp.float32)
    y = xf * lax.rsqrt(jnp.mean(xf * xf, axis=-1, keepdims=True) + EPS)
    return (y * g.astype(jnp.float32)).astype(x.dtype)


def l2_normalize(x):
    xf = x.astype(jnp.float32)
    return xf * lax.rsqrt(jnp.sum(xf * xf, axis=-1, keepdims=True) + EPS)


def causal_conv(x, buf, w):
    k_w = w.shape[0]
    seq_len = x.shape[1]
    xp = jnp.concatenate([buf.astype(x.dtype), x], axis=1)
    y = sum(xp[:, j:j + seq_len] * w[j] for j in range(k_w))
    return y, xp[:, seq_len:]


def to_chunks(a, c):
    b, seq_len = a.shape[:2]
    n = -(-seq_len // c)
    a = jnp.pad(a, [(0, 0), (0, n * c - seq_len)] + [(0, 0)] * (a.ndim - 2))
    return jnp.moveaxis(a.reshape((b, n, c) + a.shape[2:]), 1, 0)


def from_chunks(a, seq_len):
    n, b, c = a.shape[:3]
    return jnp.moveaxis(a, 0, 1).reshape((b, n * c) + a.shape[3:])[:, :seq_len]


def hgrn2_scan(q, k, v, logf, s0):
    seq_len = q.shape[1]
    c = min(HG_CHUNK, seq_len)
    tri = jnp.tril(jnp.ones((c, c), bool))[None, :, :, None, None]

    def step(s, inp):
        qc, kc, vc, gc = inp
        b = jnp.cumsum(gc, axis=1)
        decay = jnp.exp(jnp.where(tri, b[:, :, None] - b[:, None], -jnp.inf))
        att = jnp.einsum('bthk,btshk->btsh', qc, decay * kc[:, None])
        o = jnp.einsum('btsh,bshv->bthv', att, vc) + jnp.einsum('bthk,bhkv->bthv', qc * jnp.exp(b), s)
        b_last = b[:, -1]
        s = jnp.exp(b_last)[..., None] * s + jnp.einsum('bshk,bshv->bhkv', kc * jnp.exp(b_last[:, None] - b), vc)
        return s, o

    xs = tuple(to_chunks(a.astype(jnp.float32), c) for a in (q, k, v, logf))
    s, o = lax.scan(step, s0.astype(jnp.float32), xs)
    return from_chunks(o, seq_len), s


def gated_delta_scan(q, k, v, beta, g, s0):
    seq_len = q.shape[1]
    c = min(DN_CHUNK, seq_len)
    incl = jnp.tril(jnp.ones((c, c), bool))
    strict = jnp.tril(jnp.ones((c, c), bool), -1)
    eye = jnp.eye(c, dtype=jnp.float32)

    def step(s, inp):
        qc, kc, vc, bc, gc = inp
        qh, kh, vh = (jnp.swapaxes(a, 1, 2) for a in (qc, kc, vc))
        bh = jnp.swapaxes(bc, 1, 2)
        gam = jnp.cumsum(jnp.swapaxes(gc, 1, 2), axis=-1)
        decay = jnp.exp(jnp.where(incl, gam[..., :, None] - gam[..., None, :], -jnp.inf))
        kk = jnp.einsum('bhtk,bhsk->bhts', kh, kh)
        t_mat = eye + jnp.where(strict, bh[..., :, None] * kk * decay, 0.0)
        u = lax.linalg.triangular_solve(t_mat, vh * bh[..., None], left_side=True, lower=True)
        w = lax.linalg.triangular_solve(t_mat, kh * (bh * jnp.exp(gam))[..., None], left_side=True, lower=True)
        v_new = u - jnp.einsum('bhtk,bhkv->bhtv', w, s)
        qk = jnp.einsum('bhtk,bhsk->bhts', qh, kh) * decay
        o = jnp.einsum('bhtk,bhkv->bhtv', qh * jnp.exp(gam)[..., None], s) + jnp.einsum('bhts,bhsv->bhtv', qk, v_new)
        g_last = gam[..., -1]
        s = jnp.exp(g_last)[..., None, None] * s + jnp.einsum('bhtk,bhtv->bhkv', kh * jnp.exp(g_last[..., None] - gam)[..., None], v_new)
        return s, jnp.swapaxes(o, 1, 2)

    xs = tuple(to_chunks(a.astype(jnp.float32), c) for a in (q, k, v, beta, g))
    s, o = lax.scan(step, s0.astype(jnp.float32), xs)
    return from_chunks(o, seq_len), s


def trunk_layer(x, start, hg_s, dn_s, dn_buf, sc_buf, past_rows, win_buf, w):
    b, seq_len, _ = x.shape
    f32 = jnp.float32
    x2 = x.reshape(b * seq_len, D_MODEL)
    m = b * seq_len
    c2 = _norm_proj(x2, w["norm_mix"], w["w_in"])
    c_all = c2.reshape(b, seq_len, N_IN_PAD)
    lb = w["lb"]
    if past_rows is None:
        oi, qb, kb, eb = hgrn_prep(c2, C_OFF["hg_q"], C_OFF["hg_f"], C_OFF["hg_i"], lb)
        o_a, hg_s = hgrn_scan(oi, qb, kb, eb, c2, C_OFF["hg_i"], C_OFF["hg_g"], w["hg_norm"], hg_s, b, seq_len)
        u, wy, qg, kg, qk, eg = gdn_prep(c2, C_OFF["dn_qkv"], C_OFF["misc"], dn_buf, w["dn_conv"], w["dn_a_log"],
                                         w["dn_dt_bias"], b, seq_len)
        o_b, dn_s = gdn_scan(u, wy, qg, kg, qk, eg, c2, C_OFF["dn_z"], w["dn_norm"], dn_s, b, seq_len)
        o_c = sc_branch(c2, C_OFF["sc_bch"], sc_buf, w["sc_conv"], seq_len)
        tail = c_all[:, seq_len - (DN_CONV - 1):]
        dn_buf = tail[..., C_OFF["dn_qkv"]:C_OFF["dn_qkv"] + 3 * BRANCH_W]
        sc_t = tail[:, DN_CONV - SC_CONV:, C_OFF["sc_bch"] + SC_W:C_OFF["sc_bch"] + 3 * SC_W]
        sc_buf = sc_t[..., :SC_W] * sc_t[..., SC_W:]
    else:
        o_a, o_b, o_c, (hg_s, dn_s, dn_buf, sc_buf) = cached_mixers(x, c_all, hg_s, dn_s, dn_buf, sc_buf, w)
    tabs = rope_tables(start + jnp.arange(seq_len))
    rows, win_rows, kvb = nsa_prep(c2, C_OFF["nsa_kv"], tabs, seq_len)
    rows = rows.reshape(b, seq_len, 4 * LANE)
    win_rows = win_rows.reshape(b, seq_len, 2 * LANE)
    kvb = kvb.reshape(b, seq_len, 4 * LANE)
    nsa_cols = (c2, C_OFF["nsa_q"], C_OFF["misc"], tabs)
    cmp_w = (w["cmp_pos"], w["cmp_w1"], w["cmp_w2"])
    if past_rows is None:
        ckv = nsa_compress(rows, *cmp_w)
        no_new = jnp.zeros((b, LANE, 2 * LANE), MXU_DT)
        o_d = nsa_attend(*nsa_cols, ckv, kvb, (0, 1), no_new, 0, kvb, (2, 3), 0, b, seq_len, 0)
        new_win = win_rows[:, max(seq_len - WINDOW, 0):]
    else:
        n_past = past_rows.shape[1]
        assert n_past % CMP_STRIDE == 0 and seq_len < CMP_STRIDE
        past = past_rows.reshape(b, n_past, 4 * LANE)
        ckv = nsa_compress(past, *cmp_w)
        new_kv = jnp.pad(kvb[:, :, 0:2 * LANE], ((0, 0), (0, LANE - seq_len), (0, 0)))
        nb = win_buf.shape[1]
        w_all = jnp.concatenate([win_buf.reshape(b, nb, 2 * LANE).astype(jnp.float32), win_rows], axis=1)
        w_pad = jnp.pad(w_all, ((0, 0), (0, max(NSA_WSPAN - nb - seq_len, 0)), (0, 0)))
        o_d = nsa_attend(*nsa_cols, ckv, past, (2, 3), new_kv, seq_len, w_pad, (0, 1), start - nb, b, seq_len, start)
        new_win = w_all[:, seq_len:]
    rows = rows.reshape(b, seq_len, 4, NSA_KV_HEADS, NSA_HD)
    new_win = new_win.reshape(b, new_win.shape[1], 2, NSA_KV_HEADS, NSA_HD)
    branches = [o_a.reshape(m, BRANCH_W), o_b.reshape(m, BRANCH_W), o_c.reshape(m, BRANCH_W), o_d.reshape(m, BRANCH_W)]
    x2 = _merge_out(x2, branches, c2, C_OFF["merge_gate"], w["w_branch"], w["w_out"])
    x2 = _mlp(x2, w["norm_mlp"], w["w_up"], w["w_down"])
    return x2.reshape(b, seq_len, D_MODEL), (hg_s, dn_s, dn_buf, sc_buf, new_win, rows)


def cached_mixers(x, c_all, hg_s, dn_s, dn_buf, sc_buf, w):
    b, seq_len, _ = x.shape
    f32 = jnp.float32
    c = {n: c_all[..., C_OFF[n]:C_OFF[n] + wd] for n, wd in C_ORDER}
    for n, o0, wd in MISC_COLS:
        c[n] = c["misc"][..., o0:o0 + wd]
    lb = w["lb"]
    hq = jax.nn.silu(c["hg_q"]).reshape(b, seq_len, HG_HEADS, HG_DK)
    z = c["hg_f"].astype(f32).reshape(b, seq_len, HG_HEADS, HG_DK)
    logf = jnp.logaddexp(jnp.log(lb), jnp.log1p(-lb) + jax.nn.log_sigmoid(z))
    hk = (1.0 - lb) * jax.nn.sigmoid(-z)
    hv = c["hg_i"].reshape(b, seq_len, HG_HEADS, HG_DV)
    o_a, hg_s = hgrn2_scan(hq, hk, hv, logf, hg_s)
    o_a = rms_norm(o_a.astype(x.dtype), w["hg_norm"]) * jax.nn.silu(c["hg_g"].reshape(b, seq_len, HG_HEADS, HG_DV))
    qkv, dn_buf = causal_conv(c["dn_qkv"], dn_buf, w["dn_conv"])
    dq, dk, dv = jnp.split(jax.nn.silu(qkv), 3, axis=-1)
    dq = l2_normalize(dq.reshape(b, seq_len, DN_HEADS, DN_DK)) * DN_DK ** -0.5
    dk = l2_normalize(dk.reshape(b, seq_len, DN_HEADS, DN_DK))
    beta = jax.nn.sigmoid(c["dn_b"].astype(f32))
    g = -jnp.exp(w["dn_a_log"].astype(f32)) * jax.nn.softplus(c["dn_a"].astype(f32) + w["dn_dt_bias"])
    o_b, dn_s = gated_delta_scan(dq, dk, dv.reshape(b, seq_len, DN_HEADS, DN_DV), beta, g, dn_s)
    o_b = rms_norm(o_b.astype(x.dtype), w["dn_norm"]) * jax.nn.silu(c["dn_z"].reshape(b, seq_len, DN_HEADS, DN_DV))
    gb, gc, hx = jnp.split(c["sc_bch"], 3, axis=-1)
    conv, sc_buf = causal_conv(gc * hx, sc_buf, w["sc_conv"])
    o_c = gb * conv
    return o_a, o_b, o_c, (hg_s, dn_s, dn_buf, sc_buf)


def kernel(x_prompt, x_sample, state_hgrn, state_dn, state_dn_conv, state_sc_conv, state_win_kv, cache_kv, page_table, norm_mix, norm_mlp, norm_final, w_in, hg_lb_logits, hg_norm, dn_conv, dn_a_log, dn_dt_bias, dn_norm, sc_conv, cmp_pos, cmp_w1, cmp_w2, w_branch, w_out, w_up, w_down):
    f32 = jnp.float32
    bf16 = jnp.bfloat16
    lbs = jnp.cumsum(jax.nn.softmax(hg_lb_logits.astype(f32), axis=0), axis=0)
    lbs = lbs - lbs[:1]
    w_in_b = _permute_w_in(w_in).astype(bf16)
    w_branch_b, w_out_b, w_up_b, w_down_b = (a.astype(bf16) for a in (w_branch, w_out, w_up, w_down))

    def layer_w(l):
        return dict(norm_mix=norm_mix[l], norm_mlp=norm_mlp[l], w_in=w_in_b[l], lb=lbs[l], hg_norm=hg_norm[l],
                    dn_conv=dn_conv[l], dn_a_log=dn_a_log[l], dn_dt_bias=dn_dt_bias[l], dn_norm=dn_norm[l],
                    sc_conv=sc_conv[l], cmp_pos=cmp_pos[l], cmp_w1=cmp_w1[l], cmp_w2=cmp_w2[l],
                    w_branch=w_branch_b[l], w_out=w_out_b[l], w_up=w_up_b[l], w_down=w_down_b[l])

    bp = x_prompt.shape[0]
    yp = x_prompt
    p_st = []
    for l in range(DEPTH):
        yp, st = trunk_layer(yp, 0,
                             jnp.zeros((bp, HG_HEADS, HG_DK, HG_DV), f32),
                             jnp.zeros((bp, DN_HEADS, DN_DK, DN_DV), f32),
                             jnp.zeros((bp, DN_CONV - 1, 3 * BRANCH_W), x_prompt.dtype),
                             jnp.zeros((bp, SC_CONV - 1, SC_W), x_prompt.dtype),
                             None, None, layer_w(l))
        p_st.append(st)
    ys = x_sample
    s_st = []
    cache_t = jnp.transpose(cache_kv, (0, 1, 3, 4, 5, 2))
    for l in range(DEPTH):
        past = page_rows(cache_t, page_table, l)
        ys, st = trunk_layer(ys, past.shape[1], state_hgrn[l], state_dn[l], state_dn_conv[l], state_sc_conv[l],
                             past, state_win_kv[l], layer_w(l))
        s_st.append(st)
    p = [jnp.stack([st[i] for st in p_st]) for i in range(6)]
    s = [jnp.stack([st[i] for st in s_st]) for i in range(6)]
    return (rms_norm(yp, norm_final), rms_norm(ys, norm_final),
            p[0], p[1], p[2], p[3], p[4], p[5],
            s[0], s[1], s[2], s[3], s[4], s[5])
```

```python
import math, functools
import jax, jax.numpy as jnp
from jax import lax
import numpy as np
from jax.experimental import pallas as pl
from jax.experimental.pallas import tpu as pltpu

D_MODEL = 1024
DEPTH = 4
PAGE_SIZE = 128
N_BRANCH = 4
BRANCH_W = D_MODEL // 2
HG_HEADS = 4
HG_DK = BRANCH_W // HG_HEADS
HG_DV = BRANCH_W // HG_HEADS
HG_CHUNK = 64
DN_HEADS = 4
DN_DK = BRANCH_W // DN_HEADS
DN_DV = BRANCH_W // DN_HEADS
DN_CONV = 4
DN_CHUNK = 64
SC_W = BRANCH_W
SC_CONV = 3
NSA_HEADS = 8
NSA_KV_HEADS = 2
NSA_HD = BRANCH_W // NSA_HEADS
NSA_GROUP = NSA_HEADS // NSA_KV_HEADS
ROPE_DIM = NSA_HD // 4
ROPE_THETA = 500000.0
CMP_BLOCK = 32
CMP_STRIDE = 16
CMP_HIDDEN = 4 * NSA_HD
SEL_BLOCK = 64
SEL_TOPN = 16
WINDOW = 512
Q_BLOCK = 128
D_FF = 4 * D_MODEL
EPS = 1e-6

IN_SPLITS = (
    ("hg_q", BRANCH_W), ("hg_f", BRANCH_W), ("hg_i", BRANCH_W), ("hg_g", BRANCH_W),
    ("dn_qkv", 3 * BRANCH_W), ("dn_b", DN_HEADS), ("dn_a", DN_HEADS), ("dn_z", BRANCH_W),
    ("sc_bch", 3 * SC_W),
    ("nsa_q", NSA_HEADS * NSA_HD), ("nsa_kv", 6 * NSA_KV_HEADS * NSA_HD), ("nsa_gate", 3 * NSA_HEADS),
    ("merge_gate", N_BRANCH * D_MODEL),
)
IN_NAMES = tuple(n for n, _ in IN_SPLITS)
IN_CUTS = tuple(int(c) for c in np.cumsum([s for _, s in IN_SPLITS])[:-1])
N_IN = sum(s for _, s in IN_SPLITS)

LANE = 128
PROJ_TM = 1024
PROJ_TN = 1024
VMEM_LIMIT = 48 * 1024 * 1024
MXU_DT = jnp.bfloat16

MISC_W = 2 * LANE
C_ORDER = (("merge_gate", N_BRANCH * D_MODEL), ("hg_q", BRANCH_W), ("hg_f", BRANCH_W), ("hg_i", BRANCH_W),
           ("hg_g", BRANCH_W), ("dn_qkv", 3 * BRANCH_W), ("sc_bch", 3 * SC_W), ("dn_z", BRANCH_W),
           ("nsa_q", NSA_HEADS * NSA_HD), ("nsa_kv", 6 * NSA_KV_HEADS * NSA_HD), ("misc", MISC_W))
C_OFF = {}
_o = 0
for _n, _w in C_ORDER:
    C_OFF[_n] = _o
    _o += _w
N_IN_PAD = _o
assert N_IN_PAD % PROJ_TN == 0
MISC_COLS = (("dn_b", 0, DN_HEADS), ("dn_a", DN_HEADS, DN_HEADS), ("nsa_gate", 2 * DN_HEADS, 3 * NSA_HEADS))
NEG_INF = float("-inf")


def _permute_w_in(w_in):
    src = dict(zip(IN_NAMES, jnp.split(w_in, IN_CUTS, axis=-1)))
    misc = jnp.concatenate([src[n] for n, _, _ in MISC_COLS], axis=-1)
    src["misc"] = jnp.pad(misc, ((0, 0), (0, 0), (0, MISC_W - misc.shape[-1])))
    return jnp.concatenate([src[n] for n, _ in C_ORDER], axis=-1)


def _row_tile(m):
    return 512 if m % 512 == 0 else m


def _norm_proj_kernel(x_ref, g_ref, w_ref, o_ref, h_ref):
    @pl.when(pl.program_id(1) == 0)
    def _():
        x = x_ref[...]
        y = x * lax.rsqrt(jnp.mean(x * x, axis=-1, keepdims=True) + EPS)
        h_ref[...] = (y * g_ref[...]).astype(jnp.bfloat16)

    o_ref[...] = jnp.dot(h_ref[...], w_ref[...], preferred_element_type=jnp.float32)


def _norm_proj(x, g, w_bf16):
    m, d = x.shape
    n = w_bf16.shape[1]
    tm = PROJ_TM if m % PROJ_TM == 0 else m
    return pl.pallas_call(
        _norm_proj_kernel,
        grid=(m // tm, n // PROJ_TN),
        in_specs=[pl.BlockSpec((tm, d), lambda i, j: (i, 0)),
                  pl.BlockSpec((1, d), lambda i, j: (0, 0)),
                  pl.BlockSpec((d, PROJ_TN), lambda i, j: (0, j))],
        out_specs=pl.BlockSpec((tm, PROJ_TN), lambda i, j: (i, j)),
        out_shape=jax.ShapeDtypeStruct((m, n), jnp.float32),
        scratch_shapes=[pltpu.VMEM((tm, d), jnp.bfloat16)],
        compiler_params=pltpu.CompilerParams(dimension_semantics=("arbitrary", "arbitrary"),
                                             vmem_limit_bytes=VMEM_LIMIT),
        name="norm_proj",
    )(x, g.reshape(1, d), w_bf16)


def _merge_kernel(x_ref, ba_ref, bb_ref, bc_ref, bd_ref, gate_ref, wb_ref, wo_ref, o_ref):
    acc = None
    for n, b_ref in enumerate((ba_ref, bb_ref, bc_ref, bd_ref)):
        p = jnp.dot(b_ref[...].astype(jnp.bfloat16), wb_ref[n], preferred_element_type=jnp.float32)
        t = jax.nn.sigmoid(gate_ref[:, n * D_MODEL:(n + 1) * D_MODEL]) * p
        acc = t if acc is None else acc + t
    o_ref[...] = x_ref[...] + jnp.dot(acc.astype(jnp.bfloat16), wo_ref[...], preferred_element_type=jnp.float32)


def _merge_out(x, branches, c_all, gate_col, wb_bf16, wo_bf16):
    m, d = x.shape
    tm = 256 if m % 256 == 0 else m
    row = lambda w: pl.BlockSpec((tm, w), lambda i: (i, 0))
    return pl.pallas_call(
        _merge_kernel,
        grid=(m // tm,),
        in_specs=[row(d)] + [row(BRANCH_W)] * N_BRANCH + [
                  pl.BlockSpec((tm, N_BRANCH * d), lambda i: (i, gate_col // (N_BRANCH * d))),
                  pl.BlockSpec((N_BRANCH, BRANCH_W, d), lambda i: (0, 0, 0)),
                  pl.BlockSpec((d, d), lambda i: (0, 0))],
        out_specs=row(d),
        out_shape=jax.ShapeDtypeStruct((m, d), jnp.float32),
        compiler_params=pltpu.CompilerParams(dimension_semantics=("arbitrary",),
                                             vmem_limit_bytes=VMEM_LIMIT),
        name="merge_out",
    )(x, *branches, c_all, wb_bf16, wo_bf16)


MLP_TF = 1024


def _mlp_kernel(x_ref, g_ref, wu_ref, wd_ref, o_ref, h_ref, acc_ref):
    j = pl.program_id(1)

    @pl.when(j == 0)
    def _():
        x = x_ref[...]
        y = x * lax.rsqrt(jnp.mean(x * x, axis=-1, keepdims=True) + EPS)
        h_ref[...] = (y * g_ref[...]).astype(jnp.bfloat16)
        acc_ref[...] = jnp.zeros_like(acc_ref)

    u = jnp.maximum(jnp.dot(h_ref[...], wu_ref[...], preferred_element_type=jnp.float32), 0.0)
    acc_ref[...] += jnp.dot((u * u).astype(jnp.bfloat16), wd_ref[...], preferred_element_type=jnp.float32)

    @pl.when(j == pl.num_programs(1) - 1)
    def _():
        o_ref[...] = x_ref[...] + acc_ref[...]


def _mlp(x, g, wu_bf16, wd_bf16):
    m, d = x.shape
    f = wu_bf16.shape[1]
    tm = _row_tile(m)
    return pl.pallas_call(
        _mlp_kernel,
        grid=(m // tm, f // MLP_TF),
        in_specs=[pl.BlockSpec((tm, d), lambda i, j: (i, 0)),
                  pl.BlockSpec((1, d), lambda i, j: (0, 0)),
                  pl.BlockSpec((d, MLP_TF), lambda i, j: (0, j)),
                  pl.BlockSpec((MLP_TF, d), lambda i, j: (j, 0))],
        out_specs=pl.BlockSpec((tm, d), lambda i, j: (i, 0)),
        out_shape=jax.ShapeDtypeStruct((m, d), jnp.float32),
        scratch_shapes=[pltpu.VMEM((tm, d), jnp.bfloat16), pltpu.VMEM((tm, d), jnp.float32)],
        compiler_params=pltpu.CompilerParams(dimension_semantics=("arbitrary", "arbitrary"),
                                             vmem_limit_bytes=VMEM_LIMIT),
        name="mlp",
    )(x, g.reshape(1, d), wu_bf16, wd_bf16)


NSA_TQ = 128
NSA_TK = 512
NSA_ROWS = NSA_HEADS * NSA_TQ
NSA_WSPAN = WINDOW + NSA_TQ


def _dot(a, b, **kw):
    return jnp.dot(a, b, preferred_element_type=jnp.float32, **kw)


def _dot_nt(a, b):
    return lax.dot_general(a, b, (((1,), (1,)), ((), ())), preferred_element_type=jnp.float32)


def rope_tables(pos):
    half = ROPE_DIM // 2
    inv_freq = ROPE_THETA ** (-jnp.arange(half, dtype=jnp.float32) / half)
    ang = pos.astype(jnp.float32)[:, None] * inv_freq
    cos, sin = jnp.cos(ang), jnp.sin(ang)
    n = pos.shape[0]
    one = jnp.ones((n, NSA_HD - ROPE_DIM), jnp.float32)
    zero = jnp.zeros((n, NSA_HD - ROPE_DIM), jnp.float32)
    z8 = jnp.zeros((n, half), jnp.float32)
    c = jnp.concatenate([cos, cos, one], axis=1)
    s1 = jnp.concatenate([-sin, z8, zero], axis=1)
    s2 = jnp.concatenate([z8, sin, zero], axis=1)
    two = lambda a: jnp.concatenate([a, a], axis=1)
    return two(c), two(s1), two(s2)


def _rope(x, c, s1, s2):
    n = x.shape[-1]
    return x * c + pltpu.roll(x, n - ROPE_DIM // 2, 1) * s1 + pltpu.roll(x, ROPE_DIM // 2, 1) * s2


def _nsa_prep_kernel(kv0_ref, kv1_ref, kv2_ref, c_ref, s1_ref, s2_ref, rows_ref, win_ref, kvb_ref):
    c, s1, s2 = c_ref[...], s1_ref[...], s2_ref[...]
    cmp_kv = kv0_ref[...]
    sel = kv1_ref[...]
    wnd = kv2_ref[...]
    ks = _rope(sel[:, :LANE], c, s1, s2)
    kw = _rope(wnd[:, :LANE], c, s1, s2)
    rows_ref[:, 0:2 * LANE] = cmp_kv
    rows_ref[:, 2 * LANE:3 * LANE] = ks
    rows_ref[:, 3 * LANE:4 * LANE] = sel[:, LANE:]
    win_ref[:, 0:LANE] = kw
    win_ref[:, LANE:2 * LANE] = wnd[:, LANE:]
    kvb_ref[:, 0:LANE] = ks.astype(MXU_DT)
    kvb_ref[:, LANE:2 * LANE] = sel[:, LANE:].astype(MXU_DT)
    kvb_ref[:, 2 * LANE:3 * LANE] = kw.astype(MXU_DT)
    kvb_ref[:, 3 * LANE:4 * LANE] = wnd[:, LANE:].astype(MXU_DT)


def nsa_prep(c_all, kv_col0, tabs, seq_len):
    m = c_all.shape[0]
    tm = 512 if seq_len % 512 == 0 else seq_len
    nlt = seq_len // tm
    cb = kv_col0 // (2 * LANE)
    kv_spec = lambda k: pl.BlockSpec((tm, 2 * LANE), lambda i, k=k: (i, cb + k))
    tab_spec = pl.BlockSpec((tm, LANE), lambda i: (i % nlt, 0))
    return pl.pallas_call(
        _nsa_prep_kernel,
        grid=(m // tm,),
        in_specs=[kv_spec(0), kv_spec(1), kv_spec(2), tab_spec, tab_spec, tab_spec],
        out_specs=[pl.BlockSpec((tm, 4 * LANE), lambda i: (i, 0)),
                   pl.BlockSpec((tm, 2 * LANE), lambda i: (i, 0)),
                   pl.BlockSpec((tm, 4 * LANE), lambda i: (i, 0))],
        out_shape=[jax.ShapeDtypeStruct((m, 4 * LANE), jnp.float32),
                   jax.ShapeDtypeStruct((m, 2 * LANE), jnp.float32),
                   jax.ShapeDtypeStruct((m, 4 * LANE), MXU_DT)],
        compiler_params=pltpu.CompilerParams(dimension_semantics=("arbitrary",), vmem_limit_bytes=VMEM_LIMIT),
        name="nsa_prep",
    )(c_all, c_all, c_all, *tabs)


def _nsa_compress_kernel(x_ref, pos_ref, w1_ref, w2_ref, o_ref):
    nh = x_ref.shape[0] // CMP_STRIDE
    top = jnp.zeros((nh, NSA_KV_HEADS * CMP_HIDDEN), jnp.float32)
    bot = jnp.zeros((nh, NSA_KV_HEADS * CMP_HIDDEN), jnp.float32)
    for t in range(CMP_STRIDE):
        x = x_ref[pl.ds(t, nh, stride=CMP_STRIDE), :]
        top = top + _dot((x + pos_ref[0, t:t + 1, :]).astype(MXU_DT), w1_ref[0, 0, t])
        bot = bot + _dot((x + pos_ref[0, CMP_STRIDE + t:CMP_STRIDE + t + 1, :]).astype(MXU_DT), w1_ref[0, 1, t])
    h = top + pltpu.roll(bot, nh - 1, 0)
    h = h * jax.nn.sigmoid(h)
    o_ref[0, 0] = _dot(h.astype(MXU_DT), w2_ref[0]).astype(o_ref.dtype)


def _block_diag2(w):
    z = jnp.zeros_like(w)
    return jnp.concatenate([jnp.concatenate([w, z], axis=-1), jnp.concatenate([z, w], axis=-1)], axis=-2)


def nsa_compress(seq_rows, cmp_pos, cmp_w1, cmp_w2):
    b, seq_len = seq_rows.shape[:2]
    nh = seq_len // CMP_STRIDE
    pos = jnp.concatenate([cmp_pos, cmp_pos], axis=-1)
    w1 = cmp_w1.reshape(2, 2, CMP_STRIDE, NSA_HD, CMP_HIDDEN)
    w1 = _block_diag2(w1).astype(MXU_DT)
    w2 = _block_diag2(cmp_w2).astype(MXU_DT)
    return pl.pallas_call(
        _nsa_compress_kernel,
        grid=(b, 2),
        in_specs=[pl.BlockSpec((None, seq_len, LANE), lambda i, j: (i, 0, j)),
                  pl.BlockSpec((1, CMP_BLOCK, LANE), lambda i, j: (j, 0, 0)),
                  pl.BlockSpec((1, 2, CMP_STRIDE, LANE, NSA_KV_HEADS * CMP_HIDDEN), lambda i, j: (j, 0, 0, 0, 0)),
                  pl.BlockSpec((1, NSA_KV_HEADS * CMP_HIDDEN, LANE), lambda i, j: (j, 0, 0))],
        out_specs=pl.BlockSpec((1, 1, nh, LANE), lambda i, j: (i, j, 0, 0)),
        out_shape=jax.ShapeDtypeStruct((b, 2, nh, LANE), MXU_DT),
        compiler_params=pltpu.CompilerParams(dimension_semantics=("arbitrary", "arbitrary"), vmem_limit_bytes=VMEM_LIMIT),
        name="nsa_compress",
    )(seq_rows, pos, w1, w2)


def _pad_heads(q):
    lane = lax.broadcasted_iota(jnp.int32, (q.shape[0], LANE), 1)
    blocks = []
    for h in range(NSA_HEADS):
        blk = q[:, (h // 2) * LANE:(h // 2 + 1) * LANE]
        g = h // NSA_GROUP
        if h % 2 != g:
            blk = pltpu.roll(blk, NSA_HD, 1)
        keep = (lane < NSA_HD) if g == 0 else (lane >= NSA_HD)
        blocks.append(jnp.where(keep, blk, 0.0))
    return jnp.concatenate(blocks, axis=0)


def _softmax_rows(s):
    m = jnp.max(s, axis=-1, keepdims=True)
    m = jnp.where(m == NEG_INF, 0.0, m)
    p = jnp.exp(s - m)
    return p / jnp.maximum(jnp.sum(p, axis=-1, keepdims=True), 1e-30)


def _nsa_attn_kernel(q_ref, gate_ref, c_ref, s1_ref, s2_ref, ck_ref, cv_ref, ks_ref, vs_ref, kn_ref, kw_ref, vw_ref,
                     o_ref, m_ref, l_ref, acc_ref, s_ref, *, q_start, win_start, ns, n_new):
    tq = q_ref.shape[1]
    n_rows = NSA_HEADS * tq
    n_keys = ks_ref.shape[1]
    i = pl.program_id(1)
    q0 = q_start + i * tq
    scale = NSA_HD ** -0.5
    q = q_ref[0] * scale
    c = jnp.concatenate([c_ref[...]] * 4, axis=1)
    s1 = jnp.concatenate([s1_ref[...]] * 4, axis=1)
    s2 = jnp.concatenate([s2_ref[...]] * 4, axis=1)
    q_raw = _pad_heads(q).astype(MXU_DT)
    q_rot = _pad_heads(_rope(q, c, s1, s2)).astype(MXU_DT)

    row = lax.broadcasted_iota(jnp.int32, (n_rows, 1), 0)
    qpos = q0 + (row & (tq - 1))

    nc = ck_ref.shape[1]
    s_c = _dot_nt(q_raw, ck_ref[0])
    c_end = lax.broadcasted_iota(jnp.int32, (1, nc), 1) * CMP_STRIDE + (CMP_BLOCK - 1)
    p_c = _softmax_rows(jnp.where(c_end <= qpos, s_c, NEG_INF))
    o_c = _dot(p_c.astype(MXU_DT), cv_ref[0])

    psum = jnp.concatenate(
        [sum(p_c[(g * NSA_GROUP + j) * tq:(g * NSA_GROUP + j + 1) * tq] for j in range(NSA_GROUP))
         for g in range(NSA_KV_HEADS)], axis=0)
    n_i = lax.broadcasted_iota(jnp.int32, (nc, ns), 0) * CMP_STRIDE
    m_i = lax.broadcasted_iota(jnp.int32, (nc, ns), 1) * SEL_BLOCK
    overlap = ((n_i <= m_i + (SEL_BLOCK - 1)) & (n_i + (CMP_BLOCK - 1) >= m_i)).astype(jnp.float32)
    imp = _dot(psum, overlap, precision=lax.Precision.HIGHEST)
    r2 = lax.broadcasted_iota(jnp.int32, (NSA_KV_HEADS * tq, 1), 0)
    qpos2 = q0 + (r2 & (tq - 1))
    cur = qpos2 >> 6
    blk = lax.broadcasted_iota(jnp.int32, (1, ns), 1)
    forced = (blk == 0) | (blk == cur) | (blk == cur - 1)
    valid = blk * SEL_BLOCK <= qpos2
    v = jnp.where(forced, jnp.inf, jnp.where(valid, imp, NEG_INF))
    blk_f = blk.astype(jnp.float32)
    sel = jnp.zeros(v.shape, jnp.float32)
    for _ in range(SEL_TOPN):
        mx = jnp.max(v, axis=-1, keepdims=True)
        first = jnp.min(jnp.where(v == mx, blk_f, float(ns)), axis=-1, keepdims=True)
        pick = blk_f == first
        sel = jnp.where(pick, 1.0, sel)
        v = jnp.where(pick, NEG_INF, v)
    sel_b = sel.astype(MXU_DT)

    m_ref[...] = jnp.full(m_ref.shape, NEG_INF, jnp.float32)
    l_ref[...] = jnp.zeros(l_ref.shape, jnp.float32)
    acc_ref[...] = jnp.zeros(acc_ref.shape, jnp.float32)
    n_kt = jnp.minimum((q0 + tq + NSA_TK - 1) // NSA_TK, n_keys // NSA_TK)

    def scores(k):
        return _dot_nt(q_rot, k.astype(MXU_DT))

    def update(s, vv, tok0):
        nk = s.shape[1]
        tok = tok0 + lax.broadcasted_iota(jnp.int32, (1, nk), 1)
        e_m = lax.broadcasted_iota(jnp.int32, (ns, nk), 0)
        e_t = (tok0 + lax.broadcasted_iota(jnp.int32, (ns, nk), 1)) >> 6
        chosen = _dot(sel_b, (e_m == e_t).astype(MXU_DT))
        bias2 = jnp.where((chosen > 0.5) & (tok <= qpos2), 0.0, NEG_INF)
        bias = jnp.concatenate([bias2[0:tq]] * NSA_GROUP + [bias2[tq:2 * tq]] * NSA_GROUP, axis=0)
        s = s + bias
        m_old = m_ref[...]
        m_new = jnp.maximum(m_old, jnp.max(s, axis=-1, keepdims=True))
        m_safe = jnp.where(m_new == NEG_INF, 0.0, m_new)
        alpha = jnp.exp(m_old - m_safe)
        p = jnp.exp(s - m_safe)
        l_ref[...] = alpha * l_ref[...] + jnp.sum(p, axis=-1, keepdims=True)
        acc_ref[...] = alpha * acc_ref[...] + _dot(p.astype(MXU_DT), vv.astype(MXU_DT))
        m_ref[...] = m_new

    def body(kt, carry):
        koff = pl.multiple_of(kt * NSA_TK, NSA_TK)
        knext = pl.multiple_of(jnp.minimum(kt + 1, n_kt - 1) * NSA_TK, NSA_TK)
        s = s_ref[...]
        s_next = scores(ks_ref[0, pl.ds(knext, NSA_TK), :])
        update(s, vs_ref[0, pl.ds(koff, NSA_TK), :], koff)
        s_ref[...] = s_next
        return carry

    s_ref[...] = scores(ks_ref[0, 0:NSA_TK, :])
    lax.fori_loop(0, n_kt, body, 0)
    if n_new:
        update(scores(kn_ref[0, :, 0:LANE]), kn_ref[0, :, LANE:2 * LANE], n_keys)
    o_s = acc_ref[...] / jnp.maximum(l_ref[...], 1e-30)

    w0 = jnp.clip(q0 - win_start - WINDOW, 0, kw_ref.shape[1] - NSA_WSPAN)
    w0 = pl.multiple_of(w0, tq)
    kw = kw_ref[0, pl.ds(w0, NSA_WSPAN), :]
    vw = vw_ref[0, pl.ds(w0, NSA_WSPAN), :]
    s_w = _dot_nt(q_rot, kw.astype(MXU_DT))
    kpos = win_start + w0 + lax.broadcasted_iota(jnp.int32, (1, NSA_WSPAN), 1)
    w_ok = (kpos <= qpos) & (kpos > qpos - WINDOW)
    p_w = _softmax_rows(jnp.where(w_ok, s_w, NEG_INF))
    o_w = _dot(p_w.astype(MXU_DT), vw.astype(MXU_DT))

    gates = jax.nn.sigmoid(gate_ref[0])
    lane = lax.broadcasted_iota(jnp.int32, (tq, LANE), 1)
    outs = []
    for h in range(NSA_HEADS):
        sl = slice(h * tq, (h + 1) * tq)
        k0 = 2 * DN_HEADS + 3 * h
        o_h = (gates[:, k0:k0 + 1] * o_c[sl] + gates[:, k0 + 1:k0 + 2] * o_s[sl] + gates[:, k0 + 2:k0 + 3] * o_w[sl])
        if h % 2 != h // NSA_GROUP:
            o_h = pltpu.roll(o_h, NSA_HD, 1)
        outs.append(o_h)
    for p2 in range(NSA_HEADS // 2):
        o_ref[0, :, p2 * LANE:(p2 + 1) * LANE] = jnp.where(lane < NSA_HD, outs[2 * p2], outs[2 * p2 + 1])


def nsa_attend(c_all, q_col0, misc_col0, tabs, ckv, sel_kv, sel_cols, new_kv, n_new, win_kv, win_cols, win_start,
               batch, seq_len, q_start):
    tq = NSA_TQ if seq_len % NSA_TQ == 0 else seq_len
    n_rows = NSA_HEADS * tq
    qb = q_col0 // (4 * LANE)
    mb = misc_col0 // LANE
    c3 = c_all.reshape(batch, seq_len, c_all.shape[1])
    n_blocks = -(-(sel_kv.shape[1] + n_new) // SEL_BLOCK)
    ns = -(-n_blocks // LANE) * LANE
    tab_spec = pl.BlockSpec((tq, LANE), lambda b, i: (i, 0))
    col_spec = lambda a, k: pl.BlockSpec((1, a.shape[1], LANE), lambda b, i, k=k: (b, 0, k))
    nc = ckv.shape[2]
    out = pl.pallas_call(
        functools.partial(_nsa_attn_kernel, q_start=q_start, win_start=win_start, ns=ns, n_new=n_new),
        grid=(batch, seq_len // tq),
        in_specs=[pl.BlockSpec((1, tq, 4 * LANE), lambda b, i: (b, i, qb)),
                  pl.BlockSpec((1, tq, LANE), lambda b, i: (b, i, mb)),
                  tab_spec, tab_spec, tab_spec,
                  pl.BlockSpec((None, 1, nc, LANE), lambda b, i: (b, 0, 0, 0)),
                  pl.BlockSpec((None, 1, nc, LANE), lambda b, i: (b, 1, 0, 0)),
                  col_spec(sel_kv, sel_cols[0]), col_spec(sel_kv, sel_cols[1]),
                  pl.BlockSpec((1, new_kv.shape[1], 2 * LANE), lambda b, i: (b, 0, 0)),
                  col_spec(win_kv, win_cols[0]), col_spec(win_kv, win_cols[1])],
        out_specs=pl.BlockSpec((1, tq, 4 * LANE), lambda b, i: (b, i, 0)),
        out_shape=jax.ShapeDtypeStruct((batch, seq_len, 4 * LANE), jnp.float32),
        scratch_shapes=[pltpu.VMEM((n_rows, 1), jnp.float32), pltpu.VMEM((n_rows, 1), jnp.float32),
                        pltpu.VMEM((n_rows, LANE), jnp.float32), pltpu.VMEM((n_rows, NSA_TK), jnp.float32)],
        compiler_params=pltpu.CompilerParams(dimension_semantics=("arbitrary", "arbitrary"), vmem_limit_bytes=VMEM_LIMIT),
        name="nsa_attn",
    )(c3, c3, *tabs, ckv, ckv, sel_kv, sel_kv, new_kv, win_kv, win_kv)
    return out.reshape(batch * seq_len, 4 * LANE)


PAGES_PER_STEP = 8


def _page_rows_kernel(pt_ref, *refs):
    o_ref = refs[-1]
    for j, x_ref in enumerate(refs[:-1]):
        n_idx, n_g, hd, page = x_ref.shape
        for i in range(n_idx):
            o_ref[0, j * page:(j + 1) * page, i * n_g * hd:(i + 1) * n_g * hd] = x_ref[i].reshape(n_g * hd, page).T


def page_rows(cache_t, page_table, layer):
    _, _, n_idx, n_g, hd, page = cache_t.shape
    b, n_pages = page_table.shape
    w = n_idx * n_g * hd
    pps = PAGES_PER_STEP if n_pages % PAGES_PER_STEP == 0 else 1
    page_spec = lambda j: pl.BlockSpec((None, None, n_idx, n_g, hd, page),
                                       lambda i, p, pt, j=j: (layer, pt[i, p * pps + j], 0, 0, 0, 0))
    return pl.pallas_call(
        _page_rows_kernel,
        grid_spec=pltpu.PrefetchScalarGridSpec(
            num_scalar_prefetch=1,
            grid=(b, n_pages // pps),
            in_specs=[page_spec(j) for j in range(pps)],
            out_specs=pl.BlockSpec((1, pps * page, w), lambda i, p, pt: (i, p, 0))),
        out_shape=jax.ShapeDtypeStruct((b, n_pages * page, w), cache_t.dtype),
        compiler_params=pltpu.CompilerParams(dimension_semantics=("arbitrary", "arbitrary")),
        name="page_rows",
    )(page_table, *([cache_t] * pps))


SCAN_C = 64
SCAN_TT = 512
HALO = 8
GDN_GROUP = 2


def _dot_tn(a, b):
    return lax.dot_general(a, b, (((0,), (0,)), ((), ())), preferred_element_type=jnp.float32)


def _mx(x):
    return x.astype(MXU_DT)


def _dot3(a, b):
    a_hi, b_hi = _mx(a), _mx(b)
    a_lo = _mx(a - a_hi.astype(jnp.float32))
    b_lo = _mx(b - b_hi.astype(jnp.float32))
    return _dot(a_hi, b_hi) + (_dot(a_hi, b_lo) + _dot(a_lo, b_hi))


def _chunk_cumsum(x):
    c = x.shape[0]
    row = lax.broadcasted_iota(jnp.int32, x.shape, 0)
    d = 1
    while d < c:
        x = x + jnp.where(row >= d, pltpu.roll(x, d, 0), 0.0)
        d *= 2
    return x


def _silu(x):
    return x * jax.nn.sigmoid(x)


def _conv_tile(x, halo, w_ref):
    kw = w_ref.shape[0]
    r8 = lax.broadcasted_iota(jnp.int32, (HALO, x.shape[1]), 0)
    y = x * w_ref[kw - 1:kw, :]
    for d in range(1, kw):
        xs = pltpu.roll(x, d, 0)
        head = jnp.where(r8 < d, pltpu.roll(halo, d, 0), xs[0:HALO])
        xs = jnp.concatenate([head, xs[HALO:]], axis=0)
        y = y + xs * w_ref[kw - 1 - d:kw - d, :]
    return y


def _halo_spec(tt, width, col_block):
    return pl.BlockSpec((HALO, width), lambda i: (jnp.maximum(i * (tt // HALO) - 1, 0), col_block))


def _pad_buf(buf):
    return jnp.pad(buf.astype(jnp.float32), ((0, 0), (HALO - buf.shape[1], 0), (0, 0)))


def _hgrn_prep_kernel(q_ref, f_ref, v_ref, la_ref, l1_ref, lbc_ref, oi_ref, qb_ref, kb_ref, eb_ref):
    c = SCAN_C
    tt = q_ref.shape[0]
    row = lax.broadcasted_iota(jnp.int32, (c, LANE), 0)
    ti = lax.broadcasted_iota(jnp.int32, (c, c), 0)
    si = lax.broadcasted_iota(jnp.int32, (c, c), 1)
    for h in range(HG_HEADS):
        ls = slice(h * LANE, (h + 1) * LANE)
        la, l1, lbc = la_ref[:, ls], l1_ref[:, ls], lbc_ref[:, ls]
        for ci in range(tt // c):
            rs = slice(ci * c, (ci + 1) * c)
            q = _silu(q_ref[rs, ls])
            z = f_ref[rs, ls]
            v = v_ref[rs, ls]
            lsig = jnp.minimum(z, 0.0) - jnp.log1p(jnp.exp(-jnp.abs(z)))
            t2 = l1 + lsig
            hi = jnp.maximum(la, t2)
            logf = hi + jnp.log1p(jnp.exp(-jnp.abs(la - t2)))
            k = lbc * jax.nn.sigmoid(-z)
            b = _chunk_cumsum(logf)
            b_last = b[c - 1:c, :]
            qb_ref[rs, ls] = q * jnp.exp(b)
            kb_ref[rs, ls] = k * jnp.exp(b_last - b)
            eb_ref[ci:ci + 1, ls] = jnp.exp(b_last)
            att = jnp.zeros((c, c), jnp.float32)
            n = c // 2
            while n >= 8:
                blk = 2 * n
                ref_rows = jnp.concatenate(
                    [jnp.broadcast_to(b[j * blk + n - 1:j * blk + n, :], (blk, LANE)) for j in range(c // blk)], axis=0)
                upper = (row & (blk - 1)) >= n
                qs = q * jnp.exp(jnp.where(upper, b - ref_rows, 0.0))
                ks = k * jnp.exp(jnp.where(upper, 0.0, ref_rows - b))
                lvl = _dot_nt(_mx(qs), _mx(ks))
                ok = ((ti & ~(blk - 1)) == (si & ~(blk - 1))) & ((ti & (blk - 1)) >= n) & ((si & (blk - 1)) < n)
                att = att + jnp.where(ok, lvl, 0.0)
                n //= 2
            o = _dot(_mx(att), _mx(v))
            for d in range(8):
                if d == 0:
                    w = jnp.sum(q * k, axis=-1, keepdims=True)
                    o = o + w * v
                else:
                    e = jnp.exp(jnp.minimum(b - pltpu.roll(b, d, 0), 0.0))
                    w = jnp.sum(q * pltpu.roll(k, d, 0) * e, axis=-1, keepdims=True)
                    w = jnp.where((row[:, 0:1] & 7) >= d, w, 0.0)
                    o = o + w * pltpu.roll(v, d, 0)
            oi_ref[rs, ls] = o


def hgrn_prep(c_all, col_q, col_f, col_v, lb):
    m = c_all.shape[0]
    tt = SCAN_TT
    lbf = lb.reshape(1, BRANCH_W).astype(jnp.float32)
    la, l1, lbc = jnp.log(lbf), jnp.log1p(-lbf), 1.0 - lbf
    blk = lambda col: pl.BlockSpec((tt, BRANCH_W), lambda i, col=col: (i, col // BRANCH_W))
    vec = pl.BlockSpec((1, BRANCH_W), lambda i: (0, 0))
    out = pl.BlockSpec((tt, BRANCH_W), lambda i: (i, 0))
    return pl.pallas_call(
        _hgrn_prep_kernel,
        grid=(m // tt,),
        in_specs=[blk(col_q), blk(col_f), blk(col_v), vec, vec, vec],
        out_specs=[out, out, out, pl.BlockSpec((tt // SCAN_C, BRANCH_W), lambda i: (i, 0))],
        out_shape=[jax.ShapeDtypeStruct((m, BRANCH_W), jnp.float32)] * 3
                  + [jax.ShapeDtypeStruct((m // SCAN_C, BRANCH_W), jnp.float32)],
        compiler_params=pltpu.CompilerParams(dimension_semantics=("arbitrary",), vmem_limit_bytes=VMEM_LIMIT),
        name="hgrn_prep",
    )(c_all, c_all, c_all, la, l1, lbc)


def _hgrn_scan_kernel(oi_ref, qb_ref, kb_ref, eb_ref, v_ref, g_ref, nw_ref, s0_ref, o_ref, sT_out_ref, sT_ref):
    c = SCAN_C
    nb = oi_ref.shape[0]
    tt = oi_ref.shape[1]
    i = pl.program_id(0)

    @pl.when(i == 0)
    def _():
        sT_ref[...] = s0_ref[...]

    nw = nw_ref[...]
    for ci in range(tt // c):
        rs = slice(ci * c, (ci + 1) * c)
        for b in range(nb):
            for h in range(HG_HEADS):
                ls = slice(h * LANE, (h + 1) * LANE)
                sT = sT_ref[b, h]
                o = oi_ref[b, rs, ls] + _dot_nt(_mx(qb_ref[b, rs, ls]), _mx(sT))
                sT_ref[b, h] = eb_ref[b, ci:ci + 1, ls] * sT + _dot_tn(_mx(v_ref[b, rs, ls]), _mx(kb_ref[b, rs, ls]))
                y = o * lax.rsqrt(jnp.mean(o * o, axis=-1, keepdims=True) + EPS) * nw
                o_ref[b, rs, ls] = y * _silu(g_ref[b, rs, ls])

    @pl.when(i == pl.num_programs(0) - 1)
    def _():
        sT_out_ref[...] = sT_ref[...]


def hgrn_scan(oi, qb, kb, eb, c_all, col_v, col_g, norm_w, s0, batch, seq_len):
    tt = SCAN_TT
    n3 = lambda a: a.reshape(batch, seq_len, a.shape[-1])
    c3 = n3(c_all)
    tok = pl.BlockSpec((batch, tt, BRANCH_W), lambda i: (0, i, 0))
    ctok = lambda col: pl.BlockSpec((batch, tt, BRANCH_W), lambda i, col=col: (0, i, col // BRANCH_W))
    st = pl.BlockSpec((batch, HG_HEADS, LANE, LANE), lambda i: (0, 0, 0, 0))
    o, sT = pl.pallas_call(
        _hgrn_scan_kernel,
        grid=(seq_len // tt,),
        in_specs=[tok, tok, tok, pl.BlockSpec((batch, tt // SCAN_C, BRANCH_W), lambda i: (0, i, 0)),
                  ctok(col_v), ctok(col_g), pl.BlockSpec((1, LANE), lambda i: (0, 0)), st],
        out_specs=[tok, st],
        out_shape=[jax.ShapeDtypeStruct((batch, seq_len, BRANCH_W), jnp.float32),
                   jax.ShapeDtypeStruct((batch, HG_HEADS, LANE, LANE), jnp.float32)],
        scratch_shapes=[pltpu.VMEM((batch, HG_HEADS, LANE, LANE), jnp.float32)],
        compiler_params=pltpu.CompilerParams(dimension_semantics=("arbitrary",), vmem_limit_bytes=VMEM_LIMIT),
        name="hgrn_scan",
    )(n3(oi), n3(qb), n3(kb), eb.reshape(batch, seq_len // SCAN_C, BRANCH_W), c3, c3,
      norm_w.reshape(1, LANE), jnp.swapaxes(s0, -1, -2))
    return o.reshape(batch * seq_len, BRANCH_W), jnp.swapaxes(sT, -1, -2)


def _gdn_prep_kernel(x_ref, halo_ref, buf_ref, misc_ref, cw_ref, alog_ref, dtb_ref,
                     u_ref, w_ref, qg_ref, kg_ref, qk_ref, eg_ref, *, tiles_per_seq):
    c = SCAN_C
    tt = x_ref.shape[0]
    first = (pl.program_id(0) % tiles_per_seq) == 0
    halo = jnp.where(first, buf_ref[0], halo_ref[...])
    y = _silu(_conv_tile(x_ref[...], halo, cw_ref))
    misc = misc_ref[...]
    beta_all = jax.nn.sigmoid(misc)
    sp_in = misc + dtb_ref[...]
    sp = jnp.maximum(sp_in, 0.0) + jnp.log1p(jnp.exp(-jnp.abs(sp_in)))
    g_all = -jnp.exp(alog_ref[...]) * sp
    ti = lax.broadcasted_iota(jnp.int32, (c, c), 0)
    si = lax.broadcasted_iota(jnp.int32, (c, c), 1)
    eye = (ti == si).astype(jnp.float32)
    for cp in range(0, tt // c, GDN_GROUP):
        heads = []
        for ci in range(cp, cp + GDN_GROUP):
            rs = slice(ci * c, (ci + 1) * c)
            gam_all = _chunk_cumsum(g_all[rs])
            gam_t = gam_all.T
            for h in range(DN_HEADS):
                q = y[rs, h * LANE:(h + 1) * LANE]
                k = y[rs, BRANCH_W + h * LANE:BRANCH_W + (h + 1) * LANE]
                v = y[rs, 2 * BRANCH_W + h * LANE:2 * BRANCH_W + (h + 1) * LANE]
                q = q * lax.rsqrt(jnp.sum(q * q, axis=-1, keepdims=True) + EPS) * (DN_DK ** -0.5)
                k = k * lax.rsqrt(jnp.sum(k * k, axis=-1, keepdims=True) + EPS)
                beta = beta_all[rs, h:h + 1]
                gam = gam_all[:, DN_HEADS + h:DN_HEADS + h + 1]
                gam_r = gam_t[DN_HEADS + h:DN_HEADS + h + 1, :]
                decay = jnp.exp(jnp.where(si <= ti, gam - gam_r, NEG_INF))
                kb16 = _mx(k)
                kk = _dot_nt(kb16, kb16)
                a = jnp.where(si < ti, beta * kk * decay, 0.0)
                heads.append((ci, h, q, k, v, beta, gam, decay, kb16, a))
        tinvs = [eye] * len(heads)
        s = 1
        while s < c:
            blk = 2 * s
            off = (((ti & ~(blk - 1)) == (si & ~(blk - 1))) & ((ti & (blk - 1)) >= s) & ((si & (blk - 1)) < s))
            a_offs = [jnp.where(off, hd[-1], 0.0) for hd in heads]
            if s == 1:
                tinvs = [t - ao for t, ao in zip(tinvs, a_offs)]
            else:
                mids = [_dot3(t, ao) for t, ao in zip(tinvs, a_offs)]
                tinvs = [t - _dot3(md, t) for t, md in zip(tinvs, mids)]
            s = blk
        for (ci, h, q, k, v, beta, gam, decay, kb16, a), tinv in zip(heads, tinvs):
            rs = slice(ci * c, (ci + 1) * c)
            ls = slice(h * LANE, (h + 1) * LANE)
            e_gam = jnp.exp(gam)
            rhs = jnp.concatenate([v * beta, k * (beta * e_gam)], axis=1)
            sol = rhs + _dot3(tinv - eye, rhs)
            u_ref[rs, ls] = sol[:, :LANE]
            w_ref[rs, ls] = sol[:, LANE:]
            qg_ref[rs, ls] = q * e_gam
            g_last = gam[c - 1:c, :]
            kg_ref[rs, ls] = k * jnp.exp(g_last - gam)
            qk_ref[rs, h * c:(h + 1) * c] = _dot_nt(_mx(q), kb16) * decay
            eg_ref[ci:ci + 1, ls] = jnp.broadcast_to(jnp.exp(g_last), (1, LANE))


def gdn_prep(c_all, col_x, col_misc, conv_buf, conv_w, a_log, dt_bias, batch, seq_len):
    m = c_all.shape[0]
    tt = SCAN_TT
    tps = seq_len // tt
    xw = 3 * BRANCH_W
    lanes = jnp.zeros((1, LANE), jnp.float32)
    alog = lanes.at[0, DN_HEADS:2 * DN_HEADS].set(a_log.astype(jnp.float32))
    dtb = lanes.at[0, DN_HEADS:2 * DN_HEADS].set(dt_bias.astype(jnp.float32))
    out = pl.BlockSpec((tt, BRANCH_W), lambda i: (i, 0))
    vec = pl.BlockSpec((1, LANE), lambda i: (0, 0))
    return pl.pallas_call(
        functools.partial(_gdn_prep_kernel, tiles_per_seq=tps),
        grid=(m // tt,),
        in_specs=[pl.BlockSpec((tt, xw), lambda i: (i, col_x // xw)),
                  _halo_spec(tt, xw, col_x // xw),
                  pl.BlockSpec((1, HALO, xw), lambda i: (i // tps, 0, 0)),
                  pl.BlockSpec((tt, LANE), lambda i: (i, col_misc // LANE)),
                  pl.BlockSpec((DN_CONV, xw), lambda i: (0, 0)), vec, vec],
        out_specs=[out, out, out, out, pl.BlockSpec((tt, DN_HEADS * SCAN_C), lambda i: (i, 0)),
                   pl.BlockSpec((tt // SCAN_C, BRANCH_W), lambda i: (i, 0))],
        out_shape=[jax.ShapeDtypeStruct((m, BRANCH_W), jnp.float32)] * 4
                  + [jax.ShapeDtypeStruct((m, DN_HEADS * SCAN_C), jnp.float32),
                     jax.ShapeDtypeStruct((m // SCAN_C, BRANCH_W), jnp.float32)],
        compiler_params=pltpu.CompilerParams(dimension_semantics=("arbitrary",), vmem_limit_bytes=VMEM_LIMIT),
        name="gdn_prep",
    )(c_all, c_all, _pad_buf(conv_buf), c_all, conv_w.astype(jnp.float32), alog, dtb)


def _gdn_scan_kernel(u_ref, w_ref, qg_ref, kg_ref, qk_ref, eg_ref, z_ref, nw_ref, s0_ref, o_ref, s_out_ref, s_ref):
    c = SCAN_C
    nb = u_ref.shape[0]
    tt = u_ref.shape[1]
    i = pl.program_id(0)

    @pl.when(i == 0)
    def _():
        s_ref[...] = s0_ref[...]

    nw = nw_ref[...]
    for ci in range(tt // c):
        rs = slice(ci * c, (ci + 1) * c)
        for b in range(nb):
            for h in range(DN_HEADS):
                ls = slice(h * LANE, (h + 1) * LANE)
                s = s_ref[b, h]
                s16 = _mx(s)
                both = _dot(_mx(jnp.concatenate([qg_ref[b, rs, ls], w_ref[b, rs, ls]], axis=0)), s16)
                v_new = u_ref[b, rs, ls] - both[c:]
                v16 = _mx(v_new)
                o = both[:c] + _dot(_mx(qk_ref[b, rs, h * c:(h + 1) * c]), v16)
                s_ref[b, h] = eg_ref[b, ci:ci + 1, ls][:, 0:1] * s + _dot_tn(_mx(kg_ref[b, rs, ls]), v16)
                y = o * lax.rsqrt(jnp.mean(o * o, axis=-1, keepdims=True) + EPS) * nw
                o_ref[b, rs, ls] = y * _silu(z_ref[b, rs, ls])

    @pl.when(i == pl.num_programs(0) - 1)
    def _():
        s_out_ref[...] = s_ref[...]


def gdn_scan(u, w, qg, kg, qk, eg, c_all, col_z, norm_w, s0, batch, seq_len):
    tt = SCAN_TT
    n3 = lambda a: a.reshape(batch, seq_len, a.shape[-1])
    tok = pl.BlockSpec((batch, tt, BRANCH_W), lambda i: (0, i, 0))
    st = pl.BlockSpec((batch, DN_HEADS, LANE, LANE), lambda i: (0, 0, 0, 0))
    o, s = pl.pallas_call(
        _gdn_scan_kernel,
        grid=(seq_len // tt,),
        in_specs=[tok, tok, tok, tok, pl.BlockSpec((batch, tt, DN_HEADS * SCAN_C), lambda i: (0, i, 0)),
                  pl.BlockSpec((batch, tt // SCAN_C, BRANCH_W), lambda i: (0, i, 0)),
                  pl.BlockSpec((batch, tt, BRANCH_W), lambda i: (0, i, col_z // BRANCH_W)),
                  pl.BlockSpec((1, LANE), lambda i: (0, 0)), st],
        out_specs=[tok, st],
        out_shape=[jax.ShapeDtypeStruct((batch, seq_len, BRANCH_W), jnp.float32),
                   jax.ShapeDtypeStruct((batch, DN_HEADS, LANE, LANE), jnp.float32)],
        scratch_shapes=[pltpu.VMEM((batch, DN_HEADS, LANE, LANE), jnp.float32)],
        compiler_params=pltpu.CompilerParams(dimension_semantics=("arbitrary",), vmem_limit_bytes=VMEM_LIMIT),
        name="gdn_scan",
    )(n3(u), n3(w), n3(qg), n3(kg), n3(qk), eg.reshape(batch, seq_len // SCAN_C, BRANCH_W), n3(c_all),
      norm_w.reshape(1, LANE), s0.astype(jnp.float32))
    return o.reshape(batch * seq_len, BRANCH_W), s


def _sc_branch_kernel(x_ref, halo_ref, buf_ref, w_ref, o_ref, *, tiles_per_seq):
    first = (pl.program_id(0) % tiles_per_seq) == 0
    x = x_ref[...]
    hr = halo_ref[...]
    halo = jnp.where(first, buf_ref[0], hr[:, SC_W:2 * SC_W] * hr[:, 2 * SC_W:3 * SC_W])
    o_ref[...] = x[:, 0:SC_W] * _conv_tile(x[:, SC_W:2 * SC_W] * x[:, 2 * SC_W:3 * SC_W], halo, w_ref)


def sc_branch(c_all, col_x, conv_buf, conv_w, seq_len):
    m = c_all.shape[0]
    tt = SCAN_TT
    tps = seq_len // tt
    xw = 3 * SC_W
    return pl.pallas_call(
        functools.partial(_sc_branch_kernel, tiles_per_seq=tps),
        grid=(m // tt,),
        in_specs=[pl.BlockSpec((tt, xw), lambda i: (i, col_x // xw)),
                  _halo_spec(tt, xw, col_x // xw),
                  pl.BlockSpec((1, HALO, SC_W), lambda i: (i // tps, 0, 0)),
                  pl.BlockSpec((SC_CONV, SC_W), lambda i: (0, 0))],
        out_specs=pl.BlockSpec((tt, SC_W), lambda i: (i, 0)),
        out_shape=jax.ShapeDtypeStruct((m, SC_W), jnp.float32),
        compiler_params=pltpu.CompilerParams(dimension_semantics=("arbitrary",), vmem_limit_bytes=VMEM_LIMIT),
        name="sc_branch",
    )(c_all, c_all, _pad_buf(conv_buf), conv_w.astype(jnp.float32))


def rms_norm(x, g):
    xf = x.astype(jnp.float32)
    y = xf * lax.rsqrt(jnp.mean(xf * xf, axis=-1, keepdims=True) + EPS)
    return (y * g.astype(jnp.float32)).astype(x.dtype)


def l2_normalize(x):
    xf = x.astype(jnp.float32)
    return xf * lax.rsqrt(jnp.sum(xf * xf, axis=-1, keepdims=True) + EPS)


def causal_conv(x, buf, w):
    k_w = w.shape[0]
    seq_len = x.shape[1]
    xp = jnp.concatenate([buf.astype(x.dtype), x], axis=1)
    y = sum(xp[:, j:j + seq_len] * w[j] for j in range(k_w))
    return y, xp[:, seq_len:]


def to_chunks(a, c):
    b, seq_len = a.shape[:2]
    n = -(-seq_len // c)
    a = jnp.pad(a, [(0, 0), (0, n * c - seq_len)] + [(0, 0)] * (a.ndim - 2))
    return jnp.moveaxis(a.reshape((b, n, c) + a.shape[2:]), 1, 0)


def from_chunks(a, seq_len):
    n, b, c = a.shape[:3]
    return jnp.moveaxis(a, 0, 1).reshape((b, n * c) + a.shape[3:])[:, :seq_len]


def hgrn2_scan(q, k, v, logf, s0):
    seq_len = q.shape[1]
    c = min(HG_CHUNK, seq_len)
    tri = jnp.tril(jnp.ones((c, c), bool))[None, :, :, None, None]

    def step(s, inp):
        qc, kc, vc, gc = inp
        b = jnp.cumsum(gc, axis=1)
        decay = jnp.exp(jnp.where(tri, b[:, :, None] - b[:, None], -jnp.inf))
        att = jnp.einsum('bthk,btshk->btsh', qc, decay * kc[:, None])
        o = jnp.einsum('btsh,bshv->bthv', att, vc) + jnp.einsum('bthk,bhkv->bthv', qc * jnp.exp(b), s)
        b_last = b[:, -1]
        s = jnp.exp(b_last)[..., None] * s + jnp.einsum('bshk,bshv->bhkv', kc * jnp.exp(b_last[:, None] - b), vc)
        return s, o

    xs = tuple(to_chunks(a.astype(jnp.float32), c) for a in (q, k, v, logf))
    s, o = lax.scan(step, s0.astype(jnp.float32), xs)
    return from_chunks(o, seq_len), s


def gated_delta_scan(q, k, v, beta, g, s0):
    seq_len = q.shape[1]
    c = min(DN_CHUNK, seq_len)
    incl = jnp.tril(jnp.ones((c, c), bool))
    strict = jnp.tril(jnp.ones((c, c), bool), -1)
    eye = jnp.eye(c, dtype=jnp.float32)

    def step(s, inp):
        qc, kc, vc, bc, gc = inp
        qh, kh, vh = (jnp.swapaxes(a, 1, 2) for a in (qc, kc, vc))
        bh = jnp.swapaxes(bc, 1, 2)
        gam = jnp.cumsum(jnp.swapaxes(gc, 1, 2), axis=-1)
        decay = jnp.exp(jnp.where(incl, gam[..., :, None] - gam[..., None, :], -jnp.inf))
        kk = jnp.einsum('bhtk,bhsk->bhts', kh, kh)
        t_mat = eye + jnp.where(strict, bh[..., :, None] * kk * decay, 0.0)
        u = lax.linalg.triangular_solve(t_mat, vh * bh[..., None], left_side=True, lower=True)
        w = lax.linalg.triangular_solve(t_mat, kh * (bh * jnp.exp(gam))[..., None], left_side=True, lower=True)
        v_new = u - jnp.einsum('bhtk,bhkv->bhtv', w, s)
        qk = jnp.einsum('bhtk,bhsk->bhts', qh, kh) * decay
        o = jnp.einsum('bhtk,bhkv->bhtv', qh * jnp.exp(gam)[..., None], s) + jnp.einsum('bhts,bhsv->bhtv', qk, v_new)
        g_last = gam[..., -1]
        s = jnp.exp(g_last)[..., None, None] * s + jnp.einsum('bhtk,bhtv->bhkv', kh * jnp.exp(g_last[..., None] - gam)[..., None], v_new)
        return s, jnp.swapaxes(o, 1, 2)

    xs = tuple(to_chunks(a.astype(jnp.float32), c) for a in (q, k, v, beta, g))
    s, o = lax.scan(step, s0.astype(jnp.float32), xs)
    return from_chunks(o, seq_len), s


def trunk_layer(x, start, hg_s, dn_s, dn_buf, sc_buf, past_rows, win_buf, w):
    b, seq_len, _ = x.shape
    f32 = jnp.float32
    x2 = x.reshape(b * seq_len, D_MODEL)
    m = b * seq_len
    c2 = _norm_proj(x2, w["norm_mix"], w["w_in"])
    c_all = c2.reshape(b, seq_len, N_IN_PAD)
    lb = w["lb"]
    if past_rows is None:
        oi, qb, kb, eb = hgrn_prep(c2, C_OFF["hg_q"], C_OFF["hg_f"], C_OFF["hg_i"], lb)
        o_a, hg_s = hgrn_scan(oi, qb, kb, eb, c2, C_OFF["hg_i"], C_OFF["hg_g"], w["hg_norm"], hg_s, b, seq_len)
        u, wy, qg, kg, qk, eg = gdn_prep(c2, C_OFF["dn_qkv"], C_OFF["misc"], dn_buf, w["dn_conv"], w["dn_a_log"],
                                         w["dn_dt_bias"], b, seq_len)
        o_b, dn_s = gdn_scan(u, wy, qg, kg, qk, eg, c2, C_OFF["dn_z"], w["dn_norm"], dn_s, b, seq_len)
        o_c = sc_branch(c2, C_OFF["sc_bch"], sc_buf, w["sc_conv"], seq_len)
        tail = c_all[:, seq_len - (DN_CONV - 1):]
        dn_buf = tail[..., C_OFF["dn_qkv"]:C_OFF["dn_qkv"] + 3 * BRANCH_W]
        sc_t = tail[:, DN_CONV - SC_CONV:, C_OFF["sc_bch"] + SC_W:C_OFF["sc_bch"] + 3 * SC_W]
        sc_buf = sc_t[..., :SC_W] * sc_t[..., SC_W:]
    else:
        o_a, o_b, o_c, (hg_s, dn_s, dn_buf, sc_buf) = cached_mixers(x, c_all, hg_s, dn_s, dn_buf, sc_buf, w)
    tabs = rope_tables(start + jnp.arange(seq_len))
    rows, win_rows, kvb = nsa_prep(c2, C_OFF["nsa_kv"], tabs, seq_len)
    rows = rows.reshape(b, seq_len, 4 * LANE)
    win_rows = win_rows.reshape(b, seq_len, 2 * LANE)
    kvb = kvb.reshape(b, seq_len, 4 * LANE)
    nsa_cols = (c2, C_OFF["nsa_q"], C_OFF["misc"], tabs)
    cmp_w = (w["cmp_pos"], w["cmp_w1"], w["cmp_w2"])
    if past_rows is None:
        ckv = nsa_compress(rows, *cmp_w)
        no_new = jnp.zeros((b, LANE, 2 * LANE), MXU_DT)
        o_d = nsa_attend(*nsa_cols, ckv, kvb, (0, 1), no_new, 0, kvb, (2, 3), 0, b, seq_len, 0)
        new_win = win_rows[:, max(seq_len - WINDOW, 0):]
    else:
        n_past = past_rows.shape[1]
        assert n_past % CMP_STRIDE == 0 and seq_len < CMP_STRIDE
        past = past_rows.reshape(b, n_past, 4 * LANE)
        ckv = nsa_compress(past, *cmp_w)
        new_kv = jnp.pad(kvb[:, :, 0:2 * LANE], ((0, 0), (0, LANE - seq_len), (0, 0)))
        nb = win_buf.shape[1]
        w_all = jnp.concatenate([win_buf.reshape(b, nb, 2 * LANE).astype(jnp.float32), win_rows], axis=1)
        w_pad = jnp.pad(w_all, ((0, 0), (0, max(NSA_WSPAN - nb - seq_len, 0)), (0, 0)))
        o_d = nsa_attend(*nsa_cols, ckv, past, (2, 3), new_kv, seq_len, w_pad, (0, 1), start - nb, b, seq_len, start)
        new_win = w_all[:, seq_len:]
    rows = rows.reshape(b, seq_len, 4, NSA_KV_HEADS, NSA_HD)
    new_win = new_win.reshape(b, new_win.shape[1], 2, NSA_KV_HEADS, NSA_HD)
    branches = [o_a.reshape(m, BRANCH_W), o_b.reshape(m, BRANCH_W), o_c.reshape(m, BRANCH_W), o_d.reshape(m, BRANCH_W)]
    x2 = _merge_out(x2, branches, c2, C_OFF["merge_gate"], w["w_branch"], w["w_out"])
    x2 = _mlp(x2, w["norm_mlp"], w["w_up"], w["w_down"])
    return x2.reshape(b, seq_len, D_MODEL), (hg_s, dn_s, dn_buf, sc_buf, new_win, rows)


def cached_mixers(x, c_all, hg_s, dn_s, dn_buf, sc_buf, w):
    b, seq_len, _ = x.shape
    f32 = jnp.float32
    c = {n: c_all[..., C_OFF[n]:C_OFF[n] + wd] for n, wd in C_ORDER}
    for n, o0, wd in MISC_COLS:
        c[n] = c["misc"][..., o0:o0 + wd]
    lb = w["lb"]
    hq = jax.nn.silu(c["hg_q"]).reshape(b, seq_len, HG_HEADS, HG_DK)
    z = c["hg_f"].astype(f32).reshape(b, seq_len, HG_HEADS, HG_DK)
    logf = jnp.logaddexp(jnp.log(lb), jnp.log1p(-lb) + jax.nn.log_sigmoid(z))
    hk = (1.0 - lb) * jax.nn.sigmoid(-z)
    hv = c["hg_i"].reshape(b, seq_len, HG_HEADS, HG_DV)
    o_a, hg_s = hgrn2_scan(hq, hk, hv, logf, hg_s)
    o_a = rms_norm(o_a.astype(x.dtype), w["hg_norm"]) * jax.nn.silu(c["hg_g"].reshape(b, seq_len, HG_HEADS, HG_DV))
    qkv, dn_buf = causal_conv(c["dn_qkv"], dn_buf, w["dn_conv"])
    dq, dk, dv = jnp.split(jax.nn.silu(qkv), 3, axis=-1)
    dq = l2_normalize(dq.reshape(b, seq_len, DN_HEADS, DN_DK)) * DN_DK ** -0.5
    dk = l2_normalize(dk.reshape(b, seq_len, DN_HEADS, DN_DK))
    beta = jax.nn.sigmoid(c["dn_b"].astype(f32))
    g = -jnp.exp(w["dn_a_log"].astype(f32)) * jax.nn.softplus(c["dn_a"].astype(f32) + w["dn_dt_bias"])
    o_b, dn_s = gated_delta_scan(dq, dk, dv.reshape(b, seq_len, DN_HEADS, DN_DV), beta, g, dn_s)
    o_b = rms_norm(o_b.astype(x.dtype), w["dn_norm"]) * jax.nn.silu(c["dn_z"].reshape(b, seq_len, DN_HEADS, DN_DV))
    gb, gc, hx = jnp.split(c["sc_bch"], 3, axis=-1)
    conv, sc_buf = causal_conv(gc * hx, sc_buf, w["sc_conv"])
    o_c = gb * conv
    return o_a, o_b, o_c, (hg_s, dn_s, dn_buf, sc_buf)


def kernel(x_prompt, x_sample, state_hgrn, state_dn, state_dn_conv, state_sc_conv, state_win_kv, cache_kv, page_table, norm_mix, norm_mlp, norm_final, w_in, hg_lb_logits, hg_norm, dn_conv, dn_a_log, dn_dt_bias, dn_norm, sc_conv, cmp_pos, cmp_w1, cmp_w2, w_branch, w_out, w_up, w_down):
    f32 = jnp.float32
    bf16 = jnp.bfloat16
    lbs = jnp.cumsum(jax.nn.softmax(hg_lb_logits.astype(f32), axis=0), axis=0)
    lbs = lbs - lbs[:1]
    w_in_b = _permute_w_in(w_in).astype(bf16)
    w_branch_b, w_out_b, w_up_b, w_down_b = (a.astype(bf16) for a in (w_branch, w_out, w_up, w_down))

    def layer_w(l):
        return dict(norm_mix=norm_mix[l], norm_mlp=norm_mlp[l], w_in=w_in_b[l], lb=lbs[l], hg_norm=hg_norm[l],
                    dn_conv=dn_conv[l], dn_a_log=dn_a_log[l], dn_dt_bias=dn_dt_bias[l], dn_norm=dn_norm[l],
                    sc_conv=sc_conv[l], cmp_pos=cmp_pos[l], cmp_w1=cmp_w1[l], cmp_w2=cmp_w2[l],
                    w_branch=w_branch_b[l], w_out=w_out_b[l], w_up=w_up_b[l], w_down=w_down_b[l])

    bp = x_prompt.shape[0]
    yp = x_prompt
    p_st = []
    for l in range(DEPTH):
        yp, st = trunk_layer(yp, 0,
                             jnp.zeros((bp, HG_HEADS, HG_DK, HG_DV), f32),
                             jnp.zeros((bp, DN_HEADS, DN_DK, DN_DV), f32),
                             jnp.zeros((bp, DN_CONV - 1, 3 * BRANCH_W), x_prompt.dtype),
                             jnp.zeros((bp, SC_CONV - 1, SC_W), x_prompt.dtype),
                             None, None, layer_w(l))
        p_st.append(st)
    ys = x_sample
    s_st = []
    cache_t = jnp.transpose(cache_kv, (0, 1, 3, 4, 5, 2))
    for l in range(DEPTH):
        past = page_rows(cache_t, page_table, l)
        ys, st = trunk_layer(ys, past.shape[1], state_hgrn[l], state_dn[l], state_dn_conv[l], state_sc_conv[l],
                             past, state_win_kv[l], layer_w(l))
        s_st.append(st)
    p = [jnp.stack([st[i] for st in p_st]) for i in range(6)]
    s = [jnp.stack([st[i] for st in s_st]) for i in range(6)]
    return (rms_norm(yp, norm_final), rms_norm(ys, norm_final),
            p[0], p[1], p[2], p[3], p[4], p[5],
            s[0], s[1], s[2], s[3], s[4], s[5])
```

```python
import math, functools
import jax, jax.numpy as jnp
from jax import lax
import numpy as np
from jax.experimental import pallas as pl
from jax.experimental.pallas import tpu as pltpu

D_MODEL = 1024
DEPTH = 4
PAGE_SIZE = 128
N_BRANCH = 4
BRANCH_W = D_MODEL // 2
HG_HEADS = 4
HG_DK = BRANCH_W // HG_HEADS
HG_DV = BRANCH_W // HG_HEADS
HG_CHUNK = 64
DN_HEADS = 4
DN_DK = BRANCH_W // DN_HEADS
DN_DV = BRANCH_W // DN_HEADS
DN_CONV = 4
DN_CHUNK = 64
SC_W = BRANCH_W
SC_CONV = 3
NSA_HEADS = 8
NSA_KV_HEADS = 2
NSA_HD = BRANCH_W // NSA_HEADS
NSA_GROUP = NSA_HEADS // NSA_KV_HEADS
ROPE_DIM = NSA_HD // 4
ROPE_THETA = 500000.0
CMP_BLOCK = 32
CMP_STRIDE = 16
CMP_HIDDEN = 4 * NSA_HD
SEL_BLOCK = 64
SEL_TOPN = 16
WINDOW = 512
Q_BLOCK = 128
D_FF = 4 * D_MODEL
EPS = 1e-6

IN_SPLITS = (
    ("hg_q", BRANCH_W), ("hg_f", BRANCH_W), ("hg_i", BRANCH_W), ("hg_g", BRANCH_W),
    ("dn_qkv", 3 * BRANCH_W), ("dn_b", DN_HEADS), ("dn_a", DN_HEADS), ("dn_z", BRANCH_W),
    ("sc_bch", 3 * SC_W),
    ("nsa_q", NSA_HEADS * NSA_HD), ("nsa_kv", 6 * NSA_KV_HEADS * NSA_HD), ("nsa_gate", 3 * NSA_HEADS),
    ("merge_gate", N_BRANCH * D_MODEL),
)
IN_NAMES = tuple(n for n, _ in IN_SPLITS)
IN_CUTS = tuple(int(c) for c in np.cumsum([s for _, s in IN_SPLITS])[:-1])
N_IN = sum(s for _, s in IN_SPLITS)

LANE = 128
PROJ_TM = 1024
PROJ_TN = 1024
VMEM_LIMIT = 48 * 1024 * 1024
MXU_DT = jnp.bfloat16

MISC_W = 2 * LANE
C_ORDER = (("merge_gate", N_BRANCH * D_MODEL), ("hg_q", BRANCH_W), ("hg_f", BRANCH_W), ("hg_i", BRANCH_W),
           ("hg_g", BRANCH_W), ("dn_qkv", 3 * BRANCH_W), ("sc_bch", 3 * SC_W), ("dn_z", BRANCH_W),
           ("nsa_q", NSA_HEADS * NSA_HD), ("nsa_kv", 6 * NSA_KV_HEADS * NSA_HD), ("misc", MISC_W))
C_OFF = {}
_o = 0
for _n, _w in C_ORDER:
    C_OFF[_n] = _o
    _o += _w
N_IN_PAD = _o
assert N_IN_PAD % PROJ_TN == 0
MISC_COLS = (("dn_b", 0, DN_HEADS), ("dn_a", DN_HEADS, DN_HEADS), ("nsa_gate", 2 * DN_HEADS, 3 * NSA_HEADS))
NEG_INF = float("-inf")


def _permute_w_in(w_in):
    src = dict(zip(IN_NAMES, jnp.split(w_in, IN_CUTS, axis=-1)))
    misc = jnp.concatenate([src[n] for n, _, _ in MISC_COLS], axis=-1)
    src["misc"] = jnp.pad(misc, ((0, 0), (0, 0), (0, MISC_W - misc.shape[-1])))
    return jnp.concatenate([src[n] for n, _ in C_ORDER], axis=-1)


def _row_tile(m):
    return 1024 if m % 1024 == 0 else m


def _norm_proj_kernel(x_ref, g_ref, w_ref, o_ref, h_ref):
    @pl.when(pl.program_id(1) == 0)
    def _():
        x = x_ref[...]
        y = x * lax.rsqrt(jnp.mean(x * x, axis=-1, keepdims=True) + EPS)
        h_ref[...] = (y * g_ref[...]).astype(jnp.bfloat16)

    o_ref[...] = jnp.dot(h_ref[...], w_ref[...], preferred_element_type=jnp.float32)


def _norm_proj(x, g, w_bf16):
    m, d = x.shape
    n = w_bf16.shape[1]
    tm = PROJ_TM if m % PROJ_TM == 0 else m
    return pl.pallas_call(
        _norm_proj_kernel,
        grid=(m // tm, n // PROJ_TN),
        in_specs=[pl.BlockSpec((tm, d), lambda i, j: (i, 0)),
                  pl.BlockSpec((1, d), lambda i, j: (0, 0)),
                  pl.BlockSpec((d, PROJ_TN), lambda i, j: (0, j))],
        out_specs=pl.BlockSpec((tm, PROJ_TN), lambda i, j: (i, j)),
        out_shape=jax.ShapeDtypeStruct((m, n), jnp.float32),
        scratch_shapes=[pltpu.VMEM((tm, d), jnp.bfloat16)],
        compiler_params=pltpu.CompilerParams(dimension_semantics=("arbitrary", "arbitrary"),
                                             vmem_limit_bytes=VMEM_LIMIT),
        name="norm_proj",
    )(x, g.reshape(1, d), w_bf16)


def _merge_kernel(x_ref, ba_ref, bb_ref, bc_ref, bd_ref, gate_ref, wb_ref, wo_ref, o_ref):
    acc = None
    for n, b_ref in enumerate((ba_ref, bb_ref, bc_ref, bd_ref)):
        p = jnp.dot(b_ref[...].astype(jnp.bfloat16), wb_ref[n], preferred_element_type=jnp.float32)
        t = jax.nn.sigmoid(gate_ref[:, n * D_MODEL:(n + 1) * D_MODEL]) * p
        acc = t if acc is None else acc + t
    o_ref[...] = x_ref[...] + jnp.dot(acc.astype(jnp.bfloat16), wo_ref[...], preferred_element_type=jnp.float32)


def _merge_out(x, branches, c_all, gate_col, wb_bf16, wo_bf16):
    m, d = x.shape
    tm = 256 if m % 256 == 0 else m
    row = lambda w: pl.BlockSpec((tm, w), lambda i: (i, 0))
    return pl.pallas_call(
        _merge_kernel,
        grid=(m // tm,),
        in_specs=[row(d)] + [row(BRANCH_W)] * N_BRANCH + [
                  pl.BlockSpec((tm, N_BRANCH * d), lambda i: (i, gate_col // (N_BRANCH * d))),
                  pl.BlockSpec((N_BRANCH, BRANCH_W, d), lambda i: (0, 0, 0)),
                  pl.BlockSpec((d, d), lambda i: (0, 0))],
        out_specs=row(d),
        out_shape=jax.ShapeDtypeStruct((m, d), jnp.float32),
        compiler_params=pltpu.CompilerParams(dimension_semantics=("arbitrary",),
                                             vmem_limit_bytes=VMEM_LIMIT),
        name="merge_out",
    )(x, *branches, c_all, wb_bf16, wo_bf16)


MLP_TF = 1024


def _mlp_kernel(x_ref, g_ref, wu_ref, wd_ref, o_ref, h_ref, acc_ref):
    j = pl.program_id(1)

    @pl.when(j == 0)
    def _():
        x = x_ref[...]
        y = x * lax.rsqrt(jnp.mean(x * x, axis=-1, keepdims=True) + EPS)
        h_ref[...] = (y * g_ref[...]).astype(jnp.bfloat16)
        acc_ref[...] = jnp.zeros_like(acc_ref)

    u = jnp.maximum(jnp.dot(h_ref[...], wu_ref[...], preferred_element_type=jnp.float32), 0.0)
    acc_ref[...] += jnp.dot((u * u).astype(jnp.bfloat16), wd_ref[...], preferred_element_type=jnp.float32)

    @pl.when(j == pl.num_programs(1) - 1)
    def _():
        o_ref[...] = x_ref[...] + acc_ref[...]


def _mlp(x, g, wu_bf16, wd_bf16):
    m, d = x.shape
    f = wu_bf16.shape[1]
    tm = _row_tile(m)
    return pl.pallas_call(
        _mlp_kernel,
        grid=(m // tm, f // MLP_TF),
        in_specs=[pl.BlockSpec((tm, d), lambda i, j: (i, 0)),
                  pl.BlockSpec((1, d), lambda i, j: (0, 0)),
                  pl.BlockSpec((d, MLP_TF), lambda i, j: (0, j)),
                  pl.BlockSpec((MLP_TF, d), lambda i, j: (j, 0))],
        out_specs=pl.BlockSpec((tm, d), lambda i, j: (i, 0)),
        out_shape=jax.ShapeDtypeStruct((m, d), jnp.float32),
        scratch_shapes=[pltpu.VMEM((tm, d), jnp.bfloat16), pltpu.VMEM((tm, d), jnp.float32)],
        compiler_params=pltpu.CompilerParams(dimension_semantics=("arbitrary", "arbitrary"),
                                             vmem_limit_bytes=VMEM_LIMIT),
        name="mlp",
    )(x, g.reshape(1, d), wu_bf16, wd_bf16)


NSA_TQ = 128
NSA_TK = 1024
NSA_ROWS = NSA_HEADS * NSA_TQ
NSA_WSPAN = WINDOW + NSA_TQ


def _dot(a, b, **kw):
    return jnp.dot(a, b, preferred_element_type=jnp.float32, **kw)


def _dot_nt(a, b):
    return lax.dot_general(a, b, (((1,), (1,)), ((), ())), preferred_element_type=jnp.float32)


def rope_tables(pos):
    half = ROPE_DIM // 2
    inv_freq = ROPE_THETA ** (-jnp.arange(half, dtype=jnp.float32) / half)
    ang = pos.astype(jnp.float32)[:, None] * inv_freq
    cos, sin = jnp.cos(ang), jnp.sin(ang)
    n = pos.shape[0]
    one = jnp.ones((n, NSA_HD - ROPE_DIM), jnp.float32)
    zero = jnp.zeros((n, NSA_HD - ROPE_DIM), jnp.float32)
    z8 = jnp.zeros((n, half), jnp.float32)
    c = jnp.concatenate([cos, cos, one], axis=1)
    s1 = jnp.concatenate([-sin, z8, zero], axis=1)
    s2 = jnp.concatenate([z8, sin, zero], axis=1)
    two = lambda a: jnp.concatenate([a, a], axis=1)
    return two(c), two(s1), two(s2)


def _rope(x, c, s1, s2):
    n = x.shape[-1]
    return x * c + pltpu.roll(x, n - ROPE_DIM // 2, 1) * s1 + pltpu.roll(x, ROPE_DIM // 2, 1) * s2


def _nsa_prep_kernel(kv0_ref, kv1_ref, kv2_ref, c_ref, s1_ref, s2_ref, rows_ref, win_ref, kvb_ref):
    c, s1, s2 = c_ref[...], s1_ref[...], s2_ref[...]
    cmp_kv = kv0_ref[...]
    sel = kv1_ref[...]
    wnd = kv2_ref[...]
    ks = _rope(sel[:, :LANE], c, s1, s2)
    kw = _rope(wnd[:, :LANE], c, s1, s2)
    rows_ref[:, 0:2 * LANE] = cmp_kv
    rows_ref[:, 2 * LANE:3 * LANE] = ks
    rows_ref[:, 3 * LANE:4 * LANE] = sel[:, LANE:]
    win_ref[:, 0:LANE] = kw
    win_ref[:, LANE:2 * LANE] = wnd[:, LANE:]
    kvb_ref[:, 0:LANE] = ks.astype(MXU_DT)
    kvb_ref[:, LANE:2 * LANE] = sel[:, LANE:].astype(MXU_DT)
    kvb_ref[:, 2 * LANE:3 * LANE] = kw.astype(MXU_DT)
    kvb_ref[:, 3 * LANE:4 * LANE] = wnd[:, LANE:].astype(MXU_DT)


def nsa_prep(c_all, kv_col0, tabs, seq_len):
    m = c_all.shape[0]
    tm = 512 if seq_len % 512 == 0 else seq_len
    nlt = seq_len // tm
    cb = kv_col0 // (2 * LANE)
    kv_spec = lambda k: pl.BlockSpec((tm, 2 * LANE), lambda i, k=k: (i, cb + k))
    tab_spec = pl.BlockSpec((tm, LANE), lambda i: (i % nlt, 0))
    return pl.pallas_call(
        _nsa_prep_kernel,
        grid=(m // tm,),
        in_specs=[kv_spec(0), kv_spec(1), kv_spec(2), tab_spec, tab_spec, tab_spec],
        out_specs=[pl.BlockSpec((tm, 4 * LANE), lambda i: (i, 0)),
                   pl.BlockSpec((tm, 2 * LANE), lambda i: (i, 0)),
                   pl.BlockSpec((tm, 4 * LANE), lambda i: (i, 0))],
        out_shape=[jax.ShapeDtypeStruct((m, 4 * LANE), jnp.float32),
                   jax.ShapeDtypeStruct((m, 2 * LANE), jnp.float32),
                   jax.ShapeDtypeStruct((m, 4 * LANE), MXU_DT)],
        compiler_params=pltpu.CompilerParams(dimension_semantics=("arbitrary",), vmem_limit_bytes=VMEM_LIMIT),
        name="nsa_prep",
    )(c_all, c_all, c_all, *tabs)


def _nsa_compress_kernel(x_ref, pos_ref, w1_ref, w2_ref, o_ref):
    nh = x_ref.shape[0] // CMP_STRIDE
    top = jnp.zeros((nh, NSA_KV_HEADS * CMP_HIDDEN), jnp.float32)
    bot = jnp.zeros((nh, NSA_KV_HEADS * CMP_HIDDEN), jnp.float32)
    for t in range(CMP_STRIDE):
        x = x_ref[pl.ds(t, nh, stride=CMP_STRIDE), :]
        top = top + _dot((x + pos_ref[0, t:t + 1, :]).astype(MXU_DT), w1_ref[0, 0, t])
        bot = bot + _dot((x + pos_ref[0, CMP_STRIDE + t:CMP_STRIDE + t + 1, :]).astype(MXU_DT), w1_ref[0, 1, t])
    h = top + pltpu.roll(bot, nh - 1, 0)
    h = h * jax.nn.sigmoid(h)
    o_ref[0, 0] = _dot(h.astype(MXU_DT), w2_ref[0]).astype(o_ref.dtype)


def _block_diag2(w):
    z = jnp.zeros_like(w)
    return jnp.concatenate([jnp.concatenate([w, z], axis=-1), jnp.concatenate([z, w], axis=-1)], axis=-2)


def nsa_compress(seq_rows, cmp_pos, cmp_w1, cmp_w2):
    b, seq_len = seq_rows.shape[:2]
    nh = seq_len // CMP_STRIDE
    pos = jnp.concatenate([cmp_pos, cmp_pos], axis=-1)
    w1 = cmp_w1.reshape(2, 2, CMP_STRIDE, NSA_HD, CMP_HIDDEN)
    w1 = _block_diag2(w1).astype(MXU_DT)
    w2 = _block_diag2(cmp_w2).astype(MXU_DT)
    return pl.pallas_call(
        _nsa_compress_kernel,
        grid=(b, 2),
        in_specs=[pl.BlockSpec((None, seq_len, LANE), lambda i, j: (i, 0, j)),
                  pl.BlockSpec((1, CMP_BLOCK, LANE), lambda i, j: (j, 0, 0)),
                  pl.BlockSpec((1, 2, CMP_STRIDE, LANE, NSA_KV_HEADS * CMP_HIDDEN), lambda i, j: (j, 0, 0, 0, 0)),
                  pl.BlockSpec((1, NSA_KV_HEADS * CMP_HIDDEN, LANE), lambda i, j: (j, 0, 0))],
        out_specs=pl.BlockSpec((1, 1, nh, LANE), lambda i, j: (i, j, 0, 0)),
        out_shape=jax.ShapeDtypeStruct((b, 2, nh, LANE), MXU_DT),
        compiler_params=pltpu.CompilerParams(dimension_semantics=("arbitrary", "arbitrary"), vmem_limit_bytes=VMEM_LIMIT),
        name="nsa_compress",
    )(seq_rows, pos, w1, w2)


def _pad_heads(q):
    lane = lax.broadcasted_iota(jnp.int32, (q.shape[0], LANE), 1)
    blocks = []
    for h in range(NSA_HEADS):
        blk = q[:, (h // 2) * LANE:(h // 2 + 1) * LANE]
        g = h // NSA_GROUP
        if h % 2 != g:
            blk = pltpu.roll(blk, NSA_HD, 1)
        keep = (lane < NSA_HD) if g == 0 else (lane >= NSA_HD)
        blocks.append(jnp.where(keep, blk, 0.0))
    return jnp.concatenate(blocks, axis=0)


def _softmax_rows(s):
    m = jnp.max(s, axis=-1, keepdims=True)
    m = jnp.where(m == NEG_INF, 0.0, m)
    p = jnp.exp(s - m)
    return p / jnp.maximum(jnp.sum(p, axis=-1, keepdims=True), 1e-30)


def _nsa_attn_kernel(q_ref, gate_ref, c_ref, s1_ref, s2_ref, ck_ref, cv_ref, ks_ref, vs_ref, kn_ref, kw_ref, vw_ref,
                     o_ref, m_ref, l_ref, acc_ref, s_ref, *, q_start, win_start, ns, n_new):
    tq = q_ref.shape[1]
    n_rows = NSA_HEADS * tq
    n_keys = ks_ref.shape[1]
    i = pl.program_id(1)
    q0 = q_start + i * tq
    scale = NSA_HD ** -0.5
    q = q_ref[0] * scale
    c = jnp.concatenate([c_ref[...]] * 4, axis=1)
    s1 = jnp.concatenate([s1_ref[...]] * 4, axis=1)
    s2 = jnp.concatenate([s2_ref[...]] * 4, axis=1)
    q_raw = _pad_heads(q).astype(MXU_DT)
    q_rot = _pad_heads(_rope(q, c, s1, s2)).astype(MXU_DT)

    row = lax.broadcasted_iota(jnp.int32, (n_rows, 1), 0)
    qpos = q0 + (row & (tq - 1))

    nc = ck_ref.shape[1]
    s_c = _dot_nt(q_raw, ck_ref[0])
    c_end = lax.broadcasted_iota(jnp.int32, (1, nc), 1) * CMP_STRIDE + (CMP_BLOCK - 1)
    p_c = _softmax_rows(jnp.where(c_end <= qpos, s_c, NEG_INF))
    o_c = _dot(p_c.astype(MXU_DT), cv_ref[0])

    psum = jnp.concatenate(
        [sum(p_c[(g * NSA_GROUP + j) * tq:(g * NSA_GROUP + j + 1) * tq] for j in range(NSA_GROUP))
         for g in range(NSA_KV_HEADS)], axis=0)
    n_i = lax.broadcasted_iota(jnp.int32, (nc, ns), 0) * CMP_STRIDE
    m_i = lax.broadcasted_iota(jnp.int32, (nc, ns), 1) * SEL_BLOCK
    overlap = ((n_i <= m_i + (SEL_BLOCK - 1)) & (n_i + (CMP_BLOCK - 1) >= m_i)).astype(jnp.float32)
    imp = _dot(psum, overlap, precision=lax.Precision.HIGHEST)
    r2 = lax.broadcasted_iota(jnp.int32, (NSA_KV_HEADS * tq, 1), 0)
    qpos2 = q0 + (r2 & (tq - 1))
    cur = qpos2 >> 6
    blk = lax.broadcasted_iota(jnp.int32, (1, ns), 1)
    forced = (blk == 0) | (blk == cur) | (blk == cur - 1)
    valid = blk * SEL_BLOCK <= qpos2
    v = jnp.where(forced, jnp.inf, jnp.where(valid, imp, NEG_INF))
    blk_f = blk.astype(jnp.float32)
    sel = jnp.zeros(v.shape, jnp.float32)
    for _ in range(SEL_TOPN):
        mx = jnp.max(v, axis=-1, keepdims=True)
        first = jnp.min(jnp.where(v == mx, blk_f, float(ns)), axis=-1, keepdims=True)
        pick = blk_f == first
        sel = jnp.where(pick, 1.0, sel)
        v = jnp.where(pick, NEG_INF, v)
    sel_b = sel.astype(MXU_DT)

    m_ref[...] = jnp.full(m_ref.shape, NEG_INF, jnp.float32)
    l_ref[...] = jnp.zeros(l_ref.shape, jnp.float32)
    acc_ref[...] = jnp.zeros(acc_ref.shape, jnp.float32)
    n_kt = jnp.minimum((q0 + tq + NSA_TK - 1) // NSA_TK, n_keys // NSA_TK)

    def scores(k):
        return _dot_nt(q_rot, k.astype(MXU_DT))

    def update(s, vv, tok0):
        nk = s.shape[1]
        tok = tok0 + lax.broadcasted_iota(jnp.int32, (1, nk), 1)
        e_m = lax.broadcasted_iota(jnp.int32, (ns, nk), 0)
        e_t = (tok0 + lax.broadcasted_iota(jnp.int32, (ns, nk), 1)) >> 6
        chosen = _dot(sel_b, (e_m == e_t).astype(MXU_DT))
        bias2 = jnp.where((chosen > 0.5) & (tok <= qpos2), 0.0, NEG_INF)
        bias = jnp.concatenate([bias2[0:tq]] * NSA_GROUP + [bias2[tq:2 * tq]] * NSA_GROUP, axis=0)
        s = s + bias
        m_old = m_ref[...]
        m_new = jnp.maximum(m_old, jnp.max(s, axis=-1, keepdims=True))
        m_safe = jnp.where(m_new == NEG_INF, 0.0, m_new)
        alpha = jnp.exp(m_old - m_safe)
        p = jnp.exp(s - m_safe)
        l_ref[...] = alpha * l_ref[...] + jnp.sum(p, axis=-1, keepdims=True)
        acc_ref[...] = alpha * acc_ref[...] + _dot(p.astype(MXU_DT), vv.astype(MXU_DT))
        m_ref[...] = m_new

    def body(kt, carry):
        koff = pl.multiple_of(kt * NSA_TK, NSA_TK)
        knext = pl.multiple_of(jnp.minimum(kt + 1, n_kt - 1) * NSA_TK, NSA_TK)
        s = s_ref[...]
        s_next = scores(ks_ref[0, pl.ds(knext, NSA_TK), :])
        update(s, vs_ref[0, pl.ds(koff, NSA_TK), :], koff)
        s_ref[...] = s_next
        return carry

    s_ref[...] = scores(ks_ref[0, 0:NSA_TK, :])
    lax.fori_loop(0, n_kt, body, 0)
    if n_new:
        update(scores(kn_ref[0, :, 0:LANE]), kn_ref[0, :, LANE:2 * LANE], n_keys)
    o_s = acc_ref[...] / jnp.maximum(l_ref[...], 1e-30)

    w0 = jnp.clip(q0 - win_start - WINDOW, 0, kw_ref.shape[1] - NSA_WSPAN)
    w0 = pl.multiple_of(w0, tq)
    kw = kw_ref[0, pl.ds(w0, NSA_WSPAN), :]
    vw = vw_ref[0, pl.ds(w0, NSA_WSPAN), :]
    s_w = _dot_nt(q_rot, kw.astype(MXU_DT))
    kpos = win_start + w0 + lax.broadcasted_iota(jnp.int32, (1, NSA_WSPAN), 1)
    w_ok = (kpos <= qpos) & (kpos > qpos - WINDOW)
    p_w = _softmax_rows(jnp.where(w_ok, s_w, NEG_INF))
    o_w = _dot(p_w.astype(MXU_DT), vw.astype(MXU_DT))

    gates = jax.nn.sigmoid(gate_ref[0])
    lane = lax.broadcasted_iota(jnp.int32, (tq, LANE), 1)
    outs = []
    for h in range(NSA_HEADS):
        sl = slice(h * tq, (h + 1) * tq)
        k0 = 2 * DN_HEADS + 3 * h
        o_h = (gates[:, k0:k0 + 1] * o_c[sl] + gates[:, k0 + 1:k0 + 2] * o_s[sl] + gates[:, k0 + 2:k0 + 3] * o_w[sl])
        if h % 2 != h // NSA_GROUP:
            o_h = pltpu.roll(o_h, NSA_HD, 1)
        outs.append(o_h)
    for p2 in range(NSA_HEADS // 2):
        o_ref[0, :, p2 * LANE:(p2 + 1) * LANE] = jnp.where(lane < NSA_HD, outs[2 * p2], outs[2 * p2 + 1])


def nsa_attend(c_all, q_col0, misc_col0, tabs, ckv, sel_kv, sel_cols, new_kv, n_new, win_kv, win_cols, win_start,
               batch, seq_len, q_start):
    tq = NSA_TQ if seq_len % NSA_TQ == 0 else seq_len
    n_rows = NSA_HEADS * tq
    qb = q_col0 // (4 * LANE)
    mb = misc_col0 // LANE
    c3 = c_all.reshape(batch, seq_len, c_all.shape[1])
    n_blocks = -(-(sel_kv.shape[1] + n_new) // SEL_BLOCK)
    ns = -(-n_blocks // LANE) * LANE
    tab_spec = pl.BlockSpec((tq, LANE), lambda b, i: (i, 0))
    col_spec = lambda a, k: pl.BlockSpec((1, a.shape[1], LANE), lambda b, i, k=k: (b, 0, k))
    nc = ckv.shape[2]
    out = pl.pallas_call(
        functools.partial(_nsa_attn_kernel, q_start=q_start, win_start=win_start, ns=ns, n_new=n_new),
        grid=(batch, seq_len // tq),
        in_specs=[pl.BlockSpec((1, tq, 4 * LANE), lambda b, i: (b, i, qb)),
                  pl.BlockSpec((1, tq, LANE), lambda b, i: (b, i, mb)),
                  tab_spec, tab_spec, tab_spec,
                  pl.BlockSpec((None, 1, nc, LANE), lambda b, i: (b, 0, 0, 0)),
                  pl.BlockSpec((None, 1, nc, LANE), lambda b, i: (b, 1, 0, 0)),
                  col_spec(sel_kv, sel_cols[0]), col_spec(sel_kv, sel_cols[1]),
                  pl.BlockSpec((1, new_kv.shape[1], 2 * LANE), lambda b, i: (b, 0, 0)),
                  col_spec(win_kv, win_cols[0]), col_spec(win_kv, win_cols[1])],
        out_specs=pl.BlockSpec((1, tq, 4 * LANE), lambda b, i: (b, i, 0)),
        out_shape=jax.ShapeDtypeStruct((batch, seq_len, 4 * LANE), jnp.float32),
        scratch_shapes=[pltpu.VMEM((n_rows, 1), jnp.float32), pltpu.VMEM((n_rows, 1), jnp.float32),
                        pltpu.VMEM((n_rows, LANE), jnp.float32), pltpu.VMEM((n_rows, NSA_TK), jnp.float32)],
        compiler_params=pltpu.CompilerParams(dimension_semantics=("arbitrary", "arbitrary"), vmem_limit_bytes=VMEM_LIMIT),
        name="nsa_attn",
    )(c3, c3, *tabs, ckv, ckv, sel_kv, sel_kv, new_kv, win_kv, win_kv)
    return out.reshape(batch * seq_len, 4 * LANE)


PAGES_PER_STEP = 8


def _page_rows_kernel(pt_ref, *refs):
    o_ref = refs[-1]
    for j, x_ref in enumerate(refs[:-1]):
        n_idx, n_g, hd, page = x_ref.shape
        for i in range(n_idx):
            o_ref[0, j * page:(j + 1) * page, i * n_g * hd:(i + 1) * n_g * hd] = x_ref[i].reshape(n_g * hd, page).T


def page_rows(cache_t, page_table, layer):
    _, _, n_idx, n_g, hd, page = cache_t.shape
    b, n_pages = page_table.shape
    w = n_idx * n_g * hd
    pps = PAGES_PER_STEP if n_pages % PAGES_PER_STEP == 0 else 1
    page_spec = lambda j: pl.BlockSpec((None, None, n_idx, n_g, hd, page),
                                       lambda i, p, pt, j=j: (layer, pt[i, p * pps + j], 0, 0, 0, 0))
    return pl.pallas_call(
        _page_rows_kernel,
        grid_spec=pltpu.PrefetchScalarGridSpec(
            num_scalar_prefetch=1,
            grid=(b, n_pages // pps),
            in_specs=[page_spec(j) for j in range(pps)],
            out_specs=pl.BlockSpec((1, pps * page, w), lambda i, p, pt: (i, p, 0))),
        out_shape=jax.ShapeDtypeStruct((b, n_pages * page, w), cache_t.dtype),
        compiler_params=pltpu.CompilerParams(dimension_semantics=("arbitrary", "arbitrary")),
        name="page_rows",
    )(page_table, *([cache_t] * pps))


SCAN_C = 64
SCAN_TT = 512
HALO = 8
GDN_GROUP = 2


def _dot_tn(a, b):
    return lax.dot_general(a, b, (((0,), (0,)), ((), ())), preferred_element_type=jnp.float32)


def _mx(x):
    return x.astype(MXU_DT)


def _dot3(a, b):
    a_hi, b_hi = _mx(a), _mx(b)
    a_lo = _mx(a - a_hi.astype(jnp.float32))
    b_lo = _mx(b - b_hi.astype(jnp.float32))
    return _dot(a_hi, b_hi) + (_dot(a_hi, b_lo) + _dot(a_lo, b_hi))


def _chunk_cumsum(x):
    c = x.shape[0]
    row = lax.broadcasted_iota(jnp.int32, x.shape, 0)
    d = 1
    while d < c:
        x = x + jnp.where(row >= d, pltpu.roll(x, d, 0), 0.0)
        d *= 2
    return x


def _silu(x):
    return x * jax.nn.sigmoid(x)


def _conv_tile(x, halo, w_ref):
    kw = w_ref.shape[0]
    r8 = lax.broadcasted_iota(jnp.int32, (HALO, x.shape[1]), 0)
    y = x * w_ref[kw - 1:kw, :]
    for d in range(1, kw):
        xs = pltpu.roll(x, d, 0)
        head = jnp.where(r8 < d, pltpu.roll(halo, d, 0), xs[0:HALO])
        xs = jnp.concatenate([head, xs[HALO:]], axis=0)
        y = y + xs * w_ref[kw - 1 - d:kw - d, :]
    return y


def _halo_spec(tt, width, col_block):
    return pl.BlockSpec((HALO, width), lambda i: (jnp.maximum(i * (tt // HALO) - 1, 0), col_block))


def _pad_buf(buf):
    return jnp.pad(buf.astype(jnp.float32), ((0, 0), (HALO - buf.shape[1], 0), (0, 0)))


def _hgrn_prep_kernel(q_ref, f_ref, v_ref, la_ref, l1_ref, lbc_ref, oi_ref, qb_ref, kb_ref, eb_ref):
    c = SCAN_C
    tt = q_ref.shape[0]
    row = lax.broadcasted_iota(jnp.int32, (c, LANE), 0)
    ti = lax.broadcasted_iota(jnp.int32, (c, c), 0)
    si = lax.broadcasted_iota(jnp.int32, (c, c), 1)
    for h in range(HG_HEADS):
        ls = slice(h * LANE, (h + 1) * LANE)
        la, l1, lbc = la_ref[:, ls], l1_ref[:, ls], lbc_ref[:, ls]
        for ci in range(tt // c):
            rs = slice(ci * c, (ci + 1) * c)
            q = _silu(q_ref[rs, ls])
            z = f_ref[rs, ls]
            v = v_ref[rs, ls]
            lsig = jnp.minimum(z, 0.0) - jnp.log1p(jnp.exp(-jnp.abs(z)))
            t2 = l1 + lsig
            hi = jnp.maximum(la, t2)
            logf = hi + jnp.log1p(jnp.exp(-jnp.abs(la - t2)))
            k = lbc * jax.nn.sigmoid(-z)
            b = _chunk_cumsum(logf)
            b_last = b[c - 1:c, :]
            qb_ref[rs, ls] = q * jnp.exp(b)
            kb_ref[rs, ls] = k * jnp.exp(b_last - b)
            eb_ref[ci:ci + 1, ls] = jnp.exp(b_last)
            att = jnp.zeros((c, c), jnp.float32)
            n = c // 2
            while n >= 8:
                blk = 2 * n
                ref_rows = jnp.concatenate(
                    [jnp.broadcast_to(b[j * blk + n - 1:j * blk + n, :], (blk, LANE)) for j in range(c // blk)], axis=0)
                upper = (row & (blk - 1)) >= n
                qs = q * jnp.exp(jnp.where(upper, b - ref_rows, 0.0))
                ks = k * jnp.exp(jnp.where(upper, 0.0, ref_rows - b))
                lvl = _dot_nt(_mx(qs), _mx(ks))
                ok = ((ti & ~(blk - 1)) == (si & ~(blk - 1))) & ((ti & (blk - 1)) >= n) & ((si & (blk - 1)) < n)
                att = att + jnp.where(ok, lvl, 0.0)
                n //= 2
            o = _dot(_mx(att), _mx(v))
            for d in range(8):
                if d == 0:
                    w = jnp.sum(q * k, axis=-1, keepdims=True)
                    o = o + w * v
                else:
                    e = jnp.exp(jnp.minimum(b - pltpu.roll(b, d, 0), 0.0))
                    w = jnp.sum(q * pltpu.roll(k, d, 0) * e, axis=-1, keepdims=True)
                    w = jnp.where((row[:, 0:1] & 7) >= d, w, 0.0)
                    o = o + w * pltpu.roll(v, d, 0)
            oi_ref[rs, ls] = o


def hgrn_prep(c_all, col_q, col_f, col_v, lb):
    m = c_all.shape[0]
    tt = SCAN_TT
    lbf = lb.reshape(1, BRANCH_W).astype(jnp.float32)
    la, l1, lbc = jnp.log(lbf), jnp.log1p(-lbf), 1.0 - lbf
    blk = lambda col: pl.BlockSpec((tt, BRANCH_W), lambda i, col=col: (i, col // BRANCH_W))
    vec = pl.BlockSpec((1, BRANCH_W), lambda i: (0, 0))
    out = pl.BlockSpec((tt, BRANCH_W), lambda i: (i, 0))
    return pl.pallas_call(
        _hgrn_prep_kernel,
        grid=(m // tt,),
        in_specs=[blk(col_q), blk(col_f), blk(col_v), vec, vec, vec],
        out_specs=[out, out, out, pl.BlockSpec((tt // SCAN_C, BRANCH_W), lambda i: (i, 0))],
        out_shape=[jax.ShapeDtypeStruct((m, BRANCH_W), jnp.float32)] * 3
                  + [jax.ShapeDtypeStruct((m // SCAN_C, BRANCH_W), jnp.float32)],
        compiler_params=pltpu.CompilerParams(dimension_semantics=("arbitrary",), vmem_limit_bytes=VMEM_LIMIT),
        name="hgrn_prep",
    )(c_all, c_all, c_all, la, l1, lbc)


def _hgrn_scan_kernel(oi_ref, qb_ref, kb_ref, eb_ref, v_ref, g_ref, nw_ref, s0_ref, o_ref, sT_out_ref, sT_ref):
    c = SCAN_C
    nb = oi_ref.shape[0]
    tt = oi_ref.shape[1]
    i = pl.program_id(0)

    @pl.when(i == 0)
    def _():
        sT_ref[...] = s0_ref[...]

    nw = nw_ref[...]
    for ci in range(tt // c):
        rs = slice(ci * c, (ci + 1) * c)
        for b in range(nb):
            for h in range(HG_HEADS):
                ls = slice(h * LANE, (h + 1) * LANE)
                sT = sT_ref[b, h]
                o = oi_ref[b, rs, ls] + _dot_nt(_mx(qb_ref[b, rs, ls]), _mx(sT))
                sT_ref[b, h] = eb_ref[b, ci:ci + 1, ls] * sT + _dot_tn(_mx(v_ref[b, rs, ls]), _mx(kb_ref[b, rs, ls]))
                y = o * lax.rsqrt(jnp.mean(o * o, axis=-1, keepdims=True) + EPS) * nw
                o_ref[b, rs, ls] = y * _silu(g_ref[b, rs, ls])

    @pl.when(i == pl.num_programs(0) - 1)
    def _():
        sT_out_ref[...] = sT_ref[...]


def hgrn_scan(oi, qb, kb, eb, c_all, col_v, col_g, norm_w, s0, batch, seq_len):
    tt = SCAN_TT
    n3 = lambda a: a.reshape(batch, seq_len, a.shape[-1])
    c3 = n3(c_all)
    tok = pl.BlockSpec((batch, tt, BRANCH_W), lambda i: (0, i, 0))
    ctok = lambda col: pl.BlockSpec((batch, tt, BRANCH_W), lambda i, col=col: (0, i, col // BRANCH_W))
    st = pl.BlockSpec((batch, HG_HEADS, LANE, LANE), lambda i: (0, 0, 0, 0))
    o, sT = pl.pallas_call(
        _hgrn_scan_kernel,
        grid=(seq_len // tt,),
        in_specs=[tok, tok, tok, pl.BlockSpec((batch, tt // SCAN_C, BRANCH_W), lambda i: (0, i, 0)),
                  ctok(col_v), ctok(col_g), pl.BlockSpec((1, LANE), lambda i: (0, 0)), st],
        out_specs=[tok, st],
        out_shape=[jax.ShapeDtypeStruct((batch, seq_len, BRANCH_W), jnp.float32),
                   jax.ShapeDtypeStruct((batch, HG_HEADS, LANE, LANE), jnp.float32)],
        scratch_shapes=[pltpu.VMEM((batch, HG_HEADS, LANE, LANE), jnp.float32)],
        compiler_params=pltpu.CompilerParams(dimension_semantics=("arbitrary",), vmem_limit_bytes=VMEM_LIMIT),
        name="hgrn_scan",
    )(n3(oi), n3(qb), n3(kb), eb.reshape(batch, seq_len // SCAN_C, BRANCH_W), c3, c3,
      norm_w.reshape(1, LANE), jnp.swapaxes(s0, -1, -2))
    return o.reshape(batch * seq_len, BRANCH_W), jnp.swapaxes(sT, -1, -2)


def _gdn_prep_kernel(x_ref, halo_ref, buf_ref, misc_ref, cw_ref, alog_ref, dtb_ref,
                     u_ref, w_ref, qg_ref, kg_ref, qk_ref, eg_ref, *, tiles_per_seq):
    c = SCAN_C
    tt = x_ref.shape[0]
    first = (pl.program_id(0) % tiles_per_seq) == 0
    halo = jnp.where(first, buf_ref[0], halo_ref[...])
    y = _silu(_conv_tile(x_ref[...], halo, cw_ref))
    misc = misc_ref[...]
    beta_all = jax.nn.sigmoid(misc)
    sp_in = misc + dtb_ref[...]
    sp = jnp.maximum(sp_in, 0.0) + jnp.log1p(jnp.exp(-jnp.abs(sp_in)))
    g_all = -jnp.exp(alog_ref[...]) * sp
    ti = lax.broadcasted_iota(jnp.int32, (c, c), 0)
    si = lax.broadcasted_iota(jnp.int32, (c, c), 1)
    eye = (ti == si).astype(jnp.float32)
    for cp in range(0, tt // c, GDN_GROUP):
        heads = []
        for ci in range(cp, cp + GDN_GROUP):
            rs = slice(ci * c, (ci + 1) * c)
            gam_all = _chunk_cumsum(g_all[rs])
            gam_t = gam_all.T
            for h in range(DN_HEADS):
                q = y[rs, h * LANE:(h + 1) * LANE]
                k = y[rs, BRANCH_W + h * LANE:BRANCH_W + (h + 1) * LANE]
                v = y[rs, 2 * BRANCH_W + h * LANE:2 * BRANCH_W + (h + 1) * LANE]
                q = q * lax.rsqrt(jnp.sum(q * q, axis=-1, keepdims=True) + EPS) * (DN_DK ** -0.5)
                k = k * lax.rsqrt(jnp.sum(k * k, axis=-1, keepdims=True) + EPS)
                beta = beta_all[rs, h:h + 1]
                gam = gam_all[:, DN_HEADS + h:DN_HEADS + h + 1]
                gam_r = gam_t[DN_HEADS + h:DN_HEADS + h + 1, :]
                decay = jnp.exp(jnp.where(si <= ti, gam - gam_r, NEG_INF))
                kb16 = _mx(k)
                kk = _dot_nt(kb16, kb16)
                a = jnp.where(si < ti, beta * kk * decay, 0.0)
                heads.append((ci, h, q, k, v, beta, gam, decay, kb16, a))
        tinvs = [eye] * len(heads)
        s = 1
        while s < c:
            blk = 2 * s
            off = (((ti & ~(blk - 1)) == (si & ~(blk - 1))) & ((ti & (blk - 1)) >= s) & ((si & (blk - 1)) < s))
            a_offs = [jnp.where(off, hd[-1], 0.0) for hd in heads]
            if s == 1:
                tinvs = [t - ao for t, ao in zip(tinvs, a_offs)]
            else:
                mids = [_dot3(t, ao) for t, ao in zip(tinvs, a_offs)]
                tinvs = [t - _dot3(md, t) for t, md in zip(tinvs, mids)]
            s = blk
        for (ci, h, q, k, v, beta, gam, decay, kb16, a), tinv in zip(heads, tinvs):
            rs = slice(ci * c, (ci + 1) * c)
            ls = slice(h * LANE, (h + 1) * LANE)
            e_gam = jnp.exp(gam)
            rhs = jnp.concatenate([v * beta, k * (beta * e_gam)], axis=1)
            sol = rhs + _dot3(tinv - eye, rhs)
            u_ref[rs, ls] = sol[:, :LANE]
            w_ref[rs, ls] = sol[:, LANE:]
            qg_ref[rs, ls] = q * e_gam
            g_last = gam[c - 1:c, :]
            kg_ref[rs, ls] = k * jnp.exp(g_last - gam)
            qk_ref[rs, h * c:(h + 1) * c] = _dot_nt(_mx(q), kb16) * decay
            eg_ref[ci:ci + 1, ls] = jnp.broadcast_to(jnp.exp(g_last), (1, LANE))


def gdn_prep(c_all, col_x, col_misc, conv_buf, conv_w, a_log, dt_bias, batch, seq_len):
    m = c_all.shape[0]
    tt = SCAN_TT
    tps = seq_len // tt
    xw = 3 * BRANCH_W
    lanes = jnp.zeros((1, LANE), jnp.float32)
    alog = lanes.at[0, DN_HEADS:2 * DN_HEADS].set(a_log.astype(jnp.float32))
    dtb = lanes.at[0, DN_HEADS:2 * DN_HEADS].set(dt_bias.astype(jnp.float32))
    out = pl.BlockSpec((tt, BRANCH_W), lambda i: (i, 0))
    vec = pl.BlockSpec((1, LANE), lambda i: (0, 0))
    return pl.pallas_call(
        functools.partial(_gdn_prep_kernel, tiles_per_seq=tps),
        grid=(m // tt,),
        in_specs=[pl.BlockSpec((tt, xw), lambda i: (i, col_x // xw)),
                  _halo_spec(tt, xw, col_x // xw),
                  pl.BlockSpec((1, HALO, xw), lambda i: (i // tps, 0, 0)),
                  pl.BlockSpec((tt, LANE), lambda i: (i, col_misc // LANE)),
                  pl.BlockSpec((DN_CONV, xw), lambda i: (0, 0)), vec, vec],
        out_specs=[out, out, out, out, pl.BlockSpec((tt, DN_HEADS * SCAN_C), lambda i: (i, 0)),
                   pl.BlockSpec((tt // SCAN_C, BRANCH_W), lambda i: (i, 0))],
        out_shape=[jax.ShapeDtypeStruct((m, BRANCH_W), jnp.float32)] * 4
                  + [jax.ShapeDtypeStruct((m, DN_HEADS * SCAN_C), jnp.float32),
                     jax.ShapeDtypeStruct((m // SCAN_C, BRANCH_W), jnp.float32)],
        compiler_params=pltpu.CompilerParams(dimension_semantics=("arbitrary",), vmem_limit_bytes=VMEM_LIMIT),
        name="gdn_prep",
    )(c_all, c_all, _pad_buf(conv_buf), c_all, conv_w.astype(jnp.float32), alog, dtb)


def _gdn_scan_kernel(u_ref, w_ref, qg_ref, kg_ref, qk_ref, eg_ref, z_ref, nw_ref, s0_ref, o_ref, s_out_ref, s_ref):
    c = SCAN_C
    nb = u_ref.shape[0]
    tt = u_ref.shape[1]
    i = pl.program_id(0)

    @pl.when(i == 0)
    def _():
        s_ref[...] = s0_ref[...]

    nw = nw_ref[...]
    for ci in range(tt // c):
        rs = slice(ci * c, (ci + 1) * c)
        for b in range(nb):
            for h in range(DN_HEADS):
                ls = slice(h * LANE, (h + 1) * LANE)
                s = s_ref[b, h]
                s16 = _mx(s)
                both = _dot(_mx(jnp.concatenate([qg_ref[b, rs, ls], w_ref[b, rs, ls]], axis=0)), s16)
                v_new = u_ref[b, rs, ls] - both[c:]
                v16 = _mx(v_new)
                o = both[:c] + _dot(_mx(qk_ref[b, rs, h * c:(h + 1) * c]), v16)
                s_ref[b, h] = eg_ref[b, ci:ci + 1, ls][:, 0:1] * s + _dot_tn(_mx(kg_ref[b, rs, ls]), v16)
                y = o * lax.rsqrt(jnp.mean(o * o, axis=-1, keepdims=True) + EPS) * nw
                o_ref[b, rs, ls] = y * _silu(z_ref[b, rs, ls])

    @pl.when(i == pl.num_programs(0) - 1)
    def _():
        s_out_ref[...] = s_ref[...]


def gdn_scan(u, w, qg, kg, qk, eg, c_all, col_z, norm_w, s0, batch, seq_len):
    tt = SCAN_TT
    n3 = lambda a: a.reshape(batch, seq_len, a.shape[-1])
    tok = pl.BlockSpec((batch, tt, BRANCH_W), lambda i: (0, i, 0))
    st = pl.BlockSpec((batch, DN_HEADS, LANE, LANE), lambda i: (0, 0, 0, 0))
    o, s = pl.pallas_call(
        _gdn_scan_kernel,
        grid=(seq_len // tt,),
        in_specs=[tok, tok, tok, tok, pl.BlockSpec((batch, tt, DN_HEADS * SCAN_C), lambda i: (0, i, 0)),
                  pl.BlockSpec((batch, tt // SCAN_C, BRANCH_W), lambda i: (0, i, 0)),
                  pl.BlockSpec((batch, tt, BRANCH_W), lambda i: (0, i, col_z // BRANCH_W)),
                  pl.BlockSpec((1, LANE), lambda i: (0, 0)), st],
        out_specs=[tok, st],
        out_shape=[jax.ShapeDtypeStruct((batch, seq_len, BRANCH_W), jnp.float32),
                   jax.ShapeDtypeStruct((batch, DN_HEADS, LANE, LANE), jnp.float32)],
        scratch_shapes=[pltpu.VMEM((batch, DN_HEADS, LANE, LANE), jnp.float32)],
        compiler_params=pltpu.CompilerParams(dimension_semantics=("arbitrary",), vmem_limit_bytes=VMEM_LIMIT),
        name="gdn_scan",
    )(n3(u), n3(w), n3(qg), n3(kg), n3(qk), eg.reshape(batch, seq_len // SCAN_C, BRANCH_W), n3(c_all),
      norm_w.reshape(1, LANE), s0.astype(jnp.float32))
    return o.reshape(batch * seq_len, BRANCH_W), s


def _sc_branch_kernel(x_ref, halo_ref, buf_ref, w_ref, o_ref, *, tiles_per_seq):
    first = (pl.program_id(0) % tiles_per_seq) == 0
    x = x_ref[...]
    hr = halo_ref[...]
    halo = jnp.where(first, buf_ref[0], hr[:, SC_W:2 * SC_W] * hr[:, 2 * SC_W:3 * SC_W])
    o_ref[...] = x[:, 0:SC_W] * _conv_tile(x[:, SC_W:2 * SC_W] * x[:, 2 * SC_W:3 * SC_W], halo, w_ref)


def sc_branch(c_all, col_x, conv_buf, conv_w, seq_len):
    m = c_all.shape[0]
    tt = SCAN_TT
    tps = seq_len // tt
    xw = 3 * SC_W
    return pl.pallas_call(
        functools.partial(_sc_branch_kernel, tiles_per_seq=tps),
        grid=(m // tt,),
        in_specs=[pl.BlockSpec((tt, xw), lambda i: (i, col_x // xw)),
                  _halo_spec(tt, xw, col_x // xw),
                  pl.BlockSpec((1, HALO, SC_W), lambda i: (i // tps, 0, 0)),
                  pl.BlockSpec((SC_CONV, SC_W), lambda i: (0, 0))],
        out_specs=pl.BlockSpec((tt, SC_W), lambda i: (i, 0)),
        out_shape=jax.ShapeDtypeStruct((m, SC_W), jnp.float32),
        compiler_params=pltpu.CompilerParams(dimension_semantics=("arbitrary",), vmem_limit_bytes=VMEM_LIMIT),
        name="sc_branch",
    )(c_all, c_all, _pad_buf(conv_buf), conv_w.astype(jnp.float32))


def rms_norm(x, g):
    xf = x.astype(jnp.float32)
    y = xf * lax.rsqrt(jnp.mean(xf * xf, axis=-1, keepdims=True) + EPS)
    return (y * g.astype(jnp.float32)).astype(x.dtype)


def l2_normalize(x):
    xf = x.astype(jnp.float32)
    return xf * lax.rsqrt(jnp.sum(xf * xf, axis=-1, keepdims=True) + EPS)


def causal_conv(x, buf, w):
    k_w = w.shape[0]
    seq_len = x.shape[1]
    xp = jnp.concatenate([buf.astype(x.dtype), x], axis=1)
    y = sum(xp[:, j:j + seq_len] * w[j] for j in range(k_w))
    return y, xp[:, seq_len:]


def to_chunks(a, c):
    b, seq_len = a.shape[:2]
    n = -(-seq_len // c)
    a = jnp.pad(a, [(0, 0), (0, n * c - seq_len)] + [(0, 0)] * (a.ndim - 2))
    return jnp.moveaxis(a.reshape((b, n, c) + a.shape[2:]), 1, 0)


def from_chunks(a, seq_len):
    n, b, c = a.shape[:3]
    return jnp.moveaxis(a, 0, 1).reshape((b, n * c) + a.shape[3:])[:, :seq_len]


def hgrn2_scan(q, k, v, logf, s0):
    seq_len = q.shape[1]
    c = min(HG_CHUNK, seq_len)
    tri = jnp.tril(jnp.ones((c, c), bool))[None, :, :, None, None]

    def step(s, inp):
        qc, kc, vc, gc = inp
        b = jnp.cumsum(gc, axis=1)
        decay = jnp.exp(jnp.where(tri, b[:, :, None] - b[:, None], -jnp.inf))
        att = jnp.einsum('bthk,btshk->btsh', qc, decay * kc[:, None])
        o = jnp.einsum('btsh,bshv->bthv', att, vc) + jnp.einsum('bthk,bhkv->bthv', qc * jnp.exp(b), s)
        b_last = b[:, -1]
        s = jnp.exp(b_last)[..., None] * s + jnp.einsum('bshk,bshv->bhkv', kc * jnp.exp(b_last[:, None] - b), vc)
        return s, o

    xs = tuple(to_chunks(a.astype(jnp.float32), c) for a in (q, k, v, logf))
    s, o = lax.scan(step, s0.astype(jnp.float32), xs)
    return from_chunks(o, seq_len), s


def gated_delta_scan(q, k, v, beta, g, s0):
    seq_len = q.shape[1]
    c = min(DN_CHUNK, seq_len)
    incl = jnp.tril(jnp.ones((c, c), bool))
    strict = jnp.tril(jnp.ones((c, c), bool), -1)
    eye = jnp.eye(c, dtype=jnp.float32)

    def step(s, inp):
        qc, kc, vc, bc, gc = inp
        qh, kh, vh = (jnp.swapaxes(a, 1, 2) for a in (qc, kc, vc))
        bh = jnp.swapaxes(bc, 1, 2)
        gam = jnp.cumsum(jnp.swapaxes(gc, 1, 2), axis=-1)
        decay = jnp.exp(jnp.where(incl, gam[..., :, None] - gam[..., None, :], -jnp.inf))
        kk = jnp.einsum('bhtk,bhsk->bhts', kh, kh)
        t_mat = eye + jnp.where(strict, bh[..., :, None] * kk * decay, 0.0)
        u = lax.linalg.triangular_solve(t_mat, vh * bh[..., None], left_side=True, lower=True)
        w = lax.linalg.triangular_solve(t_mat, kh * (bh * jnp.exp(gam))[..., None], left_side=True, lower=True)
        v_new = u - jnp.einsum('bhtk,bhkv->bhtv', w, s)
        qk = jnp.einsum('bhtk,bhsk->bhts', qh, kh) * decay
        o = jnp.einsum('bhtk,bhkv->bhtv', qh * jnp.exp(gam)[..., None], s) + jnp.einsum('bhts,bhsv->bhtv', qk, v_new)
        g_last = gam[..., -1]
        s = jnp.exp(g_last)[..., None, None] * s + jnp.einsum('bhtk,bhtv->bhkv', kh * jnp.exp(g_last[..., None] - gam)[..., None], v_new)
        return s, jnp.swapaxes(o, 1, 2)

    xs = tuple(to_chunks(a.astype(jnp.float32), c) for a in (q, k, v, beta, g))
    s, o = lax.scan(step, s0.astype(jnp.float32), xs)
    return from_chunks(o, seq_len), s


def trunk_layer(x, start, hg_s, dn_s, dn_buf, sc_buf, past_rows, win_buf, w):
    b, seq_len, _ = x.shape
    f32 = jnp.float32
    x2 = x.reshape(b * seq_len, D_MODEL)
    m = b * seq_len
    c2 = _norm_proj(x2, w["norm_mix"], w["w_in"])
    c_all = c2.reshape(b, seq_len, N_IN_PAD)
    lb = w["lb"]
    if past_rows is None:
        oi, qb, kb, eb = hgrn_prep(c2, C_OFF["hg_q"], C_OFF["hg_f"], C_OFF["hg_i"], lb)
        o_a, hg_s = hgrn_scan(oi, qb, kb, eb, c2, C_OFF["hg_i"], C_OFF["hg_g"], w["hg_norm"], hg_s, b, seq_len)
        u, wy, qg, kg, qk, eg = gdn_prep(c2, C_OFF["dn_qkv"], C_OFF["misc"], dn_buf, w["dn_conv"], w["dn_a_log"],
                                         w["dn_dt_bias"], b, seq_len)
        o_b, dn_s = gdn_scan(u, wy, qg, kg, qk, eg, c2, C_OFF["dn_z"], w["dn_norm"], dn_s, b, seq_len)
        o_c = sc_branch(c2, C_OFF["sc_bch"], sc_buf, w["sc_conv"], seq_len)
        tail = c_all[:, seq_len - (DN_CONV - 1):]
        dn_buf = tail[..., C_OFF["dn_qkv"]:C_OFF["dn_qkv"] + 3 * BRANCH_W]
        sc_t = tail[:, DN_CONV - SC_CONV:, C_OFF["sc_bch"] + SC_W:C_OFF["sc_bch"] + 3 * SC_W]
        sc_buf = sc_t[..., :SC_W] * sc_t[..., SC_W:]
    else:
        o_a, o_b, o_c, (hg_s, dn_s, dn_buf, sc_buf) = cached_mixers(x, c_all, hg_s, dn_s, dn_buf, sc_buf, w)
    tabs = rope_tables(start + jnp.arange(seq_len))
    rows, win_rows, kvb = nsa_prep(c2, C_OFF["nsa_kv"], tabs, seq_len)
    rows = rows.reshape(b, seq_len, 4 * LANE)
    win_rows = win_rows.reshape(b, seq_len, 2 * LANE)
    kvb = kvb.reshape(b, seq_len, 4 * LANE)
    nsa_cols = (c2, C_OFF["nsa_q"], C_OFF["misc"], tabs)
    cmp_w = (w["cmp_pos"], w["cmp_w1"], w["cmp_w2"])
    if past_rows is None:
        ckv = nsa_compress(rows, *cmp_w)
        no_new = jnp.zeros((b, LANE, 2 * LANE), MXU_DT)
        o_d = nsa_attend(*nsa_cols, ckv, kvb, (0, 1), no_new, 0, kvb, (2, 3), 0, b, seq_len, 0)
        new_win = win_rows[:, max(seq_len - WINDOW, 0):]
    else:
        n_past = past_rows.shape[1]
        assert n_past % CMP_STRIDE == 0 and seq_len < CMP_STRIDE
        past = past_rows.reshape(b, n_past, 4 * LANE)
        ckv = nsa_compress(past, *cmp_w)
        new_kv = jnp.pad(kvb[:, :, 0:2 * LANE], ((0, 0), (0, LANE - seq_len), (0, 0)))
        nb = win_buf.shape[1]
        w_all = jnp.concatenate([win_buf.reshape(b, nb, 2 * LANE).astype(jnp.float32), win_rows], axis=1)
        w_pad = jnp.pad(w_all, ((0, 0), (0, max(NSA_WSPAN - nb - seq_len, 0)), (0, 0)))
        o_d = nsa_attend(*nsa_cols, ckv, past, (2, 3), new_kv, seq_len, w_pad, (0, 1), start - nb, b, seq_len, start)
        new_win = w_all[:, seq_len:]
    rows = rows.reshape(b, seq_len, 4, NSA_KV_HEADS, NSA_HD)
    new_win = new_win.reshape(b, new_win.shape[1], 2, NSA_KV_HEADS, NSA_HD)
    branches = [o_a.reshape(m, BRANCH_W), o_b.reshape(m, BRANCH_W), o_c.reshape(m, BRANCH_W), o_d.reshape(m, BRANCH_W)]
    x2 = _merge_out(x2, branches, c2, C_OFF["merge_gate"], w["w_branch"], w["w_out"])
    x2 = _mlp(x2, w["norm_mlp"], w["w_up"], w["w_down"])
    return x2.reshape(b, seq_len, D_MODEL), (hg_s, dn_s, dn_buf, sc_buf, new_win, rows)


def cached_mixers(x, c_all, hg_s, dn_s, dn_buf, sc_buf, w):
    b, seq_len, _ = x.shape
    f32 = jnp.float32
    c = {n: c_all[..., C_OFF[n]:C_OFF[n] + wd] for n, wd in C_ORDER}
    for n, o0, wd in MISC_COLS:
        c[n] = c["misc"][..., o0:o0 + wd]
    lb = w["lb"]
    hq = jax.nn.silu(c["hg_q"]).reshape(b, seq_len, HG_HEADS, HG_DK)
    z = c["hg_f"].astype(f32).reshape(b, seq_len, HG_HEADS, HG_DK)
    logf = jnp.logaddexp(jnp.log(lb), jnp.log1p(-lb) + jax.nn.log_sigmoid(z))
    hk = (1.0 - lb) * jax.nn.sigmoid(-z)
    hv = c["hg_i"].reshape(b, seq_len, HG_HEADS, HG_DV)
    o_a, hg_s = hgrn2_scan(hq, hk, hv, logf, hg_s)
    o_a = rms_norm(o_a.astype(x.dtype), w["hg_norm"]) * jax.nn.silu(c["hg_g"].reshape(b, seq_len, HG_HEADS, HG_DV))
    qkv, dn_buf = causal_conv(c["dn_qkv"], dn_buf, w["dn_conv"])
    dq, dk, dv = jnp.split(jax.nn.silu(qkv), 3, axis=-1)
    dq = l2_normalize(dq.reshape(b, seq_len, DN_HEADS, DN_DK)) * DN_DK ** -0.5
    dk = l2_normalize(dk.reshape(b, seq_len, DN_HEADS, DN_DK))
    beta = jax.nn.sigmoid(c["dn_b"].astype(f32))
    g = -jnp.exp(w["dn_a_log"].astype(f32)) * jax.nn.softplus(c["dn_a"].astype(f32) + w["dn_dt_bias"])
    o_b, dn_s = gated_delta_scan(dq, dk, dv.reshape(b, seq_len, DN_HEADS, DN_DV), beta, g, dn_s)
    o_b = rms_norm(o_b.astype(x.dtype), w["dn_norm"]) * jax.nn.silu(c["dn_z"].reshape(b, seq_len, DN_HEADS, DN_DV))
    gb, gc, hx = jnp.split(c["sc_bch"], 3, axis=-1)
    conv, sc_buf = causal_conv(gc * hx, sc_buf, w["sc_conv"])
    o_c = gb * conv
    return o_a, o_b, o_c, (hg_s, dn_s, dn_buf, sc_buf)


def kernel(x_prompt, x_sample, state_hgrn, state_dn, state_dn_conv, state_sc_conv, state_win_kv, cache_kv, page_table, norm_mix, norm_mlp, norm_final, w_in, hg_lb_logits, hg_norm, dn_conv, dn_a_log, dn_dt_bias, dn_norm, sc_conv, cmp_pos, cmp_w1, cmp_w2, w_branch, w_out, w_up, w_down):
    f32 = jnp.float32
    bf16 = jnp.bfloat16
    lbs = jnp.cumsum(jax.nn.softmax(hg_lb_logits.astype(f32), axis=0), axis=0)
    lbs = lbs - lbs[:1]
    w_in_b = _permute_w_in(w_in).astype(bf16)
    w_branch_b, w_out_b, w_up_b, w_down_b = (a.astype(bf16) for a in (w_branch, w_out, w_up, w_down))

    def layer_w(l):
        return dict(norm_mix=norm_mix[l], norm_mlp=norm_mlp[l], w_in=w_in_b[l], lb=lbs[l], hg_norm=hg_norm[l],
                    dn_conv=dn_conv[l], dn_a_log=dn_a_log[l], dn_dt_bias=dn_dt_bias[l], dn_norm=dn_norm[l],
                    sc_conv=sc_conv[l], cmp_pos=cmp_pos[l], cmp_w1=cmp_w1[l], cmp_w2=cmp_w2[l],
                    w_branch=w_branch_b[l], w_out=w_out_b[l], w_up=w_up_b[l], w_down=w_down_b[l])

    bp = x_prompt.shape[0]
    yp = x_prompt
    p_st = []
    for l in range(DEPTH):
        yp, st = trunk_layer(yp, 0,
                             jnp.zeros((bp, HG_HEADS, HG_DK, HG_DV), f32),
                             jnp.zeros((bp, DN_HEADS, DN_DK, DN_DV), f32),
                             jnp.zeros((bp, DN_CONV - 1, 3 * BRANCH_W), x_prompt.dtype),
                             jnp.zeros((bp, SC_CONV - 1, SC_W), x_prompt.dtype),
                             None, None, layer_w(l))
        p_st.append(st)
    ys = x_sample
    s_st = []
    cache_t = jnp.transpose(cache_kv, (0, 1, 3, 4, 5, 2))
    for l in range(DEPTH):
        past = page_rows(cache_t, page_table, l)
        ys, st = trunk_layer(ys, past.shape[1], state_hgrn[l], state_dn[l], state_dn_conv[l], state_sc_conv[l],
                             past, state_win_kv[l], layer_w(l))
        s_st.append(st)
    p = [jnp.stack([st[i] for st in p_st]) for i in range(6)]
    s = [jnp.stack([st[i] for st in s_st]) for i in range(6)]
    return (rms_norm(yp, norm_final), rms_norm(ys, norm_final),
            p[0], p[1], p[2], p[3], p[4], p[5],
            s[0], s[1], s[2], s[3], s[4], s[5])
```
